```python
import math
import jax
import jax.numpy as jnp
from jax import lax
import numpy as np

D_MODEL = 4096
BATCH = 32
SEQ = 256
DEPTH = 4
DEC_BATCH = 8
DEC_SEQ = 1024
PAST_LEN = 256

F32 = jnp.float32
GRID_W = 64
Q_BLOCK = 128
NORM_EPS = 1e-6

ATT_HEADS = 12
ATT_QK_DIM = 64
ATT_V_DIM = 128
ATT_WIDTH = ATT_HEADS * ATT_V_DIM
ATT_QK_COLS = ATT_HEADS * 2 * ATT_QK_DIM
ROPE_BASE = 10000.0

SSD_HEADS = 24
SSD_HEAD_DIM = 64
SSD_WIDTH = SSD_HEADS * SSD_HEAD_DIM
SSD_GROUPS = 4
SSD_STATE = 128
SSD_CHUNK = 128
SSD_CONV_DIM = SSD_WIDTH + 2 * SSD_GROUPS * SSD_STATE

MIX_WIDTH = D_MODEL
HYENA_WIDTH = MIX_WIDTH - ATT_WIDTH - SSD_WIDTH
HYENA_ORDER = 2
HYENA_BANDS = 16
HYENA_EMB_DIM = 1 + 2 * HYENA_BANDS
HYENA_FILTER_DIM = 64
HYENA_FREQ_BASE = 10000.0
HYENA_MIN_DECAY = -math.log(1e-2) / 1.5
HYENA_MAX_DECAY = -math.log(1e-2) / 0.3

SHORT_CONV = 3
D_FF = 256 * ((8 * D_MODEL // 3 + 255) // 256)

IN_SIZES = (ATT_QK_COLS, ATT_QK_COLS, ATT_WIDTH, SSD_WIDTH, SSD_CONV_DIM, 2 * SSD_HEADS, (HYENA_ORDER + 1) * HYENA_WIDTH)
IN_COLS = sum(IN_SIZES)
IN_SPLITS = tuple(int(s) for s in np.cumsum(IN_SIZES)[:-1])

kernel_name = 'hybrid_diffusion_prefix_step'


def rmsnorm(x, g):
    xf = x.astype(F32)
    y = xf * lax.rsqrt(jnp.mean(xf * xf, axis=-1, keepdims=True) + NORM_EPS)
    return (y * g.astype(F32)).astype(x.dtype)


def short_conv(x, w, b):
    L = x.shape[1]
    pad = SHORT_CONV // 2
    xp = jnp.pad(x, ((0, 0), (pad, pad), (0, 0)))
    y = xp[:, 0:L] * w[0] + b
    for j in range(1, SHORT_CONV):
        y = y + xp[:, j:j + L] * w[j]
    return y


def axial_rope(L):
    rows = L // GRID_W
    r, col = jnp.meshgrid(jnp.arange(rows), jnp.arange(GRID_W), indexing='ij')
    npair = ATT_QK_DIM // 4
    inv = ROPE_BASE ** (-jnp.arange(npair, dtype=F32) / npair)
    ang = jnp.concatenate([r.reshape(-1, 1).astype(F32) * inv, col.reshape(-1, 1).astype(F32) * inv], axis=-1)
    return jnp.cos(ang), jnp.sin(ang)


def apply_rope(x, cos, sin):
    xr = x.reshape(*x.shape[:-1], ATT_QK_DIM // 2, 2)
    xe, xo = xr[..., 0], xr[..., 1]
    c = cos[None, :, None, None, :].astype(x.dtype)
    s = sin[None, :, None, None, :].astype(x.dtype)
    return jnp.stack([xe * c - xo * s, xe * s + xo * c], axis=-1).reshape(x.shape)


def diff_lambda(lv, lam_init):
    lv = lv.astype(F32)
    return jnp.exp(jnp.sum(lv[0] * lv[1])) - jnp.exp(jnp.sum(lv[2] * lv[3])) + lam_init


def diff_attention(q, k, v, lam):
    b, L = q.shape[0], q.shape[1]
    nb = L // Q_BLOCK
    scale = ATT_QK_DIM ** -0.5

    def block(qb):
        s = jnp.einsum('bqhmd,bkhmd->bhmqk', qb, k).astype(F32) * scale
        p = jax.nn.softmax(s, axis=-1)
        w = (p[:, :, 0] - lam * p[:, :, 1]).astype(v.dtype)
        return jnp.einsum('bhqk,bkhd->bqhd', w, v)

    qb = q.reshape(b, nb, Q_BLOCK, ATT_HEADS, 2, ATT_QK_DIM).swapaxes(0, 1)
    out = lax.map(block, qb)
    return out.swapaxes(0, 1).reshape(b, L, ATT_HEADS, ATT_V_DIM)


def _segsum_exp(cs):
    T = cs.shape[-1]
    diff = cs[..., :, None] - cs[..., None, :]
    mask = jnp.tril(jnp.ones((T, T), dtype=bool))
    return jnp.exp(jnp.where(mask, diff, -jnp.inf))


def ssd_scan(x, dt, a, bm, cm, init):
    b, L, h, p = x.shape
    nc = L // SSD_CHUNK
    rep = h // bm.shape[2]

    def chunk(t):
        return t.reshape(b, nc, SSD_CHUNK, *t.shape[2:])

    xc = chunk(x * dt[..., None])
    bc = chunk(jnp.repeat(bm, rep, axis=2))
    cc = chunk(jnp.repeat(cm, rep, axis=2))
    a_cs = jnp.cumsum(jnp.moveaxis(chunk(dt * a), -1, 1), axis=-1)
    cb = jnp.einsum('bclhn,bcshn->bhcls', cc, bc) * _segsum_exp(a_cs)
    y_diag = jnp.einsum('bhcls,bcshp->bclhp', cb, xc)
    decay_states = jnp.exp(a_cs[..., -1:] - a_cs)
    states = jnp.einsum('bclhn,bhcl,bclhp->bchpn', bc, decay_states, xc)
    states = jnp.concatenate([init[:, None], states], axis=1)
    chunk_cs = jnp.cumsum(jnp.pad(a_cs[..., -1], ((0, 0), (0, 0), (1, 0))), axis=-1)
    new_states = jnp.einsum('bhzc,bchpn->bzhpn', _segsum_exp(chunk_cs), states)
    y_off = jnp.einsum('bclhn,bchpn,bhcl->bclhp', cc, new_states[:, :-1], jnp.exp(a_cs))
    return (y_diag + y_off).reshape(b, L, h, p), new_states[:, -1]


def ssd_mixer(z, xbc, dt_raw, lw, init_f, init_b):
    b, L, _ = xbc.shape
    xbc = jax.nn.silu(short_conv(xbc, lw['ssd_conv_w'], lw['ssd_conv_b']))
    xs, bm, cm = jnp.split(xbc, [SSD_WIDTH, SSD_WIDTH + SSD_GROUPS * SSD_STATE], axis=-1)
    xs = xs.reshape(b, L, SSD_HEADS, SSD_HEAD_DIM).astype(F32)
    bm = bm.reshape(b, L, SSD_GROUPS, SSD_STATE).astype(F32)
    cm = cm.reshape(b, L, SSD_GROUPS, SSD_STATE).astype(F32)
    dt = jax.nn.softplus(dt_raw.astype(F32).reshape(b, L, 2, SSD_HEADS) + lw['ssd_dt_bias'].astype(F32))
    a = -jnp.exp(lw['ssd_a_log'].astype(F32))
    zeros = jnp.zeros((b, SSD_HEADS, SSD_HEAD_DIM, SSD_STATE), F32)
    init_f = zeros if init_f is None else init_f.astype(F32)
    init_b = zeros if init_b is None else init_b.astype(F32)
    y_f, s_f = ssd_scan(xs, dt[:, :, 0], a[0], bm, cm, init_f)
    y_b, s_b = ssd_scan(jnp.flip(xs, 1), jnp.flip(dt[:, :, 1], 1), a[1], jnp.flip(bm, 1), jnp.flip(cm, 1), init_b)
    y = y_f + jnp.flip(y_b, 1) + lw['ssd_d'].astype(F32)[:, None] * xs
    y = y.reshape(b, L, SSD_WIDTH).astype(z.dtype)
    y = rmsnorm(y * jax.nn.silu(z), lw['ssd_norm_g'])
    return y, s_f.astype(z.dtype), s_b.astype(z.dtype)


def hyena_filters(L, lw):
    off = jnp.arange(L, dtype=F32) - (L // 2)
    band = HYENA_FREQ_BASE ** (-jnp.arange(HYENA_BANDS, dtype=F32) / HYENA_BANDS)
    ang = off[:, None] * band
    feats = jnp.concatenate([off[:, None] / L, jnp.sin(ang), jnp.cos(ang)], axis=-1)
    freq = lw['hy_freq'].astype(F32)
    hdn = jnp.sin(freq[0] * (feats @ lw['hy_w1'].astype(F32) + lw['hy_b1'].astype(F32)))
    hdn = jnp.sin(freq[1] * (hdn @ lw['hy_w2'].astype(F32) + lw['hy_b2'].astype(F32)))
    filt = (hdn @ lw['hy_w3'].astype(F32)).reshape(L, HYENA_ORDER, HYENA_WIDTH)
    dist = jnp.abs(off) * (2.0 / L)
    deltas = jnp.linspace(HYENA_MIN_DECAY, HYENA_MAX_DECAY, HYENA_WIDTH, dtype=F32)
    window = jnp.exp(-dist[:, None] * deltas[None, :])
    return filt * window[:, None, :]


def fft_long_conv(u, h, bias):
    L = u.shape[1]
    uf = jnp.fft.rfft(u.astype(F32), n=2 * L, axis=1)
    hf = jnp.fft.rfft(h, n=2 * L, axis=0)
    y = jnp.fft.irfft(uf * hf[None], n=2 * L, axis=1)[:, L // 2:L // 2 + L]
    return (y + u.astype(F32) * bias.astype(F32)).astype(u.dtype)


def hyena_mixer(u, lw):
    L = u.shape[1]
    u = short_conv(u, lw['hy_conv_w'], lw['hy_conv_b'])
    parts = jnp.split(u, HYENA_ORDER + 1, axis=-1)
    filt = hyena_filters(L, lw)
    zz = parts[-1]
    for n in range(HYENA_ORDER):
        zz = parts[n] * fft_long_conv(zz, filt[:, n], lw['hy_bias'][n])
    return rmsnorm(zz, lw['hy_norm_g'])


def token_mixers(h, lw, lam, lam_init, rope, ctx):
    b, L, _ = h.shape
    q, k, v, z, xbc, dt_raw, hy = jnp.split(h @ lw['w_in'], IN_SPLITS, axis=-1)
    q = q.reshape(b, L, ATT_HEADS, 2, ATT_QK_DIM)
    k = k.reshape(b, L, ATT_HEADS, 2, ATT_QK_DIM)
    v = v.reshape(b, L, ATT_HEADS, ATT_V_DIM)
    if ctx is None:
        keys, vals, init_f, init_b = k, v, None, None
    else:
        k_ctx, v_ctx, init_f, init_b = ctx
        cos, sin = rope
        q = apply_rope(q, cos, sin)
        keys = jnp.concatenate([apply_rope(k, cos, sin), k_ctx.reshape(b, -1, ATT_HEADS, 2, ATT_QK_DIM)], axis=1)
        vals = jnp.concatenate([v, v_ctx], axis=1)
    att = rmsnorm(diff_attention(q, keys, vals, lam), lw['att_subln_g']) * (1.0 - lam_init)
    y_ssd, s_f, s_b = ssd_mixer(z, xbc, dt_raw, lw, init_f, init_b)
    y_hy = hyena_mixer(hy, lw)
    out = jnp.concatenate([att.reshape(b, L, ATT_WIDTH), y_ssd, y_hy], axis=-1) @ lw['w_out']
    if ctx is None:
        return out, (k.reshape(b, L, ATT_HEADS, 2 * ATT_QK_DIM), v, s_f, s_b)
    return out, None


def conv_ffn(h, lw):
    u = short_conv(h @ lw['ffn_w_up'], lw['ffn_conv_w'], lw['ffn_conv_b'])
    gate, val = jnp.split(u, 2, axis=-1)
    return (jax.nn.silu(gate) * val) @ lw['ffn_w_down']


def trunk_layer(x, cond, lw, lam, lam_init, rope, ctx):
    mod = jax.nn.silu(cond) @ lw['w_ada'] + lw['b_ada']
    mod = mod.reshape(-1, 1, 6 * D_MODEL).astype(x.dtype)
    sh1, sc1, g1, sh2, sc2, g2 = jnp.split(mod, 6, axis=-1)
    h = rmsnorm(x, lw['norm1_g']) * (1 + sc1) + sh1
    mix, ctx_out = token_mixers(h, lw, lam, lam_init, rope, ctx)
    x = x + g1 * mix
    h = rmsnorm(x, lw['norm2_g']) * (1 + sc2) + sh2
    x = x + g2 * conv_ffn(h, lw)
    return x, ctx_out


def setup_inputs(seed: int = 0) -> dict:
    key = jax.random.key(seed)
    keys = list(jax.random.split(key, 40))

    def nrm(shape, scale=1.0):
        return jax.random.normal(keys.pop(), shape, F32) * scale

    def gain(shape):
        return 1.0 + nrm(shape, 0.02)

    D = D_MODEL
    dt0 = jnp.exp(jax.random.uniform(keys.pop(), (DEPTH, 2, SSD_HEADS), F32, math.log(1e-3), math.log(1e-1)))
    a0 = jax.random.uniform(keys.pop(), (DEPTH, 2, SSD_HEADS), F32, 1.0, 16.0)
    return {
        'x_prompt': nrm((BATCH, SEQ, D)),
        'x_sample': nrm((DEC_BATCH, DEC_SEQ, D)),
        'cache_k': nrm((DEC_BATCH, DEPTH, PAST_LEN, ATT_HEADS, 2 * ATT_QK_DIM)),
        'cache_v': nrm((DEC_BATCH, DEPTH, PAST_LEN, ATT_HEADS, ATT_V_DIM)),
        'state_ssd_fwd': nrm((DEC_BATCH, DEPTH, SSD_HEADS, SSD_HEAD_DIM, SSD_STATE), 0.3),
        'state_ssd_bwd': nrm((DEC_BATCH, DEPTH, SSD_HEADS, SSD_HEAD_DIM, SSD_STATE), 0.3),
        'c': nrm((DEC_BATCH, D)),
        'c_ctx': nrm((D,)),
        'norm1_g': gain((DEPTH, D)),
        'norm2_g': gain((DEPTH, D)),
        'w_ada': nrm((DEPTH, D, 6 * D), 0.5 * D ** -0.5),
        'b_ada': nrm((DEPTH, 6 * D), 0.01),
        'w_in': nrm((DEPTH, D, IN_COLS), D ** -0.5),
        'att_lambda': nrm((DEPTH, 4, ATT_QK_DIM), 0.1),
        'att_subln_g': gain((DEPTH, ATT_V_DIM)),
        'ssd_conv_w': nrm((DEPTH, SHORT_CONV, SSD_CONV_DIM), SHORT_CONV ** -0.5),
        'ssd_conv_b': nrm((DEPTH, SSD_CONV_DIM), 0.01),
        'ssd_dt_bias': dt0 + jnp.log(-jnp.expm1(-dt0)),
        'ssd_a_log': jnp.log(a0),
        'ssd_d': 1.0 + nrm((DEPTH, SSD_HEADS), 0.1),
        'ssd_norm_g': gain((DEPTH, SSD_WIDTH)),
        'hy_conv_w': nrm((DEPTH, SHORT_CONV, (HYENA_ORDER + 1) * HYENA_WIDTH), SHORT_CONV ** -0.5),
        'hy_conv_b': nrm((DEPTH, (HYENA_ORDER + 1) * HYENA_WIDTH), 0.01),
        'hy_w1': nrm((DEPTH, HYENA_EMB_DIM, HYENA_FILTER_DIM), HYENA_EMB_DIM ** -0.5),
        'hy_b1': nrm((DEPTH, HYENA_FILTER_DIM), 0.01),
        'hy_w2': nrm((DEPTH, HYENA_FILTER_DIM, HYENA_FILTER_DIM), HYENA_FILTER_DIM ** -0.5),
        'hy_b2': nrm((DEPTH, HYENA_FILTER_DIM), 0.01),
        'hy_freq': 1.0 + nrm((DEPTH, 2, HYENA_FILTER_DIM), 0.02),
        'hy_w3': nrm((DEPTH, HYENA_FILTER_DIM, HYENA_ORDER * HYENA_WIDTH), HYENA_FILTER_DIM ** -0.5),
        'hy_bias': nrm((DEPTH, HYENA_ORDER, HYENA_WIDTH), 0.5),
        'hy_norm_g': gain((DEPTH, HYENA_WIDTH)),
        'w_out': nrm((DEPTH, MIX_WIDTH, D), MIX_WIDTH ** -0.5),
        'ffn_w_up': nrm((DEPTH, D, 2 * D_FF), D ** -0.5),
        'ffn_conv_w': nrm((DEPTH, SHORT_CONV, 2 * D_FF), SHORT_CONV ** -0.5),
        'ffn_conv_b': nrm((DEPTH, 2 * D_FF), 0.01),
        'ffn_w_down': nrm((DEPTH, D_FF, D), D_FF ** -0.5),
        'final_norm_g': gain((D,)),
    }


def reference(x_prompt, x_sample, cache_k, cache_v, state_ssd_fwd, state_ssd_bwd, c, c_ctx,
              norm1_g, norm2_g, w_ada, b_ada, w_in, att_lambda, att_subln_g,
              ssd_conv_w, ssd_conv_b, ssd_dt_bias, ssd_a_log, ssd_d, ssd_norm_g,
              hy_conv_w, hy_conv_b, hy_w1, hy_b1, hy_w2, hy_b2, hy_freq, hy_w3, hy_bias, hy_norm_g,
              w_out, ffn_w_up, ffn_conv_w, ffn_conv_b, ffn_w_down, final_norm_g):
    rope = axial_rope(x_sample.shape[1])
    x_ctx = x_prompt
    x_lat = x_sample
    ks, vs, sfs, sbs = [], [], [], []
    for l in range(DEPTH):
        lw = {
            'norm1_g': norm1_g[l], 'norm2_g': norm2_g[l], 'w_ada': w_ada[l], 'b_ada': b_ada[l],
            'w_in': w_in[l], 'att_subln_g': att_subln_g[l],
            'ssd_conv_w': ssd_conv_w[l], 'ssd_conv_b': ssd_conv_b[l], 'ssd_dt_bias': ssd_dt_bias[l],
            'ssd_a_log': ssd_a_log[l], 'ssd_d': ssd_d[l], 'ssd_norm_g': ssd_norm_g[l],
            'hy_conv_w': hy_conv_w[l], 'hy_conv_b': hy_conv_b[l], 'hy_w1': hy_w1[l], 'hy_b1': hy_b1[l],
            'hy_w2': hy_w2[l], 'hy_b2': hy_b2[l], 'hy_freq': hy_freq[l], 'hy_w3': hy_w3[l],
            'hy_bias': hy_bias[l], 'hy_norm_g': hy_norm_g[l], 'w_out': w_out[l],
            'ffn_w_up': ffn_w_up[l], 'ffn_conv_w': ffn_conv_w[l], 'ffn_conv_b': ffn_conv_b[l],
            'ffn_w_down': ffn_w_down[l],
        }
        lam_init = 0.8 - 0.6 * math.exp(-0.3 * l)
        lam = diff_lambda(att_lambda[l], lam_init)
        x_ctx, (k_l, v_l, sf_l, sb_l) = trunk_layer(x_ctx, c_ctx, lw, lam, lam_init, None, None)
        ks.append(k_l)
        vs.append(v_l)
        sfs.append(sf_l)
        sbs.append(sb_l)
        x_lat, _ = trunk_layer(x_lat, c, lw, lam, lam_init, rope,
                               (cache_k[:, l], cache_v[:, l], state_ssd_fwd[:, l], state_ssd_bwd[:, l]))
    y_prompt = rmsnorm(x_ctx, final_norm_g)
    y_sample = rmsnorm(x_lat, final_norm_g)
    new_cache_k = jnp.stack(ks, axis=1)
    new_cache_v = jnp.stack(vs, axis=1)
    new_state_ssd_fwd = jnp.stack(sfs, axis=1)
    new_state_ssd_bwd = jnp.stack(sbs, axis=1)
    return (y_prompt, y_sample, new_cache_k, new_cache_v, new_state_ssd_fwd, new_state_ssd_bwd)
```

```python
import functools
import math

import numpy as np
import jax
import jax.numpy as jnp
from jax import lax
from jax.experimental import pallas as pl
from jax.experimental.pallas import tpu as pltpu

F32 = jnp.float32
BF16 = jnp.bfloat16

LANE = 128
V7X_VMEM_BYTES = 64 * 1024 * 1024
VMEM_CAP = V7X_VMEM_BYTES - 6 * 1024 * 1024

NORM_EPS = 1e-6
GRID_W = 64
ROPE_BASE = 10000.0
QK_DIM = 64
V_DIM = 128
SSD_P = 64
SSD_N = 128
SSD_CHUNK = 128
SSD_HPG = 6
SSD_GW = SSD_HPG * SSD_P
HY_BANDS = 16
HY_FREQ_BASE = 10000.0
HY_MIN_DECAY = -math.log(1e-2) / 1.5
HY_MAX_DECAY = -math.log(1e-2) / 0.3
HY_CB = 256
FFN_TN = 256


def _cparams(sem, vmem_bytes):
    limit = int(min(VMEM_CAP, max(vmem_bytes, 16 * 1024 * 1024)))
    return pltpu.CompilerParams(dimension_semantics=sem, vmem_limit_bytes=limit)


def _pick(n, candidates):
    for c in candidates:
        if n % c == 0:
            return c
    raise ValueError(f"no block size in {candidates} divides {n}")


def _dot(a, b):
    return jnp.dot(a, b, preferred_element_type=F32)


def _dot_nt(a, b):
    return lax.dot_general(a, b, (((1,), (1,)), ((), ())), preferred_element_type=F32)


def _split_bf16(x, n):
    parts = []
    r = x
    for _ in range(n):
        h = r.astype(BF16)
        parts.append(h)
        r = r - h.astype(F32)
    return parts


def _dot_f32_lhs(a, b_bf16):
    return sum(_dot(p, b_bf16) for p in _split_bf16(a, 3))


def _dot_f32_rhs(a_bf16, b):
    return sum(_dot(a_bf16, p) for p in _split_bf16(b, 3))


def _dot_hp(a, b):
    ah, al = _split_bf16(a, 2)
    bh, bl = _split_bf16(b, 2)
    return _dot(ah, bh) + _dot(ah, bl) + _dot(al, bh)


def _silu(x):
    return x * jax.nn.sigmoid(x)


def _conv3_rows(u, w_ref, b_ref, first, last):
    n = u.shape[0]
    prev = jnp.where(first, 0.0, pltpu.roll(u, 1, 0))
    nxt = jnp.where(last, 0.0, pltpu.roll(u, n - 1, 0))
    return prev * w_ref[0:1, :] + b_ref[...] + u * w_ref[1:2, :] + nxt * w_ref[2:3, :]


def _ada_kernel(c_ref, w_ref, b_ref, o_ref):
    s = _silu(c_ref[...]).astype(BF16)
    o_ref[0] = _dot(s, w_ref[0].astype(BF16)) + b_ref[0]


def _ada_call(cond, w_ada, b_ada):
    depth, d, n = w_ada.shape
    rows = cond.shape[0]
    tn = _pick(n, (512, 256, 128))
    vmem = 2 * (d * tn * 4) + d * tn * 2 + 4 * rows * (d + 2 * tn) * 4 + (4 << 20)
    return pl.pallas_call(
        _ada_kernel,
        grid=(depth, n // tn),
        in_specs=[
            pl.BlockSpec((rows, d), lambda l, j: (0, 0)),
            pl.BlockSpec((1, d, tn), lambda l, j: (l, 0, j)),
            pl.BlockSpec((1, 1, tn), lambda l, j: (l, 0, j)),
        ],
        out_specs=pl.BlockSpec((1, rows, tn), lambda l, j: (l, 0, j)),
        out_shape=jax.ShapeDtypeStruct((depth, rows, n), F32),
        compiler_params=_cparams(("arbitrary", "arbitrary"), vmem),
        name="ada_mod",
    )(cond, w_ada, b_ada.reshape(depth, 1, n))


def _norm_mod_kernel(x_ref, g_ref, sc_ref, sh_ref, o_ref):
    x = x_ref[...]
    r = lax.rsqrt(jnp.mean(x * x, axis=-1, keepdims=True) + NORM_EPS)
    y = x * r * g_ref[...]
    o_ref[...] = (y * (1.0 + sc_ref[0, 0]) + sh_ref[0, 0]).astype(o_ref.dtype)


def _norm_kernel(x_ref, g_ref, o_ref):
    x = x_ref[...]
    r = lax.rsqrt(jnp.mean(x * x, axis=-1, keepdims=True) + NORM_EPS)
    o_ref[...] = (x * r * g_ref[...]).astype(o_ref.dtype)


def _mod_row(i, tm, dims):
    start = i * tm
    lat = jnp.maximum(start - dims.n_ctx, 0) // dims.l_lat
    return jnp.where(start >= dims.n_ctx, lat + 1, 0)


def _norm_mod_call(x, g, mod, sh_idx, sc_idx, dims):
    m, d = x.shape
    tr = 256
    row = functools.partial(_mod_row, tm=tr, dims=dims)
    vmem = 2 * tr * d * (4 + 2) + 3 * tr * d * 4 + (4 << 20)
    return pl.pallas_call(
        _norm_mod_kernel,
        grid=(m // tr,),
        in_specs=[
            pl.BlockSpec((tr, d), lambda i: (i, 0)),
            pl.BlockSpec((1, d), lambda i: (0, 0)),
            pl.BlockSpec((1, 1, 1, d), lambda i: (row(i), sc_idx, 0, 0)),
            pl.BlockSpec((1, 1, 1, d), lambda i: (row(i), sh_idx, 0, 0)),
        ],
        out_specs=pl.BlockSpec((tr, d), lambda i: (i, 0)),
        out_shape=jax.ShapeDtypeStruct((m, d), BF16),
        compiler_params=_cparams(("arbitrary",), vmem),
        name="norm_mod",
    )(x, g.reshape(1, d), mod, mod)


def _norm_call(x, g, out_dtype, row_start=0, rows=None, name="norm"):
    m, d = x.shape
    rows = m if rows is None else rows
    tr = 256
    off = row_start // tr
    vmem = 2 * tr * d * 8 + 3 * tr * d * 4 + (4 << 20)
    return pl.pallas_call(
        _norm_kernel,
        grid=(rows // tr,),
        in_specs=[
            pl.BlockSpec((tr, d), lambda i: (i + off, 0)),
            pl.BlockSpec((1, d), lambda i: (0, 0)),
        ],
        out_specs=pl.BlockSpec((tr, d), lambda i: (i, 0)),
        out_shape=jax.ShapeDtypeStruct((rows, d), out_dtype),
        compiler_params=_cparams(("arbitrary",), vmem),
        name=name,
    )(x, g.reshape(1, d))


def _mm_kernel(a_ref, b_ref, o_ref):
    o_ref[...] = _dot(a_ref[...], b_ref[...]).astype(o_ref.dtype)


def _mm_res_kernel(a_ref, b_ref, x_ref, g_ref, o_ref):
    o_ref[...] = x_ref[...] + g_ref[0, 0] * _dot(a_ref[...], b_ref[...])


def _mm_call(a, b, dims, out_dtype=F32):
    m, k = a.shape
    n = b.shape[1]
    tm = dims.tm
    tn = _pick(n, (1024, 768, 512, 384, 256, 128))
    vmem = 2 * (tm * k * 2 + k * tn * 2 + tm * tn * 4) + tm * tn * 4 + (4 << 20)
    return pl.pallas_call(
        _mm_kernel,
        grid=(m // tm, n // tn),
        in_specs=[
            pl.BlockSpec((tm, k), lambda i, j: (i, 0)),
            pl.BlockSpec((k, tn), lambda i, j: (0, j)),
        ],
        out_specs=pl.BlockSpec((tm, tn), lambda i, j: (i, j)),
        out_shape=jax.ShapeDtypeStruct((m, n), out_dtype),
        compiler_params=_cparams(("arbitrary", "arbitrary"), vmem),
        name="proj",
    )(a, b)


def _mm_res_call(a, b, x, mod, g_idx, dims, tn_candidates, single_buffer_a=False):
    m, k = a.shape
    n = b.shape[1]
    tm = dims.tm
    tn = _pick(n, tn_candidates)
    row = functools.partial(_mod_row, tm=tm, dims=dims)
    a_bufs = 1 if single_buffer_a else 2
    vmem = a_bufs * tm * k * 2 + 2 * (k * tn * 2 + 2 * tm * tn * 4) + tm * tn * 4 + (4 << 20)
    if single_buffer_a:
        a_spec = pl.BlockSpec((tm, k), lambda i, j: (i, 0), pipeline_mode=pl.Buffered(1))
    else:
        a_spec = pl.BlockSpec((tm, k), lambda i, j: (i, 0))
    return pl.pallas_call(
        _mm_res_kernel,
        grid=(m // tm, n // tn),
        in_specs=[
            a_spec,
            pl.BlockSpec((k, tn), lambda i, j: (0, j)),
            pl.BlockSpec((tm, tn), lambda i, j: (i, j)),
            pl.BlockSpec((1, 1, 1, tn), lambda i, j: (row(i), g_idx, 0, j)),
        ],
        out_specs=pl.BlockSpec((tm, tn), lambda i, j: (i, j)),
        out_shape=jax.ShapeDtypeStruct((m, n), F32),
        compiler_params=_cparams(("arbitrary", "arbitrary"), vmem),
        name="proj_residual",
    )(a, b, x, mod)


def _ffn_up_kernel(a_ref, bg_ref, bv_ref, cwg_ref, cwv_ref, cbg_ref, cbv_ref, o_ref, *, dims):
    i = pl.program_id(0)
    a = a_ref[...]
    tm = a.shape[0]
    seq = jnp.where(i * tm < dims.n_ctx, dims.l_ctx, dims.l_lat)
    shape = (tm, bg_ref.shape[1])
    t = lax.broadcasted_iota(jnp.int32, shape, 0) & (seq - 1)
    first = t == 0
    last = t == seq - 1
    g = _conv3_rows(_dot(a, bg_ref[...]), cwg_ref, cbg_ref, first, last)
    v = _conv3_rows(_dot(a, bv_ref[...]), cwv_ref, cbv_ref, first, last)
    o_ref[...] = (_silu(g) * v).astype(o_ref.dtype)


def _ffn_up_call(h, w_up, conv_w, conv_b, dims):
    m, k = h.shape
    dff = w_up.shape[1] // 2
    tm = dims.tm
    tn = FFN_TN
    nb = dff // tn
    vmem = 2 * (tm * k * 2 + 2 * k * tn * 2 + tm * tn * 2) + 8 * tm * tn * 4 + (4 << 20)
    return pl.pallas_call(
        functools.partial(_ffn_up_kernel, dims=dims),
        grid=(m // tm, nb),
        in_specs=[
            pl.BlockSpec((tm, k), lambda i, j: (i, 0)),
            pl.BlockSpec((k, tn), lambda i, j: (0, j)),
            pl.BlockSpec((k, tn), lambda i, j: (0, j + nb)),
            pl.BlockSpec((3, tn), lambda i, j: (0, j)),
            pl.BlockSpec((3, tn), lambda i, j: (0, j + nb)),
            pl.BlockSpec((1, tn), lambda i, j: (0, j)),
            pl.BlockSpec((1, tn), lambda i, j: (0, j + nb)),
        ],
        out_specs=pl.BlockSpec((tm, tn), lambda i, j: (i, j)),
        out_shape=jax.ShapeDtypeStruct((m, dff), BF16),
        compiler_params=_cparams(("arbitrary", "arbitrary"), vmem),
        name="ffn_up_conv_gate",
    )(h, w_up, w_up, conv_w, conv_w, conv_b.reshape(1, -1), conv_b.reshape(1, -1))


def _rope(x, cos, sin_signed):
    lane = lax.broadcasted_iota(jnp.int32, x.shape, 1)
    n = x.shape[1]
    swapped = jnp.where((lane & 1) == 0, pltpu.roll(x, n - 1, 1), pltpu.roll(x, 1, 1))
    return x * cos + swapped * sin_signed


def _softmax_parts(s, sc):
    m = jnp.max(s, axis=-1, keepdims=True)
    if sc is not None:
        m = jnp.maximum(m, jnp.max(sc, axis=-1, keepdims=True))
    e = jnp.exp(s - m)
    tot = jnp.sum(e, axis=-1, keepdims=True)
    ec = None
    if sc is not None:
        ec = jnp.exp(sc - m)
        tot = tot + jnp.sum(ec, axis=-1, keepdims=True)
    return e, ec, 1.0 / tot


def _attn_kernel(*refs, latent, heads, tq, lq, lam_init):
    if latent:
        (q_ref, k_ref, v_ref, kc_ref, vc_ref, cos_ref, sin_ref, lv_ref, g_ref, o_ref,
         kb_s, vb_s, kcb_s, vcb_s) = refs
    else:
        q_ref, k_ref, v_ref, lv_ref, g_ref, o_ref, kb_s, vb_s = refs
    lv = lv_ref[...]
    lam = (jnp.exp(jnp.sum(lv[0:1] * lv[1:2], axis=-1, keepdims=True))
           - jnp.exp(jnp.sum(lv[2:3] * lv[3:4], axis=-1, keepdims=True)) + lam_init)
    gain = g_ref[...] * (1.0 - lam_init)
    scale = QK_DIM ** -0.5
    lo = lax.broadcasted_iota(jnp.int32, (tq, V_DIM), 1) < QK_DIM

    for h in range(heads):
        cols = slice(h * V_DIM, (h + 1) * V_DIM)
        k = k_ref[:, cols]
        if latent:
            k = _rope(k, cos_ref[...], sin_ref[...])
            kcb_s[...] = kc_ref[0, 0, :, cols].astype(BF16)
            vcb_s[...] = vc_ref[0, 0, :, cols].astype(BF16)
        kb_s[...] = k.astype(BF16)
        vb_s[...] = v_ref[:, cols].astype(BF16)

        def q_block(i, carry, cols=cols):
            rows = pl.ds(0, tq) if lq == tq else pl.ds(pl.multiple_of(i * tq, tq), tq)
            q = q_ref[rows, cols]
            if latent:
                q = _rope(q, cos_ref[rows, :], sin_ref[rows, :])
            q = q * scale
            q1 = jnp.where(lo, q, 0.0).astype(BF16)
            q2 = jnp.where(lo, 0.0, q).astype(BF16)
            kb = kb_s[...]
            s1 = _dot_nt(q1, kb)
            s2 = _dot_nt(q2, kb)
            s1c = s2c = None
            if latent:
                kcb = kcb_s[...]
                s1c = _dot_nt(q1, kcb)
                s2c = _dot_nt(q2, kcb)
            e1, e1c, r1 = _softmax_parts(s1, s1c)
            e2, e2c, r2 = _softmax_parts(s2, s2c)
            r2 = r2 * lam
            o = _dot((e1 * r1 - e2 * r2).astype(BF16), vb_s[...])
            if latent:
                o = o + _dot((e1c * r1 - e2c * r2).astype(BF16), vcb_s[...])
            rr = lax.rsqrt(jnp.mean(o * o, axis=-1, keepdims=True) + NORM_EPS)
            o_ref[rows, cols] = (o * rr * gain).astype(o_ref.dtype)
            return carry

        if lq == tq:
            q_block(0, 0)
        else:
            lax.fori_loop(0, lq // tq, q_block, 0)


def _attn_call(p, lv, g, dims, lam_init, latent, layer=None, cache_k=None, cache_v=None,
               cos=None, sin=None):
    n_heads = dims.att_heads
    if latent:
        nb, lq, heads, row_off = dims.b_lat, dims.l_lat, 1, dims.n_ctx // dims.l_lat
    else:
        nb, lq, heads, row_off = dims.b_ctx, dims.l_ctx, _pick(n_heads, (4, 2, 1)), 0
    tq = min(lq, 256)
    hb = n_heads // heads
    w = heads * V_DIM
    q_spec = pl.BlockSpec((lq, w), lambda b, h: (b + row_off, h))
    k_spec = pl.BlockSpec((lq, w), lambda b, h: (b + row_off, hb + h))
    v_spec = pl.BlockSpec((lq, w), lambda b, h: (b + row_off, 2 * hb + h))
    small = [pl.BlockSpec((4, QK_DIM), lambda b, h: (0, 0)),
             pl.BlockSpec((1, V_DIM), lambda b, h: (0, 0))]
    scratch = [pltpu.VMEM((lq, V_DIM), BF16), pltpu.VMEM((lq, V_DIM), BF16)]
    if latent:
        past = cache_k.shape[2]
        c_spec = pl.BlockSpec((1, 1, past, w), lambda b, h: (b, layer, 0, h))
        t_spec = pl.BlockSpec((lq, V_DIM), lambda b, h: (0, 0))
        in_specs = [q_spec, k_spec, v_spec, c_spec, c_spec, t_spec, t_spec] + small
        args = (p, p, p, cache_k, cache_v, cos, sin, lv, g.reshape(1, V_DIM))
        scratch += [pltpu.VMEM((past, V_DIM), BF16), pltpu.VMEM((past, V_DIM), BF16)]
        lk = lq + past
    else:
        in_specs = [q_spec, k_spec, v_spec] + small
        args = (p, p, p, lv, g.reshape(1, V_DIM))
        lk = lq
    vmem = 2 * 4 * lq * w * 4 + 12 * tq * lk * 4 + (8 << 20)
    return pl.pallas_call(
        functools.partial(_attn_kernel, latent=latent, heads=heads, tq=tq, lq=lq,
                          lam_init=lam_init),
        grid=(nb, hb),
        in_specs=in_specs,
        out_specs=pl.BlockSpec((lq, w), lambda b, h: (b, h)),
        out_shape=jax.ShapeDtypeStruct((nb * lq, n_heads * V_DIM), BF16),
        scratch_shapes=scratch,
        compiler_params=_cparams(("arbitrary", "arbitrary"), vmem),
        name="diff_attention_latent" if latent else "diff_attention_context",
    )(*args)


def _softplus(x):
    return jnp.maximum(x, 0.0) + jnp.log1p(jnp.exp(-jnp.abs(x)))


def _ssd_kernel(*refs, seq_len, latent):
    it = iter(refs)
    x_ref, b_ref, c_ref, z_ref, dt_ref = (next(it) for _ in range(5))
    cwx_ref, cbx_ref, cwb_ref, cbb_ref, cwc_ref, cbc_ref = (next(it) for _ in range(6))
    dtb_ref, alog_ref, d_ref = (next(it) for _ in range(3))
    if latent:
        sf0_ref, sb0_ref = next(it), next(it)
        y_ref = next(it)
    else:
        y_ref, sf_ref, sb_ref = next(it), next(it), next(it)
    (xs_s, b_s, c_s, bt_s, cs_s, rcs_s, dtxf_s, dtxb_s, csxf_s, rcsxb_s, y_s, st_s) = it

    ch = SSD_CHUNK
    nc = seq_len // ch

    def conv_silu(u_ref, w_ref, bias_ref):
        u = u_ref[...]
        t = lax.broadcasted_iota(jnp.int32, u.shape, 0)
        return _silu(_conv3_rows(u, w_ref, bias_ref, t == 0, t == seq_len - 1))

    xs_s[...] = conv_silu(x_ref, cwx_ref, cbx_ref)
    bm = conv_silu(b_ref, cwb_ref, cbb_ref)
    b_s[...] = bm.astype(BF16)
    for c in range(nc):
        bt_s[c] = bm[c * ch:(c + 1) * ch, :].T.astype(BF16)
    c_s[...] = conv_silu(c_ref, cwc_ref, cbc_ref).astype(BF16)

    dt = _softplus(dt_ref[...] + dtb_ref[...])
    dta = dt * (-jnp.exp(alog_ref[...]))
    ri = lax.broadcasted_iota(jnp.int32, (ch, ch), 0)
    ci = lax.broadcasted_iota(jnp.int32, (ch, ch), 1)
    lower = ri >= ci
    upper = ri <= ci
    t_low = jnp.where(lower, 1.0, 0.0).astype(BF16)
    t_up = jnp.where(upper, 1.0, 0.0).astype(BF16)
    for c in range(nc):
        blk = dta[c * ch:(c + 1) * ch, :]
        cs_s[c * ch:(c + 1) * ch, :] = _dot_f32_rhs(t_low, blk)
        rcs_s[c * ch:(c + 1) * ch, :] = _dot_f32_rhs(t_up, blk)
    er = lax.broadcasted_iota(jnp.int32, (LANE, SSD_GW), 0)
    eh = lax.broadcasted_iota(jnp.int32, (LANE, SSD_GW), 1) >> int(math.log2(SSD_P))
    exp_f = jnp.where(er == eh, 1.0, 0.0).astype(BF16)
    exp_b = jnp.where(er == eh + SSD_HPG, 1.0, 0.0).astype(BF16)
    dtxf_s[...] = _dot_f32_lhs(dt, exp_f)
    dtxb_s[...] = _dot_f32_lhs(dt, exp_b)
    csxf_s[...] = _dot_f32_lhs(cs_s[...], exp_f)
    rcsxb_s[...] = _dot_f32_lhs(rcs_s[...], exp_b)

    lo = lax.broadcasted_iota(jnp.int32, (ch, LANE), 1) < SSD_P

    def chunk_step(c, forward):
        rows = pl.ds(pl.multiple_of(c * ch, ch), ch)
        if forward:
            dtx, ex, cs, base, tri = dtxf_s[rows, :], csxf_s[rows, :], cs_s[rows, :], 0, lower
        else:
            dtx, ex, cs, base, tri = dtxb_s[rows, :], rcsxb_s[rows, :], rcs_s[rows, :], SSD_HPG, upper
        xdt = xs_s[rows, :] * dtx
        cc = c_s[rows, :]
        cb = _dot_nt(cc, b_s[rows, :])
        st = st_s[...]
        y_off = _dot(cc, st.astype(BF16)) * jnp.exp(ex)
        cs_t = cs.T
        for j in range(SSD_GW // LANE):
            lanes = slice(j * LANE, (j + 1) * LANE)
            xp = xdt[:, lanes]
            acc = y_off[:, lanes]
            for k in range(LANE // SSD_P):
                hh = base + j * (LANE // SSD_P) + k
                diff = cs[:, hh:hh + 1] - cs_t[hh:hh + 1, :]
                decay = jnp.exp(jnp.where(tri, diff, -jnp.inf))
                g = (cb * decay).astype(BF16)
                xm = jnp.where(lo if k == 0 else jnp.logical_not(lo), xp, 0.0).astype(BF16)
                acc = acc + _dot(g, xm)
            if forward:
                y_s[rows, lanes] = acc
            else:
                y_s[rows, lanes] = y_s[rows, lanes] + acc
        edge = ex[ch - 1:ch, :] if forward else ex[0:1, :]
        xd = (xdt * jnp.exp(edge - ex)).astype(BF16)
        st_s[...] = jnp.exp(edge) * st + _dot(bt_s[c], xd)

    def fwd_body(c, carry):
        chunk_step(c, True)
        return carry

    def bwd_body(i, carry):
        chunk_step(nc - 1 - i, False)
        return carry

    st_s[...] = sf0_ref[0, 0].T if latent else jnp.zeros(st_s.shape, F32)
    lax.fori_loop(0, nc, fwd_body, 0)
    if not latent:
        sf_ref[0] = st_s[...].T
    st_s[...] = sb0_ref[0, 0].T if latent else jnp.zeros(st_s.shape, F32)
    lax.fori_loop(0, nc, bwd_body, 0)
    if not latent:
        sb_ref[0] = st_s[...].T

    y = y_s[...] + d_ref[...] * xs_s[...]
    y_ref[...] = y * _silu(z_ref[...])


def _ssd_call(p, lw, dims, latent, layer=None, init_f=None, init_b=None):
    groups = dims.ssd_groups
    gw = SSD_GW
    if latent:
        nb, sl, row_off = dims.b_lat, dims.l_lat, dims.n_ctx // dims.l_lat
    else:
        nb, sl, row_off = dims.b_ctx, dims.l_ctx, 0
    nc = sl // SSD_CHUNK
    o = dims.offs
    zb, xb, bb, cb, db = o.z // gw, o.x // gw, o.b // LANE, o.c // LANE, o.dt // LANE
    in_specs = [
        pl.BlockSpec((sl, gw), lambda b, g: (b + row_off, xb + g)),
        pl.BlockSpec((sl, LANE), lambda b, g: (b + row_off, bb + g)),
        pl.BlockSpec((sl, LANE), lambda b, g: (b + row_off, cb + g)),
        pl.BlockSpec((sl, gw), lambda b, g: (b + row_off, zb + g)),
        pl.BlockSpec((sl, LANE), lambda b, g: (b + row_off, db + g)),
        pl.BlockSpec((3, gw), lambda b, g: (0, g)),
        pl.BlockSpec((1, gw), lambda b, g: (0, g)),
        pl.BlockSpec((3, LANE), lambda b, g: (0, dims.ssd_width // LANE + g)),
        pl.BlockSpec((1, LANE), lambda b, g: (0, dims.ssd_width // LANE + g)),
        pl.BlockSpec((3, LANE), lambda b, g: (0, dims.ssd_width // LANE + groups + g)),
        pl.BlockSpec((1, LANE), lambda b, g: (0, dims.ssd_width // LANE + groups + g)),
        pl.BlockSpec((1, LANE), lambda b, g: (0, g)),
        pl.BlockSpec((1, LANE), lambda b, g: (0, g)),
        pl.BlockSpec((1, gw), lambda b, g: (0, g)),
    ]
    args = [p, p, p, p, p, lw["ssd_conv_w"], lw["ssd_conv_b"], lw["ssd_conv_w"], lw["ssd_conv_b"],
            lw["ssd_conv_w"], lw["ssd_conv_b"], lw["ssd_dt_bias"], lw["ssd_a_log"], lw["ssd_d"]]
    y_shape = jax.ShapeDtypeStruct((nb * sl, dims.ssd_width), F32)
    y_spec = pl.BlockSpec((sl, gw), lambda b, g: (b, g))
    if latent:
        s_spec = pl.BlockSpec((1, 1, gw, SSD_N), lambda b, g: (b, layer, g, 0))
        in_specs += [s_spec, s_spec]
        args += [init_f, init_b]
        out_shape, out_specs = y_shape, y_spec
    else:
        s_shape = jax.ShapeDtypeStruct((nb, dims.ssd_width, SSD_N), F32)
        s_spec = pl.BlockSpec((1, gw, SSD_N), lambda b, g: (b, g, 0))
        out_shape, out_specs = (y_shape, s_shape, s_shape), (y_spec, s_spec, s_spec)
    scratch = [
        pltpu.VMEM((sl, gw), F32), pltpu.VMEM((sl, LANE), BF16), pltpu.VMEM((sl, LANE), BF16),
        pltpu.VMEM((nc, SSD_CHUNK, SSD_CHUNK), BF16),
        pltpu.VMEM((sl, LANE), F32), pltpu.VMEM((sl, LANE), F32),
        pltpu.VMEM((sl, gw), F32), pltpu.VMEM((sl, gw), F32),
        pltpu.VMEM((sl, gw), F32), pltpu.VMEM((sl, gw), F32),
        pltpu.VMEM((sl, gw), F32), pltpu.VMEM((SSD_N, gw), F32),
    ]
    vmem = 2 * sl * (3 * gw + 3 * LANE) * 4 + sl * (8 * gw + 4 * LANE) * 4 + 8 * sl * gw * 4 + (8 << 20)
    return pl.pallas_call(
        functools.partial(_ssd_kernel, seq_len=sl, latent=latent),
        grid=(nb, groups),
        in_specs=in_specs,
        out_specs=out_specs,
        out_shape=out_shape,
        scratch_shapes=scratch,
        compiler_params=_cparams(("arbitrary", "arbitrary"), vmem),
        name="ssd_latent" if latent else "ssd_context",
    )(*args)


def _dft_mats(l):
    n = 2 * l
    k = np.arange(l, dtype=np.int64)[:, None]
    t = np.arange(l, dtype=np.int64)[None, :]
    ang = 2.0 * np.pi * ((k * t) % n).astype(np.float64) / n
    fwd = np.concatenate([np.cos(ang), -np.sin(ang)], axis=0)
    fwd[l, :] = np.where(np.arange(l) % 2 == 0, 1.0, -1.0)
    pos = (np.arange(l, dtype=np.int64) + l // 2)[:, None]
    kk = np.arange(l, dtype=np.int64)[None, :]
    ang2 = 2.0 * np.pi * ((pos * kk) % n).astype(np.float64) / n
    inv = np.concatenate([2.0 * np.cos(ang2), -2.0 * np.sin(ang2)], axis=1) / n
    inv[:, 0] = 1.0 / n
    inv[:, l] = np.where(pos[:, 0] % 2 == 0, 1.0, -1.0) / n

    def split(m):
        hi = m.astype(np.float32).astype(BF16)
        lo = (m - hi.astype(np.float64)).astype(np.float32).astype(BF16)
        return hi, lo

    return split(fwd) + split(inv)


def _hy_feats(l):
    off = jnp.arange(l, dtype=F32) - (l // 2)
    band = HY_FREQ_BASE ** (-jnp.arange(HY_BANDS, dtype=F32) / HY_BANDS)
    ang = off[:, None] * band
    feats = jnp.concatenate([off[:, None] / l, jnp.sin(ang), jnp.cos(ang)], axis=-1)
    return jnp.pad(feats, ((0, 0), (0, LANE - feats.shape[1])))


def _hy_filter_kernel(feats_ref, w1_ref, b1_ref, w2_ref, b2_ref, fr_ref, w3_ref, dl_ref,
                      fh_ref, fl_ref, o_ref, *, seq_len):
    fr = fr_ref[0]
    h1 = jnp.sin(fr[0:1] * (_dot_hp(feats_ref[...], w1_ref[0]) + b1_ref[0]))
    h2 = jnp.sin(fr[1:2] * (_dot_hp(h1, w2_ref[0]) + b2_ref[0]))
    filt = _dot_hp(h2, w3_ref[0])
    off = lax.broadcasted_iota(jnp.int32, filt.shape, 0).astype(F32) - (seq_len // 2)
    dist = jnp.abs(off) * (2.0 / seq_len)
    h = filt * jnp.exp(-dist * dl_ref[...])
    hh, hl = _split_bf16(h, 2)
    fh = fh_ref[...]
    o_ref[0, 0] = _dot(fh, hh) + _dot(fh, hl) + _dot(fl_ref[...], hh)


def _hy_filter_call(seq_len, hw, mats, dims):
    depth, hy = dims.depth, dims.hy_width
    cb = HY_CB
    ncb = hy // cb
    fh, fl = mats[0], mats[1]
    feats = _hy_feats(seq_len)
    deltas = jnp.linspace(HY_MIN_DECAY, HY_MAX_DECAY, hy, dtype=F32).reshape(1, hy)
    vmem = 2 * 2 * (2 * seq_len * seq_len * 2) + 6 * 2 * seq_len * cb * 4 + (8 << 20)
    return pl.pallas_call(
        functools.partial(_hy_filter_kernel, seq_len=seq_len),
        grid=(depth, 2, ncb),
        in_specs=[
            pl.BlockSpec((seq_len, LANE), lambda l, n, c: (0, 0)),
            pl.BlockSpec((1, LANE, LANE), lambda l, n, c: (l, 0, 0)),
            pl.BlockSpec((1, 1, LANE), lambda l, n, c: (l, 0, 0)),
            pl.BlockSpec((1, LANE, LANE), lambda l, n, c: (l, 0, 0)),
            pl.BlockSpec((1, 1, LANE), lambda l, n, c: (l, 0, 0)),
            pl.BlockSpec((1, 2, LANE), lambda l, n, c: (l, 0, 0)),
            pl.BlockSpec((1, LANE, cb), lambda l, n, c: (l, 0, n * ncb + c)),
            pl.BlockSpec((1, cb), lambda l, n, c: (0, c)),
            pl.BlockSpec((2 * seq_len, seq_len), lambda l, n, c: (0, 0)),
            pl.BlockSpec((2 * seq_len, seq_len), lambda l, n, c: (0, 0)),
        ],
        out_specs=pl.BlockSpec((1, 1, 2 * seq_len, cb), lambda l, n, c: (l, n, 0, c)),
        out_shape=jax.ShapeDtypeStruct((depth, 2, 2 * seq_len, hy), F32),
        compiler_params=_cparams(("arbitrary",) * 3, vmem),
        name="hyena_filter_spectrum",
    )(feats, hw["w1"], hw["b1"], hw["w2"], hw["b2"], hw["freq"], hw["w3"], deltas, fh, fl)


def _hy_conv_kernel(x1_ref, x2_ref, v_ref, cw1_ref, cb1_ref, cw2_ref, cb2_ref, cwv_ref, cbv_ref,
                    hf_ref, bias_ref, fh_ref, fl_ref, gh_ref, gl_ref, o_ref, *, seq_len):
    t = lax.broadcasted_iota(jnp.int32, x1_ref.shape, 0)
    first = t == 0
    last = t == seq_len - 1
    gates = (_conv3_rows(x1_ref[...], cw1_ref, cb1_ref, first, last),
             _conv3_rows(x2_ref[...], cw2_ref, cb2_ref, first, last))
    zz = _conv3_rows(v_ref[...], cwv_ref, cbv_ref, first, last)
    fh = fh_ref[...]
    gh = gh_ref[...]
    for n in range(2):
        zh, zl = _split_bf16(zz, 2)
        uf = _dot(fh, zh) + _dot(fh, zl) + _dot(fl_ref[...], zh)
        ur, ui = uf[:seq_len], uf[seq_len:]
        hr, hi = hf_ref[0, n, :seq_len, :], hf_ref[0, n, seq_len:, :]
        ii = ui * hi
        yr = ur * hr - jnp.where(first, 0.0, ii)
        yi = jnp.where(first, ii, ur * hi + ui * hr)
        yh, yl = _split_bf16(jnp.concatenate([yr, yi], axis=0), 2)
        y = _dot(gh, yh) + _dot(gh, yl) + _dot(gl_ref[...], yh)
        zz = gates[n] * (y + zz * bias_ref[n:n + 1, :])
    o_ref[...] = zz


def _hy_conv_call(p, hf, lw, mats, dims, layer, latent):
    hy = dims.hy_width
    cb = HY_CB
    ncb = hy // cb
    if latent:
        nb, sl, row_off = dims.b_lat, dims.l_lat, dims.n_ctx // dims.l_lat
    else:
        nb, sl, row_off = dims.b_ctx, dims.l_ctx, 0
    hb = dims.offs.hy // cb
    const = lambda c, b: (0, 0)
    in_specs = [pl.BlockSpec((sl, cb), lambda c, b, k=k: (b + row_off, hb + k * ncb + c)) for k in range(3)]
    for k in range(3):
        in_specs += [pl.BlockSpec((3, cb), lambda c, b, k=k: (0, k * ncb + c)),
                     pl.BlockSpec((1, cb), lambda c, b, k=k: (0, k * ncb + c))]
    in_specs += [
        pl.BlockSpec((1, 2, 2 * sl, cb), lambda c, b: (layer, 0, 0, c)),
        pl.BlockSpec((2, cb), lambda c, b: (0, c)),
        pl.BlockSpec((2 * sl, sl), const), pl.BlockSpec((2 * sl, sl), const),
        pl.BlockSpec((sl, 2 * sl), const), pl.BlockSpec((sl, 2 * sl), const),
    ]
    cw, cbias = lw["hy_conv_w"], lw["hy_conv_b"]
    vmem = 2 * 4 * (2 * sl * sl * 2) + 2 * (2 * 2 * sl * cb * 4 + 4 * sl * cb * 4) + 14 * sl * cb * 4 + (8 << 20)
    return pl.pallas_call(
        functools.partial(_hy_conv_kernel, seq_len=sl),
        grid=(ncb, nb),
        in_specs=in_specs,
        out_specs=pl.BlockSpec((sl, cb), lambda c, b: (b, c)),
        out_shape=jax.ShapeDtypeStruct((nb * sl, hy), F32),
        compiler_params=_cparams(("arbitrary", "arbitrary"), vmem),
        name="hyena_conv_latent" if latent else "hyena_conv_context",
    )(p, p, p, cw, cbias, cw, cbias, cw, cbias, hf, lw["hy_bias"], *mats)


class _Offsets:
    def __init__(self, att_w, ssd_w, groups, hy_w):
        self.q, self.k, self.v = 0, att_w, 2 * att_w
        self.z = 3 * att_w
        self.x = self.z + ssd_w
        self.b = self.x + ssd_w
        self.c = self.b + groups * SSD_N
        self.hy = self.c + groups * SSD_N
        self.dt = self.hy + 3 * hy_w
        self.total = self.dt + groups * LANE


class _Dims:
    pass


def _make_dims(x_prompt, x_sample, cache_k, state_ssd_fwd, w_in, hy_norm_g, ffn_w_down):
    d = _Dims()
    d.b_ctx, d.l_ctx, d.d_model = x_prompt.shape
    d.b_lat, d.l_lat, _ = x_sample.shape
    d.n_ctx = d.b_ctx * d.l_ctx
    d.n_lat = d.b_lat * d.l_lat
    d.depth = w_in.shape[0]
    d.att_heads = cache_k.shape[3]
    d.att_width = d.att_heads * V_DIM
    d.ssd_heads = state_ssd_fwd.shape[2]
    d.ssd_width = d.ssd_heads * SSD_P
    d.ssd_groups = d.ssd_heads // SSD_HPG
    d.hy_width = hy_norm_g.shape[-1]
    d.d_ff = ffn_w_down.shape[1]
    d.offs = _Offsets(d.att_width, d.ssd_width, d.ssd_groups, d.hy_width)
    d.tm = d.l_lat
    assert d.l_lat % d.l_ctx == 0 and d.n_ctx % d.tm == 0 and d.l_ctx % 256 == 0
    assert d.l_lat & (d.l_lat - 1) == 0 and d.l_ctx & (d.l_ctx - 1) == 0
    assert d.offs.hy % HY_CB == 0 and d.hy_width % HY_CB == 0 and d.d_ff % FFN_TN == 0
    assert d.ssd_heads % SSD_HPG == 0
    return d


def _dt_gather(dims):
    src = np.zeros((dims.ssd_groups * LANE,), np.int32)
    valid = np.zeros((dims.ssd_groups * LANE,), np.float32)
    for g in range(dims.ssd_groups):
        for direction in range(2):
            for hh in range(SSD_HPG):
                lane = g * LANE + direction * SSD_HPG + hh
                src[lane] = direction * dims.ssd_heads + g * SSD_HPG + hh
                valid[lane] = 1.0
    return src, valid


def _rope_tables(seq_len):
    rows = seq_len // GRID_W
    r, col = jnp.meshgrid(jnp.arange(rows), jnp.arange(GRID_W), indexing="ij")
    npair = QK_DIM // 4
    inv = ROPE_BASE ** (-jnp.arange(npair, dtype=F32) / npair)
    ang = jnp.concatenate([r.reshape(-1, 1).astype(F32) * inv, col.reshape(-1, 1).astype(F32) * inv], axis=-1)
    cos, sin = jnp.cos(ang), jnp.sin(ang)
    cos2 = jnp.repeat(cos, 2, axis=-1)
    sin2 = jnp.stack([-sin, sin], axis=-1).reshape(seq_len, QK_DIM)
    return jnp.tile(cos2, (1, 2)), jnp.tile(sin2, (1, 2))


def kernel(x_prompt, x_sample, cache_k, cache_v, state_ssd_fwd, state_ssd_bwd, c, c_ctx, norm1_g, norm2_g, w_ada, b_ada, w_in, att_lambda, att_subln_g, ssd_conv_w, ssd_conv_b, ssd_dt_bias, ssd_a_log, ssd_d, ssd_norm_g, hy_conv_w, hy_conv_b, hy_w1, hy_b1, hy_w2, hy_b2, hy_freq, hy_w3, hy_bias, hy_norm_g, w_out, ffn_w_up, ffn_conv_w, ffn_conv_b, ffn_w_down, final_norm_g):
    dims = _make_dims(x_prompt, x_sample, cache_k, state_ssd_fwd, w_in, hy_norm_g, ffn_w_down)
    d, depth, offs = dims.d_model, dims.depth, dims.offs
    n_ctx = dims.n_ctx

    mod_rows = 16 * ((1 + dims.b_lat + 15) // 16)
    cond = jnp.zeros((mod_rows, d), F32).at[0].set(c_ctx).at[1:1 + dims.b_lat].set(c)
    mod_all = _ada_call(cond, w_ada, b_ada).reshape(depth, mod_rows, 6, 1, d)

    pad_h = LANE - hy_w1.shape[-1]
    hw = {
        "w1": jnp.pad(hy_w1, ((0, 0), (0, LANE - hy_w1.shape[1]), (0, pad_h))),
        "b1": jnp.pad(hy_b1, ((0, 0), (0, pad_h)))[:, None, :],
        "w2": jnp.pad(hy_w2, ((0, 0), (0, pad_h), (0, pad_h))),
        "b2": jnp.pad(hy_b2, ((0, 0), (0, pad_h)))[:, None, :],
        "freq": jnp.pad(hy_freq, ((0, 0), (0, 0), (0, pad_h))),
        "w3": jnp.pad(hy_w3, ((0, 0), (0, pad_h), (0, 0))),
    }
    mats_ctx = _dft_mats(dims.l_ctx)
    mats_lat = _dft_mats(dims.l_lat)
    hf_ctx = _hy_filter_call(dims.l_ctx, hw, mats_ctx, dims)
    hf_lat = _hy_filter_call(dims.l_lat, hw, mats_lat, dims)

    cos_t, sin_t = _rope_tables(dims.l_lat)
    dt_src, dt_valid = _dt_gather(dims)
    ck = cache_k.reshape(dims.b_lat, depth, cache_k.shape[2], dims.att_width)
    cv = cache_v.reshape(dims.b_lat, depth, cache_v.shape[2], dims.att_width)
    sf0 = state_ssd_fwd.reshape(dims.b_lat, depth, dims.ssd_width, SSD_N)
    sb0 = state_ssd_bwd.reshape(dims.b_lat, depth, dims.ssd_width, SSD_N)

    x = jnp.concatenate([x_prompt.reshape(n_ctx, d), x_sample.reshape(dims.n_lat, d)], axis=0)
    ks, vs, sfs, sbs = [], [], [], []
    dt_cols = 2 * dims.ssd_heads
    for l in range(depth):
        lam_init = 0.8 - 0.6 * math.exp(-0.3 * l)
        mod = mod_all[l]
        w_l = w_in[l]
        w_main = w_l[:, :offs.hy]
        w_dt = jnp.take(w_l[:, offs.hy:offs.hy + dt_cols], dt_src, axis=1) * dt_valid
        w_hy = w_l[:, offs.hy + dt_cols:]
        w_proj = jnp.concatenate([w_main, w_hy, w_dt], axis=1).astype(BF16)
        dtb = jnp.take(ssd_dt_bias[l].reshape(-1), dt_src) * dt_valid
        alog = jnp.take(ssd_a_log[l].reshape(-1), dt_src) * dt_valid
        lw = {
            "ssd_conv_w": ssd_conv_w[l], "ssd_conv_b": ssd_conv_b[l].reshape(1, -1),
            "ssd_dt_bias": dtb.reshape(1, -1), "ssd_a_log": alog.reshape(1, -1),
            "ssd_d": jnp.repeat(ssd_d[l], SSD_P).reshape(1, -1),
            "hy_conv_w": hy_conv_w[l], "hy_conv_b": hy_conv_b[l].reshape(1, -1), "hy_bias": hy_bias[l],
        }

        h = _norm_mod_call(x, norm1_g[l], mod, 0, 1, dims)
        p = _mm_call(h, w_proj, dims)

        att_c = _attn_call(p, att_lambda[l], att_subln_g[l], dims, lam_init, latent=False)
        att_l = _attn_call(p, att_lambda[l], att_subln_g[l], dims, lam_init, latent=True, layer=l,
                           cache_k=ck, cache_v=cv, cos=cos_t, sin=sin_t)
        ys_c, sf_l, sb_l = _ssd_call(p, lw, dims, latent=False)
        ys_l = _ssd_call(p, lw, dims, latent=True, layer=l, init_f=sf0, init_b=sb0)
        zz_c = _hy_conv_call(p, hf_ctx, lw, mats_ctx, dims, l, latent=False)
        zz_l = _hy_conv_call(p, hf_lat, lw, mats_lat, dims, l, latent=True)

        y_ssd = _norm_call(jnp.concatenate([ys_c, ys_l], axis=0), ssd_norm_g[l], BF16, name="ssd_norm")
        y_hy = _norm_call(jnp.concatenate([zz_c, zz_l], axis=0), hy_norm_g[l], BF16, name="hyena_norm")
        att = jnp.concatenate([att_c, att_l], axis=0)
        mix = jnp.concatenate([att, y_ssd, y_hy], axis=-1)
        x = _mm_res_call(mix, w_out[l].astype(BF16), x, mod, 2, dims, (1024, 768, 512, 384, 256, 128))

        h2 = _norm_mod_call(x, norm2_g[l], mod, 3, 4, dims)
        act = _ffn_up_call(h2, ffn_w_up[l].astype(BF16), ffn_conv_w[l], ffn_conv_b[l], dims)
        x = _mm_res_call(act, ffn_w_down[l].astype(BF16), x, mod, 5, dims, (256, 128),
                         single_buffer_a=True)

        ks.append(p[:n_ctx, offs.k:offs.k + dims.att_width].reshape(dims.b_ctx, dims.l_ctx, -1))
        vs.append(p[:n_ctx, offs.v:offs.v + dims.att_width].reshape(dims.b_ctx, dims.l_ctx, -1))
        sfs.append(sf_l)
        sbs.append(sb_l)

    y_prompt = _norm_call(x, final_norm_g, F32, 0, n_ctx, name="final_norm").reshape(x_prompt.shape)
    y_sample = _norm_call(x, final_norm_g, F32, n_ctx, dims.n_lat, name="final_norm").reshape(x_sample.shape)
    kv_shape = (dims.b_ctx, depth, dims.l_ctx, dims.att_heads, V_DIM)
    st_shape = (dims.b_ctx, depth, dims.ssd_heads, SSD_P, SSD_N)
    new_k = jnp.stack(ks, axis=1).reshape(kv_shape)
    new_v = jnp.stack(vs, axis=1).reshape(kv_shape)
    new_sf = jnp.stack(sfs, axis=1).reshape(st_shape)
    new_sb = jnp.stack(sbs, axis=1).reshape(st_shape)
    return (y_prompt, y_sample, new_k, new_v, new_sf, new_sb)
```

```python
import functools
import math

import numpy as np
import jax
import jax.numpy as jnp
from jax import lax
from jax.experimental import pallas as pl
from jax.experimental.pallas import tpu as pltpu

F32 = jnp.float32
BF16 = jnp.bfloat16

LANE = 128
V7X_VMEM_BYTES = 64 * 1024 * 1024
VMEM_CAP = V7X_VMEM_BYTES - 6 * 1024 * 1024

NORM_EPS = 1e-6
GRID_W = 64
ROPE_BASE = 10000.0
QK_DIM = 64
V_DIM = 128
SSD_P = 64
SSD_N = 128
SSD_CHUNK = 128
SSD_HPG = 6
SSD_GW = SSD_HPG * SSD_P
HY_BANDS = 16
HY_FREQ_BASE = 10000.0
HY_MIN_DECAY = -math.log(1e-2) / 1.5
HY_MAX_DECAY = -math.log(1e-2) / 0.3
HY_CB = 256
FFN_TN = 256
FFN_BLOCKS_PER_STEP = 3
HY_ROWS_PER_STEP = 2048

_ANY_SPEC = pl.BlockSpec(memory_space=pl.ANY)


def _cparams(sem, vmem_bytes):
    limit = int(min(VMEM_CAP, max(vmem_bytes, 16 * 1024 * 1024)))
    return pltpu.CompilerParams(dimension_semantics=sem, vmem_limit_bytes=limit)


def _pick(n, candidates):
    for c in candidates:
        if n % c == 0:
            return c
    raise ValueError(f"no block size in {candidates} divides {n}")


def _dot(a, b):
    return jnp.dot(a, b, preferred_element_type=F32)


def _dot_nt(a, b):
    return lax.dot_general(a, b, (((1,), (1,)), ((), ())), preferred_element_type=F32)


def _split_bf16(x, n):
    parts = []
    r = x
    for _ in range(n):
        h = r.astype(BF16)
        parts.append(h)
        r = r - h.astype(F32)
    return parts


def _dot_f32_lhs(a, b_bf16):
    return sum(_dot(p, b_bf16) for p in _split_bf16(a, 3))


def _dot_f32_rhs(a_bf16, b):
    return sum(_dot(a_bf16, p) for p in _split_bf16(b, 3))


def _dot_hp(a, b):
    ah, al = _split_bf16(a, 2)
    bh, bl = _split_bf16(b, 2)
    return _dot(ah, bh) + _dot(ah, bl) + _dot(al, bh)


def _silu(x):
    return x * jax.nn.sigmoid(x)


def _conv3_rows(u, w_ref, b_ref, first, last):
    n = u.shape[0]
    prev = jnp.where(first, 0.0, pltpu.roll(u, 1, 0))
    nxt = jnp.where(last, 0.0, pltpu.roll(u, n - 1, 0))
    return prev * w_ref[0:1, :] + b_ref[...] + u * w_ref[1:2, :] + nxt * w_ref[2:3, :]


def _ada_kernel(c_ref, w_ref, b_ref, o_ref):
    s = _silu(c_ref[...]).astype(BF16)
    o_ref[0] = _dot(s, w_ref[0].astype(BF16)) + b_ref[0]


def _ada_call(cond, w_ada, b_ada):
    depth, d, n = w_ada.shape
    rows = cond.shape[0]
    tn = _pick(n, (512, 256, 128))
    vmem = 2 * (d * tn * 4) + d * tn * 2 + 4 * rows * (d + 2 * tn) * 4 + (4 << 20)
    return pl.pallas_call(
        _ada_kernel,
        grid=(depth, n // tn),
        in_specs=[
            pl.BlockSpec((rows, d), lambda l, j: (0, 0)),
            pl.BlockSpec((1, d, tn), lambda l, j: (l, 0, j)),
            pl.BlockSpec((1, 1, tn), lambda l, j: (l, 0, j)),
        ],
        out_specs=pl.BlockSpec((1, rows, tn), lambda l, j: (l, 0, j)),
        out_shape=jax.ShapeDtypeStruct((depth, rows, n), F32),
        compiler_params=_cparams(("arbitrary", "arbitrary"), vmem),
        name="ada_mod",
    )(cond, w_ada, b_ada.reshape(depth, 1, n))


def _norm_mod_kernel(x_ref, g_ref, sc_ref, sh_ref, o_ref):
    x = x_ref[...]
    r = lax.rsqrt(jnp.mean(x * x, axis=-1, keepdims=True) + NORM_EPS)
    y = x * r * g_ref[...]
    o_ref[...] = (y * (1.0 + sc_ref[0, 0]) + sh_ref[0, 0]).astype(o_ref.dtype)


def _norm_kernel(x_ref, g_ref, o_ref):
    x = x_ref[...]
    r = lax.rsqrt(jnp.mean(x * x, axis=-1, keepdims=True) + NORM_EPS)
    o_ref[...] = (x * r * g_ref[...]).astype(o_ref.dtype)


def _mod_row(i, tm, dims):
    start = i * tm
    lat = jnp.maximum(start - dims.n_ctx, 0) // dims.l_lat
    return jnp.where(start >= dims.n_ctx, lat + 1, 0)


def _norm_mod_call(x, g, mod, sh_idx, sc_idx, dims):
    m, d = x.shape
    tr = 256
    row = functools.partial(_mod_row, tm=tr, dims=dims)
    vmem = 2 * tr * d * (4 + 2) + 3 * tr * d * 4 + (4 << 20)
    return pl.pallas_call(
        _norm_mod_kernel,
        grid=(m // tr,),
        in_specs=[
            pl.BlockSpec((tr, d), lambda i: (i, 0)),
            pl.BlockSpec((1, d), lambda i: (0, 0)),
            pl.BlockSpec((1, 1, 1, d), lambda i: (row(i), sc_idx, 0, 0)),
            pl.BlockSpec((1, 1, 1, d), lambda i: (row(i), sh_idx, 0, 0)),
        ],
        out_specs=pl.BlockSpec((tr, d), lambda i: (i, 0)),
        out_shape=jax.ShapeDtypeStruct((m, d), BF16),
        compiler_params=_cparams(("arbitrary",), vmem),
        name="norm_mod",
    )(x, g.reshape(1, d), mod, mod)


def _norm_call(x, g, out_dtype, row_start=0, rows=None, name="norm"):
    m, d = x.shape
    rows = m if rows is None else rows
    tr = 256
    off = row_start // tr
    vmem = 2 * tr * d * 8 + 3 * tr * d * 4 + (4 << 20)
    return pl.pallas_call(
        _norm_kernel,
        grid=(rows // tr,),
        in_specs=[
            pl.BlockSpec((tr, d), lambda i: (i + off, 0)),
            pl.BlockSpec((1, d), lambda i: (0, 0)),
        ],
        out_specs=pl.BlockSpec((tr, d), lambda i: (i, 0)),
        out_shape=jax.ShapeDtypeStruct((rows, d), out_dtype),
        compiler_params=_cparams(("arbitrary",), vmem),
        name=name,
    )(x, g.reshape(1, d))


def _mm_kernel(a_ref, b_ref, o_ref):
    o_ref[...] = _dot(a_ref[...], b_ref[...]).astype(o_ref.dtype)


def _mm_res_kernel(*refs, n_terms):
    x_ref, g_ref, o_ref = refs[2 * n_terms:]
    acc = _dot(refs[0][...], refs[1][...])
    for i in range(1, n_terms):
        acc = acc + _dot(refs[2 * i][...], refs[2 * i + 1][...])
    o_ref[...] = x_ref[...] + g_ref[0, 0] * acc


def _mm_call(a, b, dims, out_dtype=F32):
    m, k = a.shape
    n = b.shape[1]
    tm = dims.tm
    tn = _pick(n, (1024, 768, 512, 384, 256, 128))
    vmem = 2 * (tm * k * 2 + k * tn * 2 + tm * tn * 4) + tm * tn * 4 + (4 << 20)
    return pl.pallas_call(
        _mm_kernel,
        grid=(m // tm, n // tn),
        in_specs=[
            pl.BlockSpec((tm, k), lambda i, j: (i, 0)),
            pl.BlockSpec((k, tn), lambda i, j: (0, j)),
        ],
        out_specs=pl.BlockSpec((tm, tn), lambda i, j: (i, j)),
        out_shape=jax.ShapeDtypeStruct((m, n), out_dtype),
        compiler_params=_cparams(("arbitrary", "arbitrary"), vmem),
        name="proj",
    )(a, b)


def _mm_res_call(terms, x, mod, g_idx, dims, tn_candidates, single_buffer_a=False):
    m, n = x.shape
    k = sum(a.shape[1] for a, _ in terms)
    tm = dims.tm
    tn = _pick(n, tn_candidates)
    row = functools.partial(_mod_row, tm=tm, dims=dims)
    a_bufs = 1 if single_buffer_a else 2
    vmem = a_bufs * tm * k * 2 + 2 * (k * tn * 2 + 2 * tm * tn * 4) + tm * tn * 4 + (4 << 20)
    in_specs, args = [], []
    for a, b in terms:
        ka = a.shape[1]
        if single_buffer_a:
            in_specs.append(pl.BlockSpec((tm, ka), lambda i, j: (i, 0), pipeline_mode=pl.Buffered(1)))
        else:
            in_specs.append(pl.BlockSpec((tm, ka), lambda i, j: (i, 0)))
        in_specs.append(pl.BlockSpec((ka, tn), lambda i, j: (0, j)))
        args += [a, b]
    in_specs += [pl.BlockSpec((tm, tn), lambda i, j: (i, j)),
                 pl.BlockSpec((1, 1, 1, tn), lambda i, j: (row(i), g_idx, 0, j))]
    return pl.pallas_call(
        functools.partial(_mm_res_kernel, n_terms=len(terms)),
        grid=(m // tm, n // tn),
        in_specs=in_specs,
        out_specs=pl.BlockSpec((tm, tn), lambda i, j: (i, j)),
        out_shape=jax.ShapeDtypeStruct((m, n), F32),
        compiler_params=_cparams(("arbitrary", "arbitrary"), vmem),
        name="proj_residual",
    )(*args, x, mod)


def _ffn_up_kernel(*refs, dims, nblk, n_alias):
    a_ref = refs[0]
    b_refs = refs[1:1 + 2 * nblk]
    cw_refs = refs[1 + 2 * nblk:1 + 4 * nblk]
    cb_refs = refs[1 + 4 * nblk:1 + 6 * nblk]
    o_ref = refs[1 + 6 * nblk + n_alias]
    a = a_ref[...]
    tm = a.shape[0]
    tn = b_refs[0].shape[1]
    seq = jnp.where(pl.program_id(0) * tm < dims.n_ctx, dims.l_ctx, dims.l_lat)
    t = lax.broadcasted_iota(jnp.int32, (tm, tn), 0) & (seq - 1)
    first = t == 0
    last = t == seq - 1
    for q in range(nblk):
        g = _conv3_rows(_dot(a, b_refs[2 * q][...]), cw_refs[2 * q], cb_refs[2 * q], first, last)
        v = _conv3_rows(_dot(a, b_refs[2 * q + 1][...]), cw_refs[2 * q + 1], cb_refs[2 * q + 1],
                        first, last)
        o_ref[:, q * tn:(q + 1) * tn] = (_silu(g) * v).astype(o_ref.dtype)


def _ffn_up_call(h, w_up, conv_w, conv_b, dims):
    m, k = h.shape
    dff = w_up.shape[1] // 2
    tm = dims.tm
    tn = FFN_TN
    nb = dff // tn
    conv_b = conv_b.reshape(1, -1)

    def run(first_block, steps, nblk, carried):
        cols = []
        for q in range(nblk):
            cols += [lambda i, j, q=q: (0, first_block + nblk * j + q),
                     lambda i, j, q=q: (0, nb + first_block + nblk * j + q)]
        in_specs = [pl.BlockSpec((tm, k), lambda i, j: (i, 0), pipeline_mode=pl.Buffered(1))]
        in_specs += [pl.BlockSpec((k, tn), c) for c in cols]
        in_specs += [pl.BlockSpec((3, tn), c) for c in cols]
        in_specs += [pl.BlockSpec((1, tn), c) for c in cols]
        n_in = len(in_specs)
        vmem = (tm * k * 2 + 2 * nblk * (2 * k * tn * 2 + tm * tn * 2) + 16 * tm * tn * 4 + (4 << 20))
        return pl.pallas_call(
            functools.partial(_ffn_up_kernel, dims=dims, nblk=nblk, n_alias=len(carried)),
            grid=(m // tm, steps),
            in_specs=in_specs + [_ANY_SPEC] * len(carried),
            out_specs=pl.BlockSpec((tm, nblk * tn), lambda i, j: (i, first_block // nblk + j)),
            out_shape=jax.ShapeDtypeStruct((m, dff), BF16),
            input_output_aliases={n_in + i: i for i in range(len(carried))},
            compiler_params=_cparams(("arbitrary", "arbitrary"), vmem),
            name="ffn_up_conv_gate",
        )(h, *([w_up] * (2 * nblk)), *([conv_w] * (2 * nblk)), *([conv_b] * (2 * nblk)), *carried)

    nblk = min(FFN_BLOCKS_PER_STEP, nb)
    main_steps = nb // nblk
    act = run(0, main_steps, nblk, ())
    if main_steps * nblk < nb:
        act = run(main_steps * nblk, nb - main_steps * nblk, 1, (act,))
    return act


def _rope(x, cos, sin_signed):
    lane = lax.broadcasted_iota(jnp.int32, x.shape, 1)
    n = x.shape[1]
    swapped = jnp.where((lane & 1) == 0, pltpu.roll(x, n - 1, 1), pltpu.roll(x, 1, 1))
    return x * cos + swapped * sin_signed


def _softmax_parts(s, sc):
    m = jnp.max(s, axis=-1, keepdims=True)
    if sc is not None:
        m = jnp.maximum(m, jnp.max(sc, axis=-1, keepdims=True))
    e = jnp.exp(s - m)
    tot = jnp.sum(e, axis=-1, keepdims=True)
    ec = None
    if sc is not None:
        ec = jnp.exp(sc - m)
        tot = tot + jnp.sum(ec, axis=-1, keepdims=True)
    return e, ec, 1.0 / tot


def _attn_kernel(*refs, latent, heads, tq, lq, lam_init, n_alias):
    if latent:
        q_ref, k_ref, v_ref, kc_ref, vc_ref, cos_ref, sin_ref, lv_ref, g_ref = refs[:9]
        o_ref, kb_s, vb_s, kcb_s, vcb_s = refs[9 + n_alias:]
    else:
        q_ref, k_ref, v_ref, lv_ref, g_ref = refs[:5]
        o_ref, ko_ref, vo_ref, kb_s, vb_s = refs[5 + n_alias:]
        ko_ref[0, 0] = k_ref[...]
        vo_ref[0, 0] = v_ref[...]
    lv = lv_ref[...]
    lam = (jnp.exp(jnp.sum(lv[0:1] * lv[1:2], axis=-1, keepdims=True))
           - jnp.exp(jnp.sum(lv[2:3] * lv[3:4], axis=-1, keepdims=True)) + lam_init)
    gain = g_ref[...] * (1.0 - lam_init)
    scale = QK_DIM ** -0.5
    lo = lax.broadcasted_iota(jnp.int32, (tq, V_DIM), 1) < QK_DIM

    for h in range(heads):
        cols = slice(h * V_DIM, (h + 1) * V_DIM)
        k = k_ref[:, cols]
        if latent:
            k = _rope(k, cos_ref[...], sin_ref[...])
            kcb_s[...] = kc_ref[0, 0, :, cols].astype(BF16)
            vcb_s[...] = vc_ref[0, 0, :, cols].astype(BF16)
        kb_s[...] = k.astype(BF16)
        vb_s[...] = v_ref[:, cols].astype(BF16)

        def q_block(i, carry, cols=cols):
            rows = pl.ds(0, tq) if lq == tq else pl.ds(pl.multiple_of(i * tq, tq), tq)
            q = q_ref[rows, cols]
            if latent:
                q = _rope(q, cos_ref[rows, :], sin_ref[rows, :])
            q = q * scale
            q1 = jnp.where(lo, q, 0.0).astype(BF16)
            q2 = jnp.where(lo, 0.0, q).astype(BF16)
            kb = kb_s[...]
            s1 = _dot_nt(q1, kb)
            s2 = _dot_nt(q2, kb)
            s1c = s2c = None
            if latent:
                kcb = kcb_s[...]
                s1c = _dot_nt(q1, kcb)
                s2c = _dot_nt(q2, kcb)
            e1, e1c, r1 = _softmax_parts(s1, s1c)
            e2, e2c, r2 = _softmax_parts(s2, s2c)
            r2 = r2 * lam
            o = _dot((e1 * r1 - e2 * r2).astype(BF16), vb_s[...])
            if latent:
                o = o + _dot((e1c * r1 - e2c * r2).astype(BF16), vcb_s[...])
            rr = lax.rsqrt(jnp.mean(o * o, axis=-1, keepdims=True) + NORM_EPS)
            o_ref[rows, cols] = (o * rr * gain).astype(o_ref.dtype)
            return carry

        if lq == tq:
            q_block(0, 0)
        else:
            lax.fori_loop(0, lq // tq, q_block, 0, unroll=2)


def _attn_call(p, lv, g, dims, lam_init, latent, layer, carried, cache_k=None, cache_v=None,
               cos=None, sin=None):
    n_heads = dims.att_heads
    if latent:
        nb, lq, heads, row_off = dims.b_lat, dims.l_lat, 1, dims.n_ctx // dims.l_lat
    else:
        nb, lq, heads, row_off = dims.b_ctx, dims.l_ctx, _pick(n_heads, (4, 2, 1)), 0
    tq = min(lq, 256)
    hb = n_heads // heads
    w = heads * V_DIM
    q_spec = pl.BlockSpec((lq, w), lambda b, h: (b + row_off, h))
    k_spec = pl.BlockSpec((lq, w), lambda b, h: (b + row_off, hb + h))
    v_spec = pl.BlockSpec((lq, w), lambda b, h: (b + row_off, 2 * hb + h))
    small = [pl.BlockSpec((4, QK_DIM), lambda b, h: (0, 0)),
             pl.BlockSpec((1, V_DIM), lambda b, h: (0, 0))]
    scratch = [pltpu.VMEM((lq, V_DIM), BF16), pltpu.VMEM((lq, V_DIM), BF16)]
    if latent:
        past = cache_k.shape[2]
        c_spec = pl.BlockSpec((1, 1, past, w), lambda b, h: (b, layer, 0, h))
        t_spec = pl.BlockSpec((lq, V_DIM), lambda b, h: (0, 0))
        in_specs = [q_spec, k_spec, v_spec, c_spec, c_spec, t_spec, t_spec] + small
        args = (p, p, p, cache_k, cache_v, cos, sin, lv, g.reshape(1, V_DIM))
        scratch += [pltpu.VMEM((past, V_DIM), BF16), pltpu.VMEM((past, V_DIM), BF16)]
        lk = lq + past
    else:
        in_specs = [q_spec, k_spec, v_spec] + small
        args = (p, p, p, lv, g.reshape(1, V_DIM))
        lk = lq
    n_in = len(args)
    att_shape = jax.ShapeDtypeStruct((dims.n_ctx + dims.n_lat, n_heads * V_DIM), BF16)
    att_spec = pl.BlockSpec((lq, w), lambda b, h: (b + row_off, h))
    if latent:
        out_shape, out_specs = att_shape, att_spec
        aliases = {n_in: 0}
    else:
        kv_shape = jax.ShapeDtypeStruct((nb, dims.depth, lq, n_heads * V_DIM), F32)
        kv_spec = pl.BlockSpec((1, 1, lq, w), lambda b, h: (b, layer, 0, h))
        out_shape, out_specs = (att_shape, kv_shape, kv_shape), (att_spec, kv_spec, kv_spec)
        aliases = {n_in + i: 1 + i for i in range(len(carried))}
    vmem = 2 * 6 * lq * w * 4 + 12 * tq * lk * 4 + (8 << 20)
    return pl.pallas_call(
        functools.partial(_attn_kernel, latent=latent, heads=heads, tq=tq, lq=lq,
                          lam_init=lam_init, n_alias=len(carried)),
        grid=(nb, hb),
        in_specs=in_specs + [_ANY_SPEC] * len(carried),
        out_specs=out_specs,
        out_shape=out_shape,
        input_output_aliases=aliases,
        scratch_shapes=scratch,
        compiler_params=_cparams(("arbitrary", "arbitrary"), vmem),
        name="diff_attention_latent" if latent else "diff_attention_context",
    )(*args, *carried)


def _softplus(x):
    return jnp.maximum(x, 0.0) + jnp.log1p(jnp.exp(-jnp.abs(x)))


def _ssd_kernel(*refs, seq_len, latent, n_alias):
    it = iter(refs)
    x_ref, b_ref, c_ref, z_ref, dt_ref = (next(it) for _ in range(5))
    cwx_ref, cbx_ref, cwb_ref, cbb_ref, cwc_ref, cbc_ref = (next(it) for _ in range(6))
    dtb_ref, alog_ref, d_ref = (next(it) for _ in range(3))
    if latent:
        sf0_ref, sb0_ref = next(it), next(it)
    for _ in range(n_alias):
        next(it)
    if latent:
        y_ref = next(it)
    else:
        y_ref, sf_ref, sb_ref = next(it), next(it), next(it)
    (xs_s, b_s, c_s, bt_s, cs_s, rcs_s, dtxf_s, dtxb_s, csxf_s, rcsxb_s, y_s, st_s) = it

    ch = SSD_CHUNK
    nc = seq_len // ch

    def conv_silu(u_ref, w_ref, bias_ref):
        u = u_ref[...]
        t = lax.broadcasted_iota(jnp.int32, u.shape, 0)
        return _silu(_conv3_rows(u, w_ref, bias_ref, t == 0, t == seq_len - 1))

    xs_s[...] = conv_silu(x_ref, cwx_ref, cbx_ref)
    bm = conv_silu(b_ref, cwb_ref, cbb_ref)
    b_s[...] = bm.astype(BF16)
    for c in range(nc):
        bt_s[c] = bm[c * ch:(c + 1) * ch, :].T.astype(BF16)
    c_s[...] = conv_silu(c_ref, cwc_ref, cbc_ref).astype(BF16)

    dt = _softplus(dt_ref[...] + dtb_ref[...])
    dta = dt * (-jnp.exp(alog_ref[...]))
    ri = lax.broadcasted_iota(jnp.int32, (ch, ch), 0)
    ci = lax.broadcasted_iota(jnp.int32, (ch, ch), 1)
    lower = ri >= ci
    upper = ri <= ci
    t_low = jnp.where(lower, 1.0, 0.0).astype(BF16)
    t_up = jnp.where(upper, 1.0, 0.0).astype(BF16)
    for c in range(nc):
        blk = dta[c * ch:(c + 1) * ch, :]
        cs_s[c * ch:(c + 1) * ch, :] = _dot_f32_rhs(t_low, blk)
        rcs_s[c * ch:(c + 1) * ch, :] = _dot_f32_rhs(t_up, blk)
    er = lax.broadcasted_iota(jnp.int32, (LANE, SSD_GW), 0)
    eh = lax.broadcasted_iota(jnp.int32, (LANE, SSD_GW), 1) >> int(math.log2(SSD_P))
    exp_f = jnp.where(er == eh, 1.0, 0.0).astype(BF16)
    exp_b = jnp.where(er == eh + SSD_HPG, 1.0, 0.0).astype(BF16)
    dtxf_s[...] = _dot_f32_lhs(dt, exp_f)
    dtxb_s[...] = _dot_f32_lhs(dt, exp_b)
    csxf_s[...] = _dot_f32_lhs(cs_s[...], exp_f)
    rcsxb_s[...] = _dot_f32_lhs(rcs_s[...], exp_b)

    lo = lax.broadcasted_iota(jnp.int32, (ch, LANE), 1) < SSD_P

    def chunk_step(c, forward):
        rows = pl.ds(pl.multiple_of(c * ch, ch), ch)
        if forward:
            dtx, ex, cs, base, tri = dtxf_s[rows, :], csxf_s[rows, :], cs_s[rows, :], 0, lower
        else:
            dtx, ex, cs, base, tri = dtxb_s[rows, :], rcsxb_s[rows, :], rcs_s[rows, :], SSD_HPG, upper
        xdt = xs_s[rows, :] * dtx
        cc = c_s[rows, :]
        cb = _dot_nt(cc, b_s[rows, :])
        st = st_s[...]
        y_off = _dot(cc, st.astype(BF16)) * jnp.exp(ex)
        cs_t = cs.T
        for j in range(SSD_GW // LANE):
            lanes = slice(j * LANE, (j + 1) * LANE)
            xp = xdt[:, lanes]
            acc = y_off[:, lanes]
            for k in range(LANE // SSD_P):
                hh = base + j * (LANE // SSD_P) + k
                diff = cs[:, hh:hh + 1] - cs_t[hh:hh + 1, :]
                decay = jnp.exp(jnp.where(tri, diff, -jnp.inf))
                g = (cb * decay).astype(BF16)
                xm = jnp.where(lo if k == 0 else jnp.logical_not(lo), xp, 0.0).astype(BF16)
                acc = acc + _dot(g, xm)
            if forward:
                y_s[rows, lanes] = acc
            else:
                y_s[rows, lanes] = y_s[rows, lanes] + acc
        edge = ex[ch - 1:ch, :] if forward else ex[0:1, :]
        xd = (xdt * jnp.exp(edge - ex)).astype(BF16)
        st_s[...] = jnp.exp(edge) * st + _dot(bt_s[c], xd)

    def fwd_body(c, carry):
        chunk_step(c, True)
        return carry

    def bwd_body(i, carry):
        chunk_step(nc - 1 - i, False)
        return carry

    st_s[...] = sf0_ref[0, 0].T if latent else jnp.zeros(st_s.shape, F32)
    lax.fori_loop(0, nc, fwd_body, 0)
    if not latent:
        sf_ref[0, 0] = st_s[...].T
    st_s[...] = sb0_ref[0, 0].T if latent else jnp.zeros(st_s.shape, F32)
    lax.fori_loop(0, nc, bwd_body, 0)
    if not latent:
        sb_ref[0, 0] = st_s[...].T

    y = y_s[...] + d_ref[...] * xs_s[...]
    y_ref[...] = y * _silu(z_ref[...])


def _ssd_call(p, lw, dims, latent, layer, carried, init_f=None, init_b=None):
    groups = dims.ssd_groups
    gw = SSD_GW
    if latent:
        nb, sl, row_off = dims.b_lat, dims.l_lat, dims.n_ctx // dims.l_lat
    else:
        nb, sl, row_off = dims.b_ctx, dims.l_ctx, 0
    nc = sl // SSD_CHUNK
    o = dims.offs
    zb, xb, bb, cb, db = o.z // gw, o.x // gw, o.b // LANE, o.c // LANE, o.dt // LANE
    in_specs = [
        pl.BlockSpec((sl, gw), lambda b, g: (b + row_off, xb + g)),
        pl.BlockSpec((sl, LANE), lambda b, g: (b + row_off, bb + g)),
        pl.BlockSpec((sl, LANE), lambda b, g: (b + row_off, cb + g)),
        pl.BlockSpec((sl, gw), lambda b, g: (b + row_off, zb + g)),
        pl.BlockSpec((sl, LANE), lambda b, g: (b + row_off, db + g)),
        pl.BlockSpec((3, gw), lambda b, g: (0, g)),
        pl.BlockSpec((1, gw), lambda b, g: (0, g)),
        pl.BlockSpec((3, LANE), lambda b, g: (0, dims.ssd_width // LANE + g)),
        pl.BlockSpec((1, LANE), lambda b, g: (0, dims.ssd_width // LANE + g)),
        pl.BlockSpec((3, LANE), lambda b, g: (0, dims.ssd_width // LANE + groups + g)),
        pl.BlockSpec((1, LANE), lambda b, g: (0, dims.ssd_width // LANE + groups + g)),
        pl.BlockSpec((1, LANE), lambda b, g: (0, g)),
        pl.BlockSpec((1, LANE), lambda b, g: (0, g)),
        pl.BlockSpec((1, gw), lambda b, g: (0, g)),
    ]
    args = [p, p, p, p, p, lw["ssd_conv_w"], lw["ssd_conv_b"], lw["ssd_conv_w"], lw["ssd_conv_b"],
            lw["ssd_conv_w"], lw["ssd_conv_b"], lw["ssd_dt_bias"], lw["ssd_a_log"], lw["ssd_d"]]
    y_shape = jax.ShapeDtypeStruct((dims.n_ctx + dims.n_lat, dims.ssd_width), F32)
    y_spec = pl.BlockSpec((sl, gw), lambda b, g: (b + row_off, g))
    s_spec = pl.BlockSpec((1, 1, gw, SSD_N), lambda b, g: (b, layer, g, 0))
    if latent:
        in_specs += [s_spec, s_spec]
        args += [init_f, init_b]
        out_shape, out_specs = y_shape, y_spec
        aliases = {len(args): 0}
    else:
        s_shape = jax.ShapeDtypeStruct((nb, dims.depth, dims.ssd_width, SSD_N), F32)
        out_shape, out_specs = (y_shape, s_shape, s_shape), (y_spec, s_spec, s_spec)
        aliases = {len(args) + i: 1 + i for i in range(len(carried))}
    scratch = [
        pltpu.VMEM((sl, gw), F32), pltpu.VMEM((sl, LANE), BF16), pltpu.VMEM((sl, LANE), BF16),
        pltpu.VMEM((nc, SSD_CHUNK, SSD_CHUNK), BF16),
        pltpu.VMEM((sl, LANE), F32), pltpu.VMEM((sl, LANE), F32),
        pltpu.VMEM((sl, gw), F32), pltpu.VMEM((sl, gw), F32),
        pltpu.VMEM((sl, gw), F32), pltpu.VMEM((sl, gw), F32),
        pltpu.VMEM((sl, gw), F32), pltpu.VMEM((SSD_N, gw), F32),
    ]
    vmem = 2 * sl * (3 * gw + 3 * LANE) * 4 + sl * (8 * gw + 4 * LANE) * 4 + 8 * sl * gw * 4 + (8 << 20)
    return pl.pallas_call(
        functools.partial(_ssd_kernel, seq_len=sl, latent=latent, n_alias=len(carried)),
        grid=(nb, groups),
        in_specs=in_specs + [_ANY_SPEC] * len(carried),
        out_specs=out_specs,
        out_shape=out_shape,
        input_output_aliases=aliases,
        scratch_shapes=scratch,
        compiler_params=_cparams(("arbitrary", "arbitrary"), vmem),
        name="ssd_latent" if latent else "ssd_context",
    )(*args, *carried)


def _dft_mats(l):
    n = 3 * l // 2
    nh = n // 2
    k = np.arange(nh, dtype=np.int64)[:, None]
    t = np.arange(l, dtype=np.int64)[None, :]
    ang = 2.0 * np.pi * ((k * t) % n).astype(np.float64) / n
    fwd = np.concatenate([np.cos(ang), -np.sin(ang)], axis=0)
    fwd[nh, :] = np.where(np.arange(l) % 2 == 0, 1.0, -1.0)
    pos = (np.arange(l, dtype=np.int64) + l // 2)[:, None]
    kk = np.arange(nh, dtype=np.int64)[None, :]
    ang2 = 2.0 * np.pi * ((pos * kk) % n).astype(np.float64) / n
    inv = np.concatenate([2.0 * np.cos(ang2), -2.0 * np.sin(ang2)], axis=1) / n
    inv[:, 0] = 1.0 / n
    inv[:, nh] = np.where(pos[:, 0] % 2 == 0, 1.0, -1.0) / n

    def split(m):
        hi = m.astype(np.float32).astype(BF16)
        lo = (m - hi.astype(np.float64)).astype(np.float32).astype(BF16)
        return hi, lo

    return split(fwd) + split(inv)


def _hy_feats(l):
    off = jnp.arange(l, dtype=F32) - (l // 2)
    band = HY_FREQ_BASE ** (-jnp.arange(HY_BANDS, dtype=F32) / HY_BANDS)
    ang = off[:, None] * band
    feats = jnp.concatenate([off[:, None] / l, jnp.sin(ang), jnp.cos(ang)], axis=-1)
    return jnp.pad(feats, ((0, 0), (0, LANE - feats.shape[1])))


def _hy_filter_kernel(feats_ref, w1_ref, b1_ref, w2_ref, b2_ref, fr_ref, w3_ref, dl_ref,
                      fh_ref, fl_ref, o_ref, *, seq_len):
    fr = fr_ref[0]
    h1 = jnp.sin(fr[0:1] * (_dot_hp(feats_ref[...], w1_ref[0]) + b1_ref[0]))
    h2 = jnp.sin(fr[1:2] * (_dot_hp(h1, w2_ref[0]) + b2_ref[0]))
    filt = _dot_hp(h2, w3_ref[0])
    off = lax.broadcasted_iota(jnp.int32, filt.shape, 0).astype(F32) - (seq_len // 2)
    dist = jnp.abs(off) * (2.0 / seq_len)
    h = filt * jnp.exp(-dist * dl_ref[...])
    hh, hl = _split_bf16(h, 2)
    fh = fh_ref[...]
    o_ref[0, 0] = _dot(fh, hh) + _dot(fh, hl) + _dot(fl_ref[...], hh)


def _hy_filter_call(seq_len, hw, mats, dims):
    depth, hy = dims.depth, dims.hy_width
    cb = HY_CB
    ncb = hy // cb
    fh, fl = mats[0], mats[1]
    n = fh.shape[0]
    feats = _hy_feats(seq_len)
    deltas = jnp.linspace(HY_MIN_DECAY, HY_MAX_DECAY, hy, dtype=F32).reshape(1, hy)
    vmem = 2 * 2 * (n * seq_len * 2) + 6 * n * cb * 4 + (8 << 20)
    return pl.pallas_call(
        functools.partial(_hy_filter_kernel, seq_len=seq_len),
        grid=(depth, 2, ncb),
        in_specs=[
            pl.BlockSpec((seq_len, LANE), lambda l, o, c: (0, 0)),
            pl.BlockSpec((1, LANE, LANE), lambda l, o, c: (l, 0, 0)),
            pl.BlockSpec((1, 1, LANE), lambda l, o, c: (l, 0, 0)),
            pl.BlockSpec((1, LANE, LANE), lambda l, o, c: (l, 0, 0)),
            pl.BlockSpec((1, 1, LANE), lambda l, o, c: (l, 0, 0)),
            pl.BlockSpec((1, 2, LANE), lambda l, o, c: (l, 0, 0)),
            pl.BlockSpec((1, LANE, cb), lambda l, o, c: (l, 0, o * ncb + c)),
            pl.BlockSpec((1, cb), lambda l, o, c: (0, c)),
            pl.BlockSpec((n, seq_len), lambda l, o, c: (0, 0)),
            pl.BlockSpec((n, seq_len), lambda l, o, c: (0, 0)),
        ],
        out_specs=pl.BlockSpec((1, 1, n, cb), lambda l, o, c: (l, o, 0, c)),
        out_shape=jax.ShapeDtypeStruct((depth, 2, n, hy), F32),
        compiler_params=_cparams(("arbitrary",) * 3, vmem),
        name="hyena_filter_spectrum",
    )(feats, hw["w1"], hw["b1"], hw["w2"], hw["b2"], hw["freq"], hw["w3"], deltas, fh, fl)


def _hy_conv_kernel(*refs, seq_len, n_seq, n_alias):
    (x1_ref, x2_ref, v_ref, cw1_ref, cb1_ref, cw2_ref, cb2_ref, cwv_ref, cbv_ref,
     hf_ref, bias_ref, fh_ref, fl_ref, gh_ref, gl_ref) = refs[:15]
    o_ref = refs[15 + n_alias]
    nh = fh_ref.shape[0] // 2
    cb = o_ref.shape[1]
    t = lax.broadcasted_iota(jnp.int32, (seq_len, cb), 0)
    first = t == 0
    last = t == seq_len - 1
    bin0 = lax.broadcasted_iota(jnp.int32, (nh, cb), 0) == 0
    fh = fh_ref[...]
    gh = gh_ref[...]
    rows = [slice(s * seq_len, (s + 1) * seq_len) for s in range(n_seq)]
    gates = [(_conv3_rows(x1_ref[r, :], cw1_ref, cb1_ref, first, last),
              _conv3_rows(x2_ref[r, :], cw2_ref, cb2_ref, first, last)) for r in rows]
    zz = [_conv3_rows(v_ref[r, :], cwv_ref, cbv_ref, first, last) for r in rows]
    for o in range(2):
        hr, hi = hf_ref[0, o, :nh, :], hf_ref[0, o, nh:, :]
        uf = []
        for z in zz:
            zh, zl = _split_bf16(z, 2)
            uf.append(_dot(fh, zh) + _dot(fh, zl) + _dot(fl_ref[...], zh))
        ys = []
        for u in uf:
            ur, ui = u[:nh], u[nh:]
            ii = ui * hi
            yr = ur * hr - jnp.where(bin0, 0.0, ii)
            yi = jnp.where(bin0, ii, ur * hi + ui * hr)
            yh, yl = _split_bf16(jnp.concatenate([yr, yi], axis=0), 2)
            ys.append(_dot(gh, yh) + _dot(gh, yl) + _dot(gl_ref[...], yh))
        zz = [g[o] * (y + z * bias_ref[o:o + 1, :]) for g, y, z in zip(gates, ys, zz)]
    for r, z in zip(rows, zz):
        o_ref[r, :] = z


def _hy_conv_call(p, hf, lw, mats, dims, layer, latent, carried):
    hy = dims.hy_width
    cb = HY_CB
    ncb = hy // cb
    if latent:
        nb, sl, row_start = dims.b_lat, dims.l_lat, dims.n_ctx
    else:
        nb, sl, row_start = dims.b_ctx, dims.l_ctx, 0
    n_seq = max(n for n in (8, 4, 2, 1) if nb % n == 0 and (n == 1 or n * sl <= HY_ROWS_PER_STEP))
    rows = n_seq * sl
    row_off = row_start // rows
    n = mats[0].shape[0]
    hb = dims.offs.hy // cb
    const = lambda c, b: (0, 0)
    in_specs = [pl.BlockSpec((rows, cb), lambda c, b, k=k: (b + row_off, hb + k * ncb + c)) for k in range(3)]
    for k in range(3):
        in_specs += [pl.BlockSpec((3, cb), lambda c, b, k=k: (0, k * ncb + c)),
                     pl.BlockSpec((1, cb), lambda c, b, k=k: (0, k * ncb + c))]
    once = pl.Buffered(1)
    in_specs += [
        pl.BlockSpec((1, 2, n, cb), lambda c, b: (layer, 0, 0, c), pipeline_mode=once),
        pl.BlockSpec((2, cb), lambda c, b: (0, c)),
        pl.BlockSpec((n, sl), const, pipeline_mode=once),
        pl.BlockSpec((n, sl), const, pipeline_mode=once),
        pl.BlockSpec((sl, n), const, pipeline_mode=once),
        pl.BlockSpec((sl, n), const, pipeline_mode=once),
    ]
    cw, cbias = lw["hy_conv_w"], lw["hy_conv_b"]
    vmem = (4 * (n * sl * 2) + 2 * n * cb * 4 + 2 * 4 * rows * cb * 4
            + n_seq * (6 * n + 8 * sl) * cb * 4 + (8 << 20))
    return pl.pallas_call(
        functools.partial(_hy_conv_kernel, seq_len=sl, n_seq=n_seq, n_alias=len(carried)),
        grid=(ncb, nb // n_seq),
        in_specs=in_specs + [_ANY_SPEC] * len(carried),
        out_specs=pl.BlockSpec((rows, cb), lambda c, b: (b + row_off, c)),
        out_shape=jax.ShapeDtypeStruct((dims.n_ctx + dims.n_lat, hy), F32),
        input_output_aliases={15 + i: i for i in range(len(carried))},
        compiler_params=_cparams(("arbitrary", "arbitrary"), vmem),
        name="hyena_conv_latent" if latent else "hyena_conv_context",
    )(p, p, p, cw, cbias, cw, cbias, cw, cbias, hf, lw["hy_bias"], *mats, *carried)


class _Offsets:
    def __init__(self, att_w, ssd_w, groups, hy_w):
        self.q, self.k, self.v = 0, att_w, 2 * att_w
        self.z = 3 * att_w
        self.x = self.z + ssd_w
        self.b = self.x + ssd_w
        self.c = self.b + groups * SSD_N
        self.hy = self.c + groups * SSD_N
        self.dt = self.hy + 3 * hy_w
        self.total = self.dt + groups * LANE


class _Dims:
    pass


def _make_dims(x_prompt, x_sample, cache_k, state_ssd_fwd, w_in, hy_norm_g, ffn_w_down):
    d = _Dims()
    d.b_ctx, d.l_ctx, d.d_model = x_prompt.shape
    d.b_lat, d.l_lat, _ = x_sample.shape
    d.n_ctx = d.b_ctx * d.l_ctx
    d.n_lat = d.b_lat * d.l_lat
    d.depth = w_in.shape[0]
    d.att_heads = cache_k.shape[3]
    d.att_width = d.att_heads * V_DIM
    d.ssd_heads = state_ssd_fwd.shape[2]
    d.ssd_width = d.ssd_heads * SSD_P
    d.ssd_groups = d.ssd_heads // SSD_HPG
    d.hy_width = hy_norm_g.shape[-1]
    d.d_ff = ffn_w_down.shape[1]
    d.offs = _Offsets(d.att_width, d.ssd_width, d.ssd_groups, d.hy_width)
    d.tm = d.l_lat
    assert d.l_lat % d.l_ctx == 0 and d.n_ctx % d.tm == 0 and d.l_ctx % 256 == 0
    assert d.l_lat & (d.l_lat - 1) == 0 and d.l_ctx & (d.l_ctx - 1) == 0
    assert d.offs.hy % HY_CB == 0 and d.hy_width % HY_CB == 0 and d.d_ff % FFN_TN == 0
    assert d.ssd_heads % SSD_HPG == 0
    return d


def _dt_gather(dims):
    src = np.zeros((dims.ssd_groups * LANE,), np.int32)
    valid = np.zeros((dims.ssd_groups * LANE,), np.float32)
    for g in range(dims.ssd_groups):
        for direction in range(2):
            for hh in range(SSD_HPG):
                lane = g * LANE + direction * SSD_HPG + hh
                src[lane] = direction * dims.ssd_heads + g * SSD_HPG + hh
                valid[lane] = 1.0
    return src, valid


def _rope_tables(seq_len):
    rows = seq_len // GRID_W
    r, col = jnp.meshgrid(jnp.arange(rows), jnp.arange(GRID_W), indexing="ij")
    npair = QK_DIM // 4
    inv = ROPE_BASE ** (-jnp.arange(npair, dtype=F32) / npair)
    ang = jnp.concatenate([r.reshape(-1, 1).astype(F32) * inv, col.reshape(-1, 1).astype(F32) * inv], axis=-1)
    cos, sin = jnp.cos(ang), jnp.sin(ang)
    cos2 = jnp.repeat(cos, 2, axis=-1)
    sin2 = jnp.stack([-sin, sin], axis=-1).reshape(seq_len, QK_DIM)
    return jnp.tile(cos2, (1, 2)), jnp.tile(sin2, (1, 2))


def kernel(x_prompt, x_sample, cache_k, cache_v, state_ssd_fwd, state_ssd_bwd, c, c_ctx, norm1_g, norm2_g, w_ada, b_ada, w_in, att_lambda, att_subln_g, ssd_conv_w, ssd_conv_b, ssd_dt_bias, ssd_a_log, ssd_d, ssd_norm_g, hy_conv_w, hy_conv_b, hy_w1, hy_b1, hy_w2, hy_b2, hy_freq, hy_w3, hy_bias, hy_norm_g, w_out, ffn_w_up, ffn_conv_w, ffn_conv_b, ffn_w_down, final_norm_g):
    dims = _make_dims(x_prompt, x_sample, cache_k, state_ssd_fwd, w_in, hy_norm_g, ffn_w_down)
    d, depth, offs = dims.d_model, dims.depth, dims.offs
    n_ctx = dims.n_ctx

    mod_rows = 16 * ((1 + dims.b_lat + 15) // 16)
    cond = jnp.zeros((mod_rows, d), F32).at[0].set(c_ctx).at[1:1 + dims.b_lat].set(c)
    mod_all = _ada_call(cond, w_ada, b_ada).reshape(depth, mod_rows, 6, 1, d)

    pad_h = LANE - hy_w1.shape[-1]
    hw = {
        "w1": jnp.pad(hy_w1, ((0, 0), (0, LANE - hy_w1.shape[1]), (0, pad_h))),
        "b1": jnp.pad(hy_b1, ((0, 0), (0, pad_h)))[:, None, :],
        "w2": jnp.pad(hy_w2, ((0, 0), (0, pad_h), (0, pad_h))),
        "b2": jnp.pad(hy_b2, ((0, 0), (0, pad_h)))[:, None, :],
        "freq": jnp.pad(hy_freq, ((0, 0), (0, 0), (0, pad_h))),
        "w3": jnp.pad(hy_w3, ((0, 0), (0, pad_h), (0, 0))),
    }
    mats_ctx = _dft_mats(dims.l_ctx)
    mats_lat = _dft_mats(dims.l_lat)
    hf_ctx = _hy_filter_call(dims.l_ctx, hw, mats_ctx, dims)
    hf_lat = _hy_filter_call(dims.l_lat, hw, mats_lat, dims)

    cos_t, sin_t = _rope_tables(dims.l_lat)
    dt_src, dt_valid = _dt_gather(dims)
    ck = cache_k.reshape(dims.b_lat, depth, cache_k.shape[2], dims.att_width)
    cv = cache_v.reshape(dims.b_lat, depth, cache_v.shape[2], dims.att_width)
    sf0 = state_ssd_fwd.reshape(dims.b_lat, depth, dims.ssd_width, SSD_N)
    sb0 = state_ssd_bwd.reshape(dims.b_lat, depth, dims.ssd_width, SSD_N)

    x = jnp.concatenate([x_prompt.reshape(n_ctx, d), x_sample.reshape(dims.n_lat, d)], axis=0)
    new_kv, new_st = (), ()
    dt_cols = 2 * dims.ssd_heads
    for l in range(depth):
        lam_init = 0.8 - 0.6 * math.exp(-0.3 * l)
        mod = mod_all[l]
        w_l = w_in[l]
        w_main = w_l[:, :offs.hy]
        w_dt = jnp.take(w_l[:, offs.hy:offs.hy + dt_cols], dt_src, axis=1) * dt_valid
        w_hy = w_l[:, offs.hy + dt_cols:]
        w_proj = jnp.concatenate([w_main, w_hy, w_dt], axis=1).astype(BF16)
        dtb = jnp.take(ssd_dt_bias[l].reshape(-1), dt_src) * dt_valid
        alog = jnp.take(ssd_a_log[l].reshape(-1), dt_src) * dt_valid
        lw = {
            "ssd_conv_w": ssd_conv_w[l], "ssd_conv_b": ssd_conv_b[l].reshape(1, -1),
            "ssd_dt_bias": dtb.reshape(1, -1), "ssd_a_log": alog.reshape(1, -1),
            "ssd_d": jnp.repeat(ssd_d[l], SSD_P).reshape(1, -1),
            "hy_conv_w": hy_conv_w[l], "hy_conv_b": hy_conv_b[l].reshape(1, -1), "hy_bias": hy_bias[l],
        }

        h = _norm_mod_call(x, norm1_g[l], mod, 0, 1, dims)
        p = _mm_call(h, w_proj, dims)

        att, new_k, new_v = _attn_call(p, att_lambda[l], att_subln_g[l], dims, lam_init, False, l, new_kv)
        new_kv = (new_k, new_v)
        att = _attn_call(p, att_lambda[l], att_subln_g[l], dims, lam_init, True, l, (att,),
                         cache_k=ck, cache_v=cv, cos=cos_t, sin=sin_t)
        ys, new_sf, new_sb = _ssd_call(p, lw, dims, False, l, new_st)
        new_st = (new_sf, new_sb)
        ys = _ssd_call(p, lw, dims, True, l, (ys,), init_f=sf0, init_b=sb0)
        zz = _hy_conv_call(p, hf_ctx, lw, mats_ctx, dims, l, False, ())
        zz = _hy_conv_call(p, hf_lat, lw, mats_lat, dims, l, True, (zz,))

        y_ssd = _norm_call(ys, ssd_norm_g[l], BF16, name="ssd_norm")
        y_hy = _norm_call(zz, hy_norm_g[l], BF16, name="hyena_norm")
        w_o = w_out[l].astype(BF16)
        a_w, s_w = dims.att_width, dims.ssd_width
        x = _mm_res_call([(att, w_o[:a_w]), (y_ssd, w_o[a_w:a_w + s_w]), (y_hy, w_o[a_w + s_w:])],
                         x, mod, 2, dims, (1024, 768, 512, 384, 256, 128))

        h2 = _norm_mod_call(x, norm2_g[l], mod, 3, 4, dims)
        act = _ffn_up_call(h2, ffn_w_up[l].astype(BF16), ffn_conv_w[l], ffn_conv_b[l], dims)
        x = _mm_res_call([(act, ffn_w_down[l].astype(BF16))], x, mod, 5, dims, (256, 128),
                         single_buffer_a=True)

    y_prompt = _norm_call(x, final_norm_g, F32, 0, n_ctx, name="final_norm").reshape(x_prompt.shape)
    y_sample = _norm_call(x, final_norm_g, F32, n_ctx, dims.n_lat, name="final_norm").reshape(x_sample.shape)
    kv_shape = (dims.b_ctx, depth, dims.l_ctx, dims.att_heads, V_DIM)
    st_shape = (dims.b_ctx, depth, dims.ssd_heads, SSD_P, SSD_N)
    return (y_prompt, y_sample, new_kv[0].reshape(kv_shape), new_kv[1].reshape(kv_shape),
            new_st[0].reshape(st_shape), new_st[1].reshape(st_shape))
```

```python
import functools
import math

import numpy as np
import jax
import jax.numpy as jnp
from jax import lax
from jax.experimental import pallas as pl
from jax.experimental.pallas import tpu as pltpu

F32 = jnp.float32
BF16 = jnp.bfloat16

LANE = 128
V7X_VMEM_BYTES = 64 * 1024 * 1024
VMEM_CAP = V7X_VMEM_BYTES - 6 * 1024 * 1024

NORM_EPS = 1e-6
GRID_W = 64
ROPE_BASE = 10000.0
QK_DIM = 64
V_DIM = 128
SSD_P = 64
SSD_N = 128
SSD_CHUNK = 128
SSD_HPG = 6
SSD_GW = SSD_HPG * SSD_P
HY_BANDS = 16
HY_FREQ_BASE = 10000.0
HY_MIN_DECAY = -math.log(1e-2) / 1.5
HY_MAX_DECAY = -math.log(1e-2) / 0.3
HY_CB = 256
FFN_TN = 256
FFN_BLOCKS_PER_STEP = 3
HY_ROWS_PER_STEP = 2048

_ANY_SPEC = pl.BlockSpec(memory_space=pl.ANY)


def _cparams(sem, vmem_bytes):
    limit = int(min(VMEM_CAP, max(vmem_bytes, 16 * 1024 * 1024)))
    return pltpu.CompilerParams(dimension_semantics=sem, vmem_limit_bytes=limit)


def _pick(n, candidates):
    for c in candidates:
        if n % c == 0:
            return c
    raise ValueError(f"no block size in {candidates} divides {n}")


def _dot(a, b):
    return jnp.dot(a, b, preferred_element_type=F32)


def _dot_nt(a, b):
    return lax.dot_general(a, b, (((1,), (1,)), ((), ())), preferred_element_type=F32)


def _split_bf16(x, n):
    parts = []
    r = x
    for _ in range(n):
        h = r.astype(BF16)
        parts.append(h)
        r = r - h.astype(F32)
    return parts


def _dot_f32_lhs(a, b_bf16, terms=3):
    return sum(_dot(p, b_bf16) for p in _split_bf16(a, terms))


def _dot_f32_rhs(a_bf16, b):
    return sum(_dot(a_bf16, p) for p in _split_bf16(b, 3))


def _dot_hp(a, b):
    ah, al = _split_bf16(a, 2)
    bh, bl = _split_bf16(b, 2)
    return _dot(ah, bh) + _dot(ah, bl) + _dot(al, bh)


def _silu(x):
    return x * jax.nn.sigmoid(x)


def _conv3_rows(u, w_ref, b_ref, first, last):
    n = u.shape[0]
    prev = jnp.where(first, 0.0, pltpu.roll(u, 1, 0))
    nxt = jnp.where(last, 0.0, pltpu.roll(u, n - 1, 0))
    return prev * w_ref[0:1, :] + b_ref[...] + u * w_ref[1:2, :] + nxt * w_ref[2:3, :]


def _ada_kernel(c_ref, w_ref, b_ref, o_ref):
    s = _silu(c_ref[...]).astype(BF16)
    o_ref[0] = _dot(s, w_ref[0].astype(BF16)) + b_ref[0]


def _ada_call(cond, w_ada, b_ada):
    depth, d, n = w_ada.shape
    rows = cond.shape[0]
    tn = _pick(n, (512, 256, 128))
    vmem = 2 * (d * tn * 4) + d * tn * 2 + 4 * rows * (d + 2 * tn) * 4 + (4 << 20)
    return pl.pallas_call(
        _ada_kernel,
        grid=(depth, n // tn),
        in_specs=[
            pl.BlockSpec((rows, d), lambda l, j: (0, 0)),
            pl.BlockSpec((1, d, tn), lambda l, j: (l, 0, j)),
            pl.BlockSpec((1, 1, tn), lambda l, j: (l, 0, j)),
        ],
        out_specs=pl.BlockSpec((1, rows, tn), lambda l, j: (l, 0, j)),
        out_shape=jax.ShapeDtypeStruct((depth, rows, n), F32),
        compiler_params=_cparams(("arbitrary", "arbitrary"), vmem),
        name="ada_mod",
    )(cond, w_ada, b_ada.reshape(depth, 1, n))


def _norm_mod_kernel(x_ref, g_ref, sc_ref, sh_ref, o_ref):
    x = x_ref[...]
    r = lax.rsqrt(jnp.mean(x * x, axis=-1, keepdims=True) + NORM_EPS)
    y = x * r * g_ref[...]
    o_ref[...] = (y * (1.0 + sc_ref[0, 0]) + sh_ref[0, 0]).astype(o_ref.dtype)


def _norm_kernel(x_ref, g_ref, o_ref):
    x = x_ref[...]
    r = lax.rsqrt(jnp.mean(x * x, axis=-1, keepdims=True) + NORM_EPS)
    o_ref[...] = (x * r * g_ref[...]).astype(o_ref.dtype)


def _mod_row(i, tm, dims):
    start = i * tm
    lat = jnp.maximum(start - dims.n_ctx, 0) // dims.l_lat
    return jnp.where(start >= dims.n_ctx, lat + 1, 0)


def _norm_mod_call(x, g, mod, sh_idx, sc_idx, dims):
    m, d = x.shape
    tr = 256
    row = functools.partial(_mod_row, tm=tr, dims=dims)
    vmem = 2 * tr * d * (4 + 2) + 3 * tr * d * 4 + (4 << 20)
    return pl.pallas_call(
        _norm_mod_kernel,
        grid=(m // tr,),
        in_specs=[
            pl.BlockSpec((tr, d), lambda i: (i, 0)),
            pl.BlockSpec((1, d), lambda i: (0, 0)),
            pl.BlockSpec((1, 1, 1, d), lambda i: (row(i), sc_idx, 0, 0)),
            pl.BlockSpec((1, 1, 1, d), lambda i: (row(i), sh_idx, 0, 0)),
        ],
        out_specs=pl.BlockSpec((tr, d), lambda i: (i, 0)),
        out_shape=jax.ShapeDtypeStruct((m, d), BF16),
        compiler_params=_cparams(("arbitrary",), vmem),
        name="norm_mod",
    )(x, g.reshape(1, d), mod, mod)


def _norm_call(x, g, out_dtype, row_start=0, rows=None, name="norm"):
    m, d = x.shape
    rows = m if rows is None else rows
    tr = 256
    off = row_start // tr
    vmem = 2 * tr * d * 8 + 3 * tr * d * 4 + (4 << 20)
    return pl.pallas_call(
        _norm_kernel,
        grid=(rows // tr,),
        in_specs=[
            pl.BlockSpec((tr, d), lambda i: (i + off, 0)),
            pl.BlockSpec((1, d), lambda i: (0, 0)),
        ],
        out_specs=pl.BlockSpec((tr, d), lambda i: (i, 0)),
        out_shape=jax.ShapeDtypeStruct((rows, d), out_dtype),
        compiler_params=_cparams(("arbitrary",), vmem),
        name=name,
    )(x, g.reshape(1, d))


def _mm_kernel(a_ref, b_ref, o_ref):
    o_ref[...] = _dot(a_ref[...], b_ref[...]).astype(o_ref.dtype)


def _mm_res_kernel(*refs, n_terms):
    x_ref, g_ref, o_ref = refs[2 * n_terms:]
    acc = _dot(refs[0][...], refs[1][...])
    for i in range(1, n_terms):
        acc = acc + _dot(refs[2 * i][...], refs[2 * i + 1][...])
    o_ref[...] = x_ref[...] + g_ref[0, 0] * acc


def _mm_call(a, w_stack, layer, dims, out_dtype=F32):
    m, k = a.shape
    n = w_stack.shape[2]
    b = w_stack
    tm = dims.tm
    tn = _pick(n, (1024, 768, 512, 384, 256, 128))
    vmem = 2 * (tm * k * 2 + k * tn * 2 + tm * tn * 4) + tm * tn * 4 + (4 << 20)
    return pl.pallas_call(
        _mm_kernel,
        grid=(m // tm, n // tn),
        in_specs=[
            pl.BlockSpec((tm, k), lambda i, j: (i, 0)),
            pl.BlockSpec((None, k, tn), lambda i, j: (layer, 0, j)),
        ],
        out_specs=pl.BlockSpec((tm, tn), lambda i, j: (i, j)),
        out_shape=jax.ShapeDtypeStruct((m, n), out_dtype),
        compiler_params=_cparams(("arbitrary", "arbitrary"), vmem),
        name="proj",
    )(a, b)


def _mm_res_call(terms, x, mod, g_idx, dims, tn_candidates, single_buffer_a=False):
    m, n = x.shape
    k = sum(t[0].shape[1] for t in terms)
    tm = dims.tm
    tn = _pick(n, tn_candidates)
    row = functools.partial(_mod_row, tm=tm, dims=dims)
    a_bufs = 1 if single_buffer_a else 2
    vmem = a_bufs * tm * k * 2 + 2 * (k * tn * 2 + 2 * tm * tn * 4) + tm * tn * 4 + (4 << 20)
    in_specs, args = [], []
    for a, w_stack, layer, row_block in terms:
        ka = a.shape[1]
        if single_buffer_a:
            in_specs.append(pl.BlockSpec((tm, ka), lambda i, j: (i, 0), pipeline_mode=pl.Buffered(1)))
        else:
            in_specs.append(pl.BlockSpec((tm, ka), lambda i, j: (i, 0)))
        in_specs.append(pl.BlockSpec((None, ka, tn), lambda i, j, layer=layer, rb=row_block: (layer, rb, j)))
        args += [a, w_stack]
    in_specs += [pl.BlockSpec((tm, tn), lambda i, j: (i, j)),
                 pl.BlockSpec((1, 1, 1, tn), lambda i, j: (row(i), g_idx, 0, j))]
    return pl.pallas_call(
        functools.partial(_mm_res_kernel, n_terms=len(terms)),
        grid=(m // tm, n // tn),
        in_specs=in_specs,
        out_specs=pl.BlockSpec((tm, tn), lambda i, j: (i, j)),
        out_shape=jax.ShapeDtypeStruct((m, n), F32),
        compiler_params=_cparams(("arbitrary", "arbitrary"), vmem),
        name="proj_residual",
    )(*args, x, mod)


def _ffn_up_kernel(*refs, dims, nblk, n_alias):
    a_ref = refs[0]
    b_refs = refs[1:1 + 2 * nblk]
    cw_refs = refs[1 + 2 * nblk:1 + 4 * nblk]
    cb_refs = refs[1 + 4 * nblk:1 + 6 * nblk]
    o_ref = refs[1 + 6 * nblk + n_alias]
    a = a_ref[...]
    tm = a.shape[0]
    tn = b_refs[0].shape[1]
    seq = jnp.where(pl.program_id(0) * tm < dims.n_ctx, dims.l_ctx, dims.l_lat)
    t = lax.broadcasted_iota(jnp.int32, (tm, tn), 0) & (seq - 1)
    first = t == 0
    last = t == seq - 1
    for q in range(nblk):
        g = _conv3_rows(_dot(a, b_refs[2 * q][...]), cw_refs[2 * q], cb_refs[2 * q], first, last)
        v = _conv3_rows(_dot(a, b_refs[2 * q + 1][...]), cw_refs[2 * q + 1], cb_refs[2 * q + 1],
                        first, last)
        o_ref[:, q * tn:(q + 1) * tn] = (_silu(g) * v).astype(o_ref.dtype)


def _ffn_up_call(h, w_up, layer, conv_w, conv_b, dims):
    m, k = h.shape
    dff = w_up.shape[2] // 2
    tm = dims.tm
    tn = FFN_TN
    nb = dff // tn
    conv_b = conv_b.reshape(1, -1)

    def run(first_block, steps, nblk, carried):
        cols = []
        for q in range(nblk):
            cols += [lambda i, j, q=q: (0, first_block + nblk * j + q),
                     lambda i, j, q=q: (0, nb + first_block + nblk * j + q)]
        in_specs = [pl.BlockSpec((tm, k), lambda i, j: (i, 0), pipeline_mode=pl.Buffered(1))]
        in_specs += [pl.BlockSpec((None, k, tn), lambda i, j, c=c: (layer,) + c(i, j)) for c in cols]
        in_specs += [pl.BlockSpec((3, tn), c) for c in cols]
        in_specs += [pl.BlockSpec((1, tn), c) for c in cols]
        n_in = len(in_specs)
        vmem = (tm * k * 2 + 2 * nblk * (2 * k * tn * 2 + tm * tn * 2) + 16 * tm * tn * 4 + (4 << 20))
        return pl.pallas_call(
            functools.partial(_ffn_up_kernel, dims=dims, nblk=nblk, n_alias=len(carried)),
            grid=(m // tm, steps),
            in_specs=in_specs + [_ANY_SPEC] * len(carried),
            out_specs=pl.BlockSpec((tm, nblk * tn), lambda i, j: (i, first_block // nblk + j)),
            out_shape=jax.ShapeDtypeStruct((m, dff), BF16),
            input_output_aliases={n_in + i: i for i in range(len(carried))},
            compiler_params=_cparams(("arbitrary", "arbitrary"), vmem),
            name="ffn_up_conv_gate",
        )(h, *([w_up] * (2 * nblk)), *([conv_w] * (2 * nblk)), *([conv_b] * (2 * nblk)), *carried)

    nblk = min(FFN_BLOCKS_PER_STEP, nb)
    main_steps = nb // nblk
    act = run(0, main_steps, nblk, ())
    if main_steps * nblk < nb:
        act = run(main_steps * nblk, nb - main_steps * nblk, 1, (act,))
    return act


def _rope(x, cos, sin_signed):
    lane = lax.broadcasted_iota(jnp.int32, x.shape, 1)
    n = x.shape[1]
    swapped = jnp.where((lane & 1) == 0, pltpu.roll(x, n - 1, 1), pltpu.roll(x, 1, 1))
    return x * cos + swapped * sin_signed


def _softmax_parts(s, sc):
    m = jnp.max(s, axis=-1, keepdims=True)
    if sc is not None:
        m = jnp.maximum(m, jnp.max(sc, axis=-1, keepdims=True))
    e = jnp.exp(s - m)
    tot = jnp.sum(e, axis=-1, keepdims=True)
    ec = None
    if sc is not None:
        ec = jnp.exp(sc - m)
        tot = tot + jnp.sum(ec, axis=-1, keepdims=True)
    return e, ec, 1.0 / tot


def _attn_kernel(*refs, latent, heads, tq, lq, lam_init, n_alias):
    if latent:
        q_ref, k_ref, v_ref, kc_ref, vc_ref, cos_ref, sin_ref, lv_ref, g_ref = refs[:9]
        o_ref, kb_s, vb_s, kcb_s, vcb_s = refs[9 + n_alias:]
    else:
        q_ref, k_ref, v_ref, lv_ref, g_ref = refs[:5]
        o_ref, ko_ref, vo_ref, kb_s, vb_s = refs[5 + n_alias:]
        ko_ref[0, 0] = k_ref[...]
        vo_ref[0, 0] = v_ref[...]
    lv = lv_ref[...]
    lam = (jnp.exp(jnp.sum(lv[0:1] * lv[1:2], axis=-1, keepdims=True))
           - jnp.exp(jnp.sum(lv[2:3] * lv[3:4], axis=-1, keepdims=True)) + lam_init)
    gain = g_ref[...] * (1.0 - lam_init)
    scale = QK_DIM ** -0.5
    lo = lax.broadcasted_iota(jnp.int32, (tq, V_DIM), 1) < QK_DIM

    for h in range(heads):
        cols = slice(h * V_DIM, (h + 1) * V_DIM)
        k = k_ref[:, cols]
        if latent:
            k = _rope(k, cos_ref[...], sin_ref[...])
            kcb_s[...] = kc_ref[0, 0, :, cols].astype(BF16)
            vcb_s[...] = vc_ref[0, 0, :, cols].astype(BF16)
        kb_s[...] = k.astype(BF16)
        vb_s[...] = v_ref[:, cols].astype(BF16)

        def q_block(i, carry, cols=cols):
            rows = pl.ds(0, tq) if lq == tq else pl.ds(pl.multiple_of(i * tq, tq), tq)
            q = q_ref[rows, cols]
            if latent:
                q = _rope(q, cos_ref[rows, :], sin_ref[rows, :])
            q = q * scale
            q1 = jnp.where(lo, q, 0.0).astype(BF16)
            q2 = jnp.where(lo, 0.0, q).astype(BF16)
            kb = kb_s[...]
            s1 = _dot_nt(q1, kb)
            s2 = _dot_nt(q2, kb)
            s1c = s2c = None
            if latent:
                kcb = kcb_s[...]
                s1c = _dot_nt(q1, kcb)
                s2c = _dot_nt(q2, kcb)
            e1, e1c, r1 = _softmax_parts(s1, s1c)
            e2, e2c, r2 = _softmax_parts(s2, s2c)
            r2 = r2 * lam
            o = _dot((e1 * r1 - e2 * r2).astype(BF16), vb_s[...])
            if latent:
                o = o + _dot((e1c * r1 - e2c * r2).astype(BF16), vcb_s[...])
            rr = lax.rsqrt(jnp.mean(o * o, axis=-1, keepdims=True) + NORM_EPS)
            o_ref[rows, cols] = (o * rr * gain).astype(o_ref.dtype)
            return carry

        if lq == tq:
            q_block(0, 0)
        else:
            lax.fori_loop(0, lq // tq, q_block, 0, unroll=2)


def _attn_call(p, lv, g, dims, lam_init, latent, layer, carried, cache_k=None, cache_v=None,
               cos=None, sin=None):
    n_heads = dims.att_heads
    if latent:
        nb, lq, heads, row_off = dims.b_lat, dims.l_lat, 1, dims.n_ctx // dims.l_lat
    else:
        nb, lq, heads, row_off = dims.b_ctx, dims.l_ctx, _pick(n_heads, (4, 2, 1)), 0
    tq = min(lq, 256)
    hb = n_heads // heads
    w = heads * V_DIM
    q_spec = pl.BlockSpec((lq, w), lambda b, h: (b + row_off, h))
    k_spec = pl.BlockSpec((lq, w), lambda b, h: (b + row_off, hb + h))
    v_spec = pl.BlockSpec((lq, w), lambda b, h: (b + row_off, 2 * hb + h))
    small = [pl.BlockSpec((4, QK_DIM), lambda b, h: (0, 0)),
             pl.BlockSpec((1, V_DIM), lambda b, h: (0, 0))]
    scratch = [pltpu.VMEM((lq, V_DIM), BF16), pltpu.VMEM((lq, V_DIM), BF16)]
    if latent:
        past = cache_k.shape[2]
        c_spec = pl.BlockSpec((1, 1, past, w), lambda b, h: (b, layer, 0, h))
        t_spec = pl.BlockSpec((lq, V_DIM), lambda b, h: (0, 0))
        in_specs = [q_spec, k_spec, v_spec, c_spec, c_spec, t_spec, t_spec] + small
        args = (p, p, p, cache_k, cache_v, cos, sin, lv, g.reshape(1, V_DIM))
        scratch += [pltpu.VMEM((past, V_DIM), BF16), pltpu.VMEM((past, V_DIM), BF16)]
        lk = lq + past
    else:
        in_specs = [q_spec, k_spec, v_spec] + small
        args = (p, p, p, lv, g.reshape(1, V_DIM))
        lk = lq
    n_in = len(args)
    att_shape = jax.ShapeDtypeStruct((dims.n_ctx + dims.n_lat, n_heads * V_DIM), BF16)
    att_spec = pl.BlockSpec((lq, w), lambda b, h: (b + row_off, h))
    if latent:
        out_shape, out_specs = att_shape, att_spec
        aliases = {n_in: 0}
    else:
        kv_shape = jax.ShapeDtypeStruct((nb, dims.depth, lq, n_heads * V_DIM), F32)
        kv_spec = pl.BlockSpec((1, 1, lq, w), lambda b, h: (b, layer, 0, h))
        out_shape, out_specs = (att_shape, kv_shape, kv_shape), (att_spec, kv_spec, kv_spec)
        aliases = {n_in + i: 1 + i for i in range(len(carried))}
    vmem = 2 * 6 * lq * w * 4 + 12 * tq * lk * 4 + (8 << 20)
    return pl.pallas_call(
        functools.partial(_attn_kernel, latent=latent, heads=heads, tq=tq, lq=lq,
                          lam_init=lam_init, n_alias=len(carried)),
        grid=(nb, hb),
        in_specs=in_specs + [_ANY_SPEC] * len(carried),
        out_specs=out_specs,
        out_shape=out_shape,
        input_output_aliases=aliases,
        scratch_shapes=scratch,
        compiler_params=_cparams(("arbitrary", "arbitrary"), vmem),
        name="diff_attention_latent" if latent else "diff_attention_context",
    )(*args, *carried)


def _softplus(x):
    return jnp.maximum(x, 0.0) + jnp.log1p(jnp.exp(-jnp.abs(x)))


def _ssd_kernel(*refs, seq_len, latent, n_alias):
    it = iter(refs)
    x_ref, b_ref, c_ref, z_ref, dt_ref = (next(it) for _ in range(5))
    cwx_ref, cbx_ref, cwb_ref, cbb_ref, cwc_ref, cbc_ref = (next(it) for _ in range(6))
    dtb_ref, alog_ref, d_ref = (next(it) for _ in range(3))
    if latent:
        sf0_ref, sb0_ref = next(it), next(it)
    for _ in range(n_alias):
        next(it)
    if latent:
        y_ref = next(it)
    else:
        y_ref, sf_ref, sb_ref = next(it), next(it), next(it)
    (xs_s, b_s, c_s, bt_s, cs_s, rcs_s, dtxf_s, dtxb_s, csxf_s, rcsxb_s,
     yf_s, yb_s, stf_s, stb_s) = it

    ch = SSD_CHUNK
    nc = seq_len // ch

    def conv_silu(u_ref, w_ref, bias_ref):
        u = u_ref[...]
        t = lax.broadcasted_iota(jnp.int32, u.shape, 0)
        return _silu(_conv3_rows(u, w_ref, bias_ref, t == 0, t == seq_len - 1))

    xs_s[...] = conv_silu(x_ref, cwx_ref, cbx_ref)
    bm = conv_silu(b_ref, cwb_ref, cbb_ref)
    b_s[...] = bm.astype(BF16)
    for c in range(nc):
        bt_s[c] = bm[c * ch:(c + 1) * ch, :].T.astype(BF16)
    c_s[...] = conv_silu(c_ref, cwc_ref, cbc_ref).astype(BF16)

    dt = _softplus(dt_ref[...] + dtb_ref[...])
    dta = dt * (-jnp.exp(alog_ref[...]))
    ri = lax.broadcasted_iota(jnp.int32, (ch, ch), 0)
    ci = lax.broadcasted_iota(jnp.int32, (ch, ch), 1)
    lower = ri >= ci
    upper = ri <= ci
    t_low = jnp.where(lower, 1.0, 0.0).astype(BF16)
    t_up = jnp.where(upper, 1.0, 0.0).astype(BF16)
    for c in range(nc):
        blk = dta[c * ch:(c + 1) * ch, :]
        cs_s[c * ch:(c + 1) * ch, :] = _dot_f32_rhs(t_low, blk)
        rcs_s[c * ch:(c + 1) * ch, :] = _dot_f32_rhs(t_up, blk)
    er = lax.broadcasted_iota(jnp.int32, (LANE, SSD_GW), 0)
    eh = lax.broadcasted_iota(jnp.int32, (LANE, SSD_GW), 1) >> int(math.log2(SSD_P))
    exp_f = jnp.where(er == eh, 1.0, 0.0).astype(BF16)
    exp_b = jnp.where(er == eh + SSD_HPG, 1.0, 0.0).astype(BF16)
    dtxf_s[...] = _dot_f32_lhs(dt, exp_f, terms=2)
    dtxb_s[...] = _dot_f32_lhs(dt, exp_b, terms=2)
    csxf_s[...] = _dot_f32_lhs(cs_s[...], exp_f)
    rcsxb_s[...] = _dot_f32_lhs(rcs_s[...], exp_b)

    lo = lax.broadcasted_iota(jnp.int32, (ch, LANE), 1) < SSD_P

    def chunk_step(c, forward, y_buf, st_buf):
        rows = pl.ds(pl.multiple_of(c * ch, ch), ch)
        if forward:
            dtx, ex, cs, base, tri = dtxf_s[rows, :], csxf_s[rows, :], cs_s[rows, :], 0, lower
        else:
            dtx, ex, cs, base, tri = dtxb_s[rows, :], rcsxb_s[rows, :], rcs_s[rows, :], SSD_HPG, upper
        xdt = xs_s[rows, :] * dtx
        cc = c_s[rows, :]
        cb = _dot_nt(cc, b_s[rows, :])
        st = st_buf[...]
        y_off =_dot(cc, st.astype(BF16)) * jnp.exp(ex)
        cs_t = cs.T
        for j in range(SSD_GW // LANE):
            lanes = slice(j * LANE, (j + 1) * LANE)
            xp = xdt[:, lanes]
            acc = y_off[:, lanes]
            for k in range(LANE // SSD_P):
                hh = base + j * (LANE // SSD_P) + k
                diff = cs[:, hh:hh + 1] - cs_t[hh:hh + 1, :]
                decay = jnp.exp(jnp.where(tri, diff, -jnp.inf))
                g = (cb * decay).astype(BF16)
                xm = jnp.where(lo if k == 0 else jnp.logical_not(lo), xp, 0.0).astype(BF16)
                acc = acc + _dot(g, xm)
            y_buf[rows, lanes] = acc
        edge = ex[ch - 1:ch, :] if forward else ex[0:1, :]
        xd = (xdt * jnp.exp(edge - ex)).astype(BF16)
        st_buf[...] = jnp.exp(edge) * st + _dot(bt_s[c], xd)

    def both_directions(i, carry):
        chunk_step(i, True, yf_s, stf_s)
        chunk_step(nc - 1 - i, False, yb_s, stb_s)
        return carry

    stf_s[...] = sf0_ref[0, 0].T if latent else jnp.zeros(stf_s.shape, F32)
    stb_s[...] = sb0_ref[0, 0].T if latent else jnp.zeros(stb_s.shape, F32)
    lax.fori_loop(0, nc, both_directions, 0)
    if not latent:
        sf_ref[0, 0] = stf_s[...].T
        sb_ref[0, 0] = stb_s[...].T

    y = yf_s[...] + yb_s[...] + d_ref[...] * xs_s[...]
    y_ref[...] = y * _silu(z_ref[...])


def _ssd_call(p, lw, dims, latent, layer, carried, init_f=None, init_b=None):
    groups = dims.ssd_groups
    gw = SSD_GW
    if latent:
        nb, sl, row_off = dims.b_lat, dims.l_lat, dims.n_ctx // dims.l_lat
    else:
        nb, sl, row_off = dims.b_ctx, dims.l_ctx, 0
    nc = sl // SSD_CHUNK
    o = dims.offs
    zb, xb, bb, cb, db = o.z // gw, o.x // gw, o.b // LANE, o.c // LANE, o.dt // LANE
    in_specs = [
        pl.BlockSpec((sl, gw), lambda b, g: (b + row_off, xb + g)),
        pl.BlockSpec((sl, LANE), lambda b, g: (b + row_off, bb + g)),
        pl.BlockSpec((sl, LANE), lambda b, g: (b + row_off, cb + g)),
        pl.BlockSpec((sl, gw), lambda b, g: (b + row_off, zb + g)),
        pl.BlockSpec((sl, LANE), lambda b, g: (b + row_off, db + g)),
        pl.BlockSpec((3, gw), lambda b, g: (0, g)),
        pl.BlockSpec((1, gw), lambda b, g: (0, g)),
        pl.BlockSpec((3, LANE), lambda b, g: (0, dims.ssd_width // LANE + g)),
        pl.BlockSpec((1, LANE), lambda b, g: (0, dims.ssd_width // LANE + g)),
        pl.BlockSpec((3, LANE), lambda b, g: (0, dims.ssd_width // LANE + groups + g)),
        pl.BlockSpec((1, LANE), lambda b, g: (0, dims.ssd_width // LANE + groups + g)),
        pl.BlockSpec((1, LANE), lambda b, g: (0, g)),
        pl.BlockSpec((1, LANE), lambda b, g: (0, g)),
        pl.BlockSpec((1, gw), lambda b, g: (0, g)),
    ]
    args = [p, p, p, p, p, lw["ssd_conv_w"], lw["ssd_conv_b"], lw["ssd_conv_w"], lw["ssd_conv_b"],
            lw["ssd_conv_w"], lw["ssd_conv_b"], lw["ssd_dt_bias"], lw["ssd_a_log"], lw["ssd_d"]]
    y_shape = jax.ShapeDtypeStruct((dims.n_ctx + dims.n_lat, dims.ssd_width), F32)
    y_spec = pl.BlockSpec((sl, gw), lambda b, g: (b + row_off, g))
    s_spec = pl.BlockSpec((1, 1, gw, SSD_N), lambda b, g: (b, layer, g, 0))
    if latent:
        in_specs += [s_spec, s_spec]
        args += [init_f, init_b]
        out_shape, out_specs = y_shape, y_spec
        aliases = {len(args): 0}
    else:
        s_shape = jax.ShapeDtypeStruct((nb, dims.depth, dims.ssd_width, SSD_N), F32)
        out_shape, out_specs = (y_shape, s_shape, s_shape), (y_spec, s_spec, s_spec)
        aliases = {len(args) + i: 1 + i for i in range(len(carried))}
    scratch = [
        pltpu.VMEM((sl, gw), F32), pltpu.VMEM((sl, LANE), BF16), pltpu.VMEM((sl, LANE), BF16),
        pltpu.VMEM((nc, SSD_CHUNK, SSD_CHUNK), BF16),
        pltpu.VMEM((sl, LANE), F32), pltpu.VMEM((sl, LANE), F32),
        pltpu.VMEM((sl, gw), F32), pltpu.VMEM((sl, gw), F32),
        pltpu.VMEM((sl, gw), F32), pltpu.VMEM((sl, gw), F32),
        pltpu.VMEM((sl, gw), F32), pltpu.VMEM((sl, gw), F32),
        pltpu.VMEM((SSD_N, gw), F32), pltpu.VMEM((SSD_N, gw), F32),
    ]
    vmem = 2 * sl * (3 * gw + 3 * LANE) * 4 + sl * (8 * gw + 4 * LANE) * 4 + 8 * sl * gw * 4 + (8 << 20)
    return pl.pallas_call(
        functools.partial(_ssd_kernel, seq_len=sl, latent=latent, n_alias=len(carried)),
        grid=(nb, groups),
        in_specs=in_specs + [_ANY_SPEC] * len(carried),
        out_specs=out_specs,
        out_shape=out_shape,
        input_output_aliases=aliases,
        scratch_shapes=scratch,
        compiler_params=_cparams(("arbitrary", "arbitrary"), vmem),
        name="ssd_latent" if latent else "ssd_context",
    )(*args, *carried)


def _dft_mats(l):
    n = 3 * l // 2
    nh = n // 2
    k = np.arange(nh, dtype=np.int64)[:, None]
    t = np.arange(l, dtype=np.int64)[None, :]
    ang = 2.0 * np.pi * ((k * t) % n).astype(np.float64) / n
    fwd = np.concatenate([np.cos(ang), -np.sin(ang)], axis=0)
    fwd[nh, :] = np.where(np.arange(l) % 2 == 0, 1.0, -1.0)
    pos = (np.arange(l, dtype=np.int64) + l // 2)[:, None]
    kk = np.arange(nh, dtype=np.int64)[None, :]
    ang2 = 2.0 * np.pi * ((pos * kk) % n).astype(np.float64) / n
    inv = np.concatenate([2.0 * np.cos(ang2), -2.0 * np.sin(ang2)], axis=1) / n
    inv[:, 0] = 1.0 / n
    inv[:, nh] = np.where(pos[:, 0] % 2 == 0, 1.0, -1.0) / n

    def split(m):
        hi = m.astype(np.float32).astype(BF16)
        lo = (m - hi.astype(np.float64)).astype(np.float32).astype(BF16)
        return hi, lo

    return split(fwd) + split(inv)


def _hy_feats(l):
    off = jnp.arange(l, dtype=F32) - (l // 2)
    band = HY_FREQ_BASE ** (-jnp.arange(HY_BANDS, dtype=F32) / HY_BANDS)
    ang = off[:, None] * band
    feats = jnp.concatenate([off[:, None] / l, jnp.sin(ang), jnp.cos(ang)], axis=-1)
    return jnp.pad(feats, ((0, 0), (0, LANE - feats.shape[1])))


def _hy_filter_kernel(feats_ref, w1_ref, b1_ref, w2_ref, b2_ref, fr_ref, w3_ref, dl_ref,
                      fh_ref, fl_ref, o_ref, *, seq_len):
    fr = fr_ref[0]
    h1 = jnp.sin(fr[0:1] * (_dot_hp(feats_ref[...], w1_ref[0]) + b1_ref[0]))
    h2 = jnp.sin(fr[1:2] * (_dot_hp(h1, w2_ref[0]) + b2_ref[0]))
    filt = _dot_hp(h2, w3_ref[0])
    off = lax.broadcasted_iota(jnp.int32, filt.shape, 0).astype(F32) - (seq_len // 2)
    dist = jnp.abs(off) * (2.0 / seq_len)
    h = filt * jnp.exp(-dist * dl_ref[...])
    hh, hl = _split_bf16(h, 2)
    fh = fh_ref[...]
    o_ref[0, 0] = _dot(fh, hh) + _dot(fh, hl) + _dot(fl_ref[...], hh)


def _hy_filter_call(seq_len, hw, mats, dims):
    depth, hy = dims.depth, dims.hy_width
    cb = HY_CB
    ncb = hy // cb
    fh, fl = mats[0], mats[1]
    n = fh.shape[0]
    feats = _hy_feats(seq_len)
    deltas = jnp.linspace(HY_MIN_DECAY, HY_MAX_DECAY, hy, dtype=F32).reshape(1, hy)
    vmem = 2 * 2 * (n * seq_len * 2) + 6 * n * cb * 4 + (8 << 20)
    return pl.pallas_call(
        functools.partial(_hy_filter_kernel, seq_len=seq_len),
        grid=(depth, 2, ncb),
        in_specs=[
            pl.BlockSpec((seq_len, LANE), lambda l, o, c: (0, 0)),
            pl.BlockSpec((1, LANE, LANE), lambda l, o, c: (l, 0, 0)),
            pl.BlockSpec((1, 1, LANE), lambda l, o, c: (l, 0, 0)),
            pl.BlockSpec((1, LANE, LANE), lambda l, o, c: (l, 0, 0)),
            pl.BlockSpec((1, 1, LANE), lambda l, o, c: (l, 0, 0)),
            pl.BlockSpec((1, 2, LANE), lambda l, o, c: (l, 0, 0)),
            pl.BlockSpec((1, LANE, cb), lambda l, o, c: (l, 0, o * ncb + c)),
            pl.BlockSpec((1, cb), lambda l, o, c: (0, c)),
            pl.BlockSpec((n, seq_len), lambda l, o, c: (0, 0)),
            pl.BlockSpec((n, seq_len), lambda l, o, c: (0, 0)),
        ],
        out_specs=pl.BlockSpec((1, 1, n, cb), lambda l, o, c: (l, o, 0, c)),
        out_shape=jax.ShapeDtypeStruct((depth, 2, n, hy), F32),
        compiler_params=_cparams(("arbitrary",) * 3, vmem),
        name="hyena_filter_spectrum",
    )(feats, hw["w1"], hw["b1"], hw["w2"], hw["b2"], hw["freq"], hw["w3"], deltas, fh, fl)


def _hy_conv_kernel(*refs, seq_len, n_seq, n_alias):
    (x1_ref, x2_ref, v_ref, cw1_ref, cb1_ref, cw2_ref, cb2_ref, cwv_ref, cbv_ref,
     hf_ref, bias_ref, fh_ref, fl_ref, gh_ref, gl_ref) = refs[:15]
    o_ref = refs[15 + n_alias]
    nh = fh_ref.shape[0] // 2
    cb = o_ref.shape[1]
    t = lax.broadcasted_iota(jnp.int32, (seq_len, cb), 0)
    first = t == 0
    last = t == seq_len - 1
    bin0 = lax.broadcasted_iota(jnp.int32, (nh, cb), 0) == 0
    fh = fh_ref[...]
    gh = gh_ref[...]
    rows = [slice(s * seq_len, (s + 1) * seq_len) for s in range(n_seq)]
    gates = [(_conv3_rows(x1_ref[r, :], cw1_ref, cb1_ref, first, last),
              _conv3_rows(x2_ref[r, :], cw2_ref, cb2_ref, first, last)) for r in rows]
    zz = [_conv3_rows(v_ref[r, :], cwv_ref, cbv_ref, first, last) for r in rows]
    for o in range(2):
        hr, hi = hf_ref[0, o, :nh, :], hf_ref[0, o, nh:, :]
        uf = []
        for z in zz:
            zh, zl = _split_bf16(z, 2)
            uf.append(_dot(fh, zh) + _dot(fh, zl) + _dot(fl_ref[...], zh))
        ys = []
        for u in uf:
            ur, ui = u[:nh], u[nh:]
            ii = ui * hi
            yr = ur * hr - jnp.where(bin0, 0.0, ii)
            yi = jnp.where(bin0, ii, ur * hi + ui * hr)
            yh, yl = _split_bf16(jnp.concatenate([yr, yi], axis=0), 2)
            ys.append(_dot(gh, yh) + _dot(gh, yl) + _dot(gl_ref[...], yh))
        zz = [g[o] * (y + z * bias_ref[o:o + 1, :]) for g, y, z in zip(gates, ys, zz)]
    for r, z in zip(rows, zz):
        o_ref[r, :] = z


def _hy_conv_call(p, hf, lw, mats, dims, layer, latent, carried):
    hy = dims.hy_width
    cb = HY_CB
    ncb = hy // cb
    if latent:
        nb, sl, row_start = dims.b_lat, dims.l_lat, dims.n_ctx
    else:
        nb, sl, row_start = dims.b_ctx, dims.l_ctx, 0
    n_seq = max(n for n in (8, 4, 2, 1) if nb % n == 0 and (n == 1 or n * sl <= HY_ROWS_PER_STEP))
    rows = n_seq * sl
    row_off = row_start // rows
    n = mats[0].shape[0]
    hb = dims.offs.hy // cb
    const = lambda c, b: (0, 0)
    in_specs = [pl.BlockSpec((rows, cb), lambda c, b, k=k: (b + row_off, hb + k * ncb + c)) for k in range(3)]
    for k in range(3):
        in_specs += [pl.BlockSpec((3, cb), lambda c, b, k=k: (0, k * ncb + c)),
                     pl.BlockSpec((1, cb), lambda c, b, k=k: (0, k * ncb + c))]
    once = pl.Buffered(1)
    in_specs += [
        pl.BlockSpec((1, 2, n, cb), lambda c, b: (layer, 0, 0, c), pipeline_mode=once),
        pl.BlockSpec((2, cb), lambda c, b: (0, c)),
        pl.BlockSpec((n, sl), const, pipeline_mode=once),
        pl.BlockSpec((n, sl), const, pipeline_mode=once),
        pl.BlockSpec((sl, n), const, pipeline_mode=once),
        pl.BlockSpec((sl, n), const, pipeline_mode=once),
    ]
    cw, cbias = lw["hy_conv_w"], lw["hy_conv_b"]
    vmem = (4 * (n * sl * 2) + 2 * n * cb * 4 + 2 * 4 * rows * cb * 4
            + n_seq * (6 * n + 8 * sl) * cb * 4 + (8 << 20))
    return pl.pallas_call(
        functools.partial(_hy_conv_kernel, seq_len=sl, n_seq=n_seq, n_alias=len(carried)),
        grid=(ncb, nb // n_seq),
        in_specs=in_specs + [_ANY_SPEC] * len(carried),
        out_specs=pl.BlockSpec((rows, cb), lambda c, b: (b + row_off, c)),
        out_shape=jax.ShapeDtypeStruct((dims.n_ctx + dims.n_lat, hy), F32),
        input_output_aliases={15 + i: i for i in range(len(carried))},
        compiler_params=_cparams(("arbitrary", "arbitrary"), vmem),
        name="hyena_conv_latent" if latent else "hyena_conv_context",
    )(p, p, p, cw, cbias, cw, cbias, cw, cbias, hf, lw["hy_bias"], *mats, *carried)


class _Offsets:
    def __init__(self, att_w, ssd_w, groups, hy_w):
        self.q, self.k, self.v = 0, att_w, 2 * att_w
        self.z = 3 * att_w
        self.x = self.z + ssd_w
        self.b = self.x + ssd_w
        self.c = self.b + groups * SSD_N
        self.hy = self.c + groups * SSD_N
        self.dt = self.hy + 3 * hy_w
        self.total = self.dt + groups * LANE


class _Dims:
    pass


def _make_dims(x_prompt, x_sample, cache_k, state_ssd_fwd, w_in, hy_norm_g, ffn_w_down):
    d = _Dims()
    d.b_ctx, d.l_ctx, d.d_model = x_prompt.shape
    d.b_lat, d.l_lat, _ = x_sample.shape
    d.n_ctx = d.b_ctx * d.l_ctx
    d.n_lat = d.b_lat * d.l_lat
    d.depth = w_in.shape[0]
    d.att_heads = cache_k.shape[3]
    d.att_width = d.att_heads * V_DIM
    d.ssd_heads = state_ssd_fwd.shape[2]
    d.ssd_width = d.ssd_heads * SSD_P
    d.ssd_groups = d.ssd_heads // SSD_HPG
    d.hy_width = hy_norm_g.shape[-1]
    d.d_ff = ffn_w_down.shape[1]
    d.offs = _Offsets(d.att_width, d.ssd_width, d.ssd_groups, d.hy_width)
    d.tm = d.l_lat
    assert d.l_lat % d.l_ctx == 0 and d.n_ctx % d.tm == 0 and d.l_ctx % 256 == 0
    assert d.l_lat & (d.l_lat - 1) == 0 and d.l_ctx & (d.l_ctx - 1) == 0
    assert d.offs.hy % HY_CB == 0 and d.hy_width % HY_CB == 0 and d.d_ff % FFN_TN == 0
    assert d.ssd_heads % SSD_HPG == 0
    return d


def _dt_gather(dims):
    src = np.zeros((dims.ssd_groups * LANE,), np.int32)
    valid = np.zeros((dims.ssd_groups * LANE,), np.float32)
    for g in range(dims.ssd_groups):
        for direction in range(2):
            for hh in range(SSD_HPG):
                lane = g * LANE + direction * SSD_HPG + hh
                src[lane] = direction * dims.ssd_heads + g * SSD_HPG + hh
                valid[lane] = 1.0
    return src, valid


def _rope_tables(seq_len):
    rows = seq_len // GRID_W
    r, col = jnp.meshgrid(jnp.arange(rows), jnp.arange(GRID_W), indexing="ij")
    npair = QK_DIM // 4
    inv = ROPE_BASE ** (-jnp.arange(npair, dtype=F32) / npair)
    ang = jnp.concatenate([r.reshape(-1, 1).astype(F32) * inv, col.reshape(-1, 1).astype(F32) * inv], axis=-1)
    cos, sin = jnp.cos(ang), jnp.sin(ang)
    cos2 = jnp.repeat(cos, 2, axis=-1)
    sin2 = jnp.stack([-sin, sin], axis=-1).reshape(seq_len, QK_DIM)
    return jnp.tile(cos2, (1, 2)), jnp.tile(sin2, (1, 2))


def kernel(x_prompt, x_sample, cache_k, cache_v, state_ssd_fwd, state_ssd_bwd, c, c_ctx, norm1_g, norm2_g, w_ada, b_ada, w_in, att_lambda, att_subln_g, ssd_conv_w, ssd_conv_b, ssd_dt_bias, ssd_a_log, ssd_d, ssd_norm_g, hy_conv_w, hy_conv_b, hy_w1, hy_b1, hy_w2, hy_b2, hy_freq, hy_w3, hy_bias, hy_norm_g, w_out, ffn_w_up, ffn_conv_w, ffn_conv_b, ffn_w_down, final_norm_g):
    dims = _make_dims(x_prompt, x_sample, cache_k, state_ssd_fwd, w_in, hy_norm_g, ffn_w_down)
    d, depth, offs = dims.d_model, dims.depth, dims.offs
    n_ctx = dims.n_ctx

    mod_rows = 16 * ((1 + dims.b_lat + 15) // 16)
    cond = jnp.zeros((mod_rows, d), F32).at[0].set(c_ctx).at[1:1 + dims.b_lat].set(c)
    mod_all = _ada_call(cond, w_ada, b_ada).reshape(depth, mod_rows, 6, 1, d)

    pad_h = LANE - hy_w1.shape[-1]
    hw = {
        "w1": jnp.pad(hy_w1, ((0, 0), (0, LANE - hy_w1.shape[1]), (0, pad_h))),
        "b1": jnp.pad(hy_b1, ((0, 0), (0, pad_h)))[:, None, :],
        "w2": jnp.pad(hy_w2, ((0, 0), (0, pad_h), (0, pad_h))),
        "b2": jnp.pad(hy_b2, ((0, 0), (0, pad_h)))[:, None, :],
        "freq": jnp.pad(hy_freq, ((0, 0), (0, 0), (0, pad_h))),
        "w3": jnp.pad(hy_w3, ((0, 0), (0, pad_h), (0, 0))),
    }
    mats_ctx = _dft_mats(dims.l_ctx)
    mats_lat = _dft_mats(dims.l_lat)
    hf_ctx = _hy_filter_call(dims.l_ctx, hw, mats_ctx, dims)
    hf_lat = _hy_filter_call(dims.l_lat, hw, mats_lat, dims)

    cos_t, sin_t = _rope_tables(dims.l_lat)
    dt_src, dt_valid = _dt_gather(dims)
    ck = cache_k.reshape(dims.b_lat, depth, cache_k.shape[2], dims.att_width)
    cv = cache_v.reshape(dims.b_lat, depth, cache_v.shape[2], dims.att_width)
    sf0 = state_ssd_fwd.reshape(dims.b_lat, depth, dims.ssd_width, SSD_N)
    sb0 = state_ssd_bwd.reshape(dims.b_lat, depth, dims.ssd_width, SSD_N)

    x = jnp.concatenate([x_prompt.reshape(n_ctx, d), x_sample.reshape(dims.n_lat, d)], axis=0)
    new_kv, new_st = (), ()

    dt0, n_h = offs.hy, dims.ssd_heads
    pieces = [w_in[:, :, :dt0], w_in[:, :, dt0 + 2 * n_h:]]
    for g in range(dims.ssd_groups):
        f0 = dt0 + g * SSD_HPG
        pieces += [w_in[:, :, f0:f0 + SSD_HPG], w_in[:, :, f0 + n_h:f0 + n_h + SSD_HPG],
                   jnp.zeros((depth, d, LANE - 2 * SSD_HPG), F32)]
    w_proj = jnp.concatenate(pieces, axis=2).astype(BF16)
    w_o = w_out.astype(BF16)
    w_up = ffn_w_up.astype(BF16)
    w_down = ffn_w_down.astype(BF16)
    a_w, s_w = dims.att_width, dims.ssd_width
    if a_w % s_w == 0 and (a_w + s_w) % dims.hy_width == 0:
        out_terms = lambda l, att, y_ssd, y_hy: [
            (att, w_o, l, 0), (y_ssd, w_o, l, a_w // s_w), (y_hy, w_o, l, (a_w + s_w) // dims.hy_width)]
    else:
        out_terms = lambda l, att, y_ssd, y_hy: [
            (att, w_o[l:l + 1, :a_w], 0, 0), (y_ssd, w_o[l:l + 1, a_w:a_w + s_w], 0, 0),
            (y_hy, w_o[l:l + 1, a_w + s_w:], 0, 0)]

    for l in range(depth):
        lam_init = 0.8 - 0.6 * math.exp(-0.3 * l)
        mod = mod_all[l]
        dtb = jnp.take(ssd_dt_bias[l].reshape(-1), dt_src) * dt_valid
        alog = jnp.take(ssd_a_log[l].reshape(-1), dt_src) * dt_valid
        lw = {
            "ssd_conv_w": ssd_conv_w[l], "ssd_conv_b": ssd_conv_b[l].reshape(1, -1),
            "ssd_dt_bias": dtb.reshape(1, -1), "ssd_a_log": alog.reshape(1, -1),
            "ssd_d": jnp.repeat(ssd_d[l], SSD_P).reshape(1, -1),
            "hy_conv_w": hy_conv_w[l], "hy_conv_b": hy_conv_b[l].reshape(1, -1), "hy_bias": hy_bias[l],
        }

        h = _norm_mod_call(x, norm1_g[l], mod, 0, 1, dims)
        p = _mm_call(h, w_proj, l, dims)

        att, new_k, new_v = _attn_call(p, att_lambda[l], att_subln_g[l], dims, lam_init, False, l, new_kv)
        new_kv = (new_k, new_v)
        att = _attn_call(p, att_lambda[l], att_subln_g[l], dims, lam_init, True, l, (att,),
                         cache_k=ck, cache_v=cv, cos=cos_t, sin=sin_t)
        ys, new_sf, new_sb = _ssd_call(p, lw, dims, False, l, new_st)
        new_st = (new_sf, new_sb)
        ys = _ssd_call(p, lw, dims, True, l, (ys,), init_f=sf0, init_b=sb0)
        zz = _hy_conv_call(p, hf_ctx, lw, mats_ctx, dims, l, False, ())
        zz = _hy_conv_call(p, hf_lat, lw, mats_lat, dims, l, True, (zz,))

        y_ssd = _norm_call(ys, ssd_norm_g[l], BF16, name="ssd_norm")
        y_hy = _norm_call(zz, hy_norm_g[l], BF16, name="hyena_norm")
        x = _mm_res_call(out_terms(l, att, y_ssd, y_hy), x, mod, 2, dims,
                         (1024, 768, 512, 384, 256, 128))

        h2 = _norm_mod_call(x, norm2_g[l], mod, 3, 4, dims)
        act = _ffn_up_call(h2, w_up, l, ffn_conv_w[l], ffn_conv_b[l], dims)
        x = _mm_res_call([(act, w_down, l, 0)], x, mod, 5, dims, (256, 128), single_buffer_a=True)

    y_prompt = _norm_call(x, final_norm_g, F32, 0, n_ctx, name="final_norm").reshape(x_prompt.shape)
    y_sample = _norm_call(x, final_norm_g, F32, n_ctx, dims.n_lat, name="final_norm").reshape(x_sample.shape)
    kv_shape = (dims.b_ctx, depth, dims.l_ctx, dims.att_heads, V_DIM)
    st_shape = (dims.b_ctx, depth, dims.ssd_heads, SSD_P, SSD_N)
    return (y_prompt, y_sample, new_kv[0].reshape(kv_shape), new_kv[1].reshape(kv_shape),
            new_st[0].reshape(st_shape), new_st[1].reshape(st_shape))
```

```python
import functools
import math

import numpy as np
import jax
import jax.numpy as jnp
from jax import lax
from jax.experimental import pallas as pl
from jax.experimental.pallas import tpu as pltpu

F32 = jnp.float32
BF16 = jnp.bfloat16

LANE = 128
V7X_VMEM_BYTES = 64 * 1024 * 1024
VMEM_CAP = V7X_VMEM_BYTES - 6 * 1024 * 1024

NORM_EPS = 1e-6
GRID_W = 64
ROPE_BASE = 10000.0
QK_DIM = 64
V_DIM = 128
SSD_P = 64
SSD_N = 128
SSD_CHUNK = 128
SSD_HPG = 6
SSD_GW = SSD_HPG * SSD_P
HY_BANDS = 16
HY_FREQ_BASE = 10000.0
HY_MIN_DECAY = -math.log(1e-2) / 1.5
HY_MAX_DECAY = -math.log(1e-2) / 0.3
HY_CB = 256
FFN_TN = 256
FFN_BLOCKS_PER_STEP = 3
PROJ_COL_ALIGN = 1024
ATT_TQ = 1024
DOWN_TILE = 512
HY_ROWS_PER_STEP = 2048

_ANY_SPEC = pl.BlockSpec(memory_space=pl.ANY)


def _cparams(sem, vmem_bytes):
    limit = int(min(VMEM_CAP, max(vmem_bytes, 16 * 1024 * 1024)))
    return pltpu.CompilerParams(dimension_semantics=sem, vmem_limit_bytes=limit)


def _pick(n, candidates):
    for c in candidates:
        if n % c == 0:
            return c
    raise ValueError(f"no block size in {candidates} divides {n}")


def _dot(a, b):
    return jnp.dot(a, b, preferred_element_type=F32)


def _dot_nt(a, b):
    return lax.dot_general(a, b, (((1,), (1,)), ((), ())), preferred_element_type=F32)


def _split_bf16(x, n):
    parts = []
    r = x
    for _ in range(n):
        h = r.astype(BF16)
        parts.append(h)
        r = r - h.astype(F32)
    return parts


def _dot_f32_lhs(a, b_bf16, terms=3):
    return sum(_dot(p, b_bf16) for p in _split_bf16(a, terms))


def _dot_f32_rhs(a_bf16, b):
    return sum(_dot(a_bf16, p) for p in _split_bf16(b, 3))


def _dot_hp(a, b):
    ah, al = _split_bf16(a, 2)
    bh, bl = _split_bf16(b, 2)
    return _dot(ah, bh) + _dot(ah, bl) + _dot(al, bh)


def _silu(x):
    return x * jax.nn.sigmoid(x)


def _conv3_rows(u, w_ref, b_ref, first, last):
    n = u.shape[0]
    prev = jnp.where(first, 0.0, pltpu.roll(u, 1, 0))
    nxt = jnp.where(last, 0.0, pltpu.roll(u, n - 1, 0))
    return prev * w_ref[0:1, :] + b_ref[...] + u * w_ref[1:2, :] + nxt * w_ref[2:3, :]


def _ada_kernel(c_ref, w_ref, b_ref, o_ref):
    s = _silu(c_ref[...]).astype(BF16)
    o_ref[0] = _dot(s, w_ref[0].astype(BF16)) + b_ref[0]


def _ada_call(cond, w_ada, b_ada):
    depth, d, n = w_ada.shape
    rows = cond.shape[0]
    tn = _pick(n, (512, 256, 128))
    vmem = 2 * (d * tn * 4) + d * tn * 2 + 4 * rows * (d + 2 * tn) * 4 + (4 << 20)
    return pl.pallas_call(
        _ada_kernel,
        grid=(depth, n // tn),
        in_specs=[
            pl.BlockSpec((rows, d), lambda l, j: (0, 0)),
            pl.BlockSpec((1, d, tn), lambda l, j: (l, 0, j)),
            pl.BlockSpec((1, 1, tn), lambda l, j: (l, 0, j)),
        ],
        out_specs=pl.BlockSpec((1, rows, tn), lambda l, j: (l, 0, j)),
        out_shape=jax.ShapeDtypeStruct((depth, rows, n), F32),
        compiler_params=_cparams(("arbitrary", "arbitrary"), vmem),
        name="ada_mod",
    )(cond, w_ada, b_ada.reshape(depth, 1, n))


def _norm_mod_kernel(x_ref, g_ref, sc_ref, sh_ref, o_ref):
    x = x_ref[...]
    r = lax.rsqrt(jnp.mean(x * x, axis=-1, keepdims=True) + NORM_EPS)
    y = x * r * g_ref[...]
    o_ref[...] = (y * (1.0 + sc_ref[0, 0]) + sh_ref[0, 0]).astype(o_ref.dtype)


def _norm_kernel(x_ref, g_ref, o_ref):
    x = x_ref[...]
    r = lax.rsqrt(jnp.mean(x * x, axis=-1, keepdims=True) + NORM_EPS)
    o_ref[...] = (x * r * g_ref[...]).astype(o_ref.dtype)


def _mod_row(i, tm, dims):
    start = i * tm
    lat = jnp.maximum(start - dims.n_ctx, 0) // dims.l_lat
    return jnp.where(start >= dims.n_ctx, lat + 1, 0)


def _norm_mod_call(x, g, mod, sh_idx, sc_idx, dims):
    m, d = x.shape
    tr = 256
    row = functools.partial(_mod_row, tm=tr, dims=dims)
    vmem = 2 * tr * d * (4 + 2) + 3 * tr * d * 4 + (4 << 20)
    return pl.pallas_call(
        _norm_mod_kernel,
        grid=(m // tr,),
        in_specs=[
            pl.BlockSpec((tr, d), lambda i: (i, 0)),
            pl.BlockSpec((1, d), lambda i: (0, 0)),
            pl.BlockSpec((1, 1, 1, d), lambda i: (row(i), sc_idx, 0, 0)),
            pl.BlockSpec((1, 1, 1, d), lambda i: (row(i), sh_idx, 0, 0)),
        ],
        out_specs=pl.BlockSpec((tr, d), lambda i: (i, 0)),
        out_shape=jax.ShapeDtypeStruct((m, d), BF16),
        compiler_params=_cparams(("arbitrary",), vmem),
        name="norm_mod",
    )(x, g.reshape(1, d), mod, mod)


def _norm_call(x, g, out_dtype, row_start=0, rows=None, name="norm"):
    m, d = x.shape
    rows = m if rows is None else rows
    tr = 256
    off = row_start // tr
    vmem = 2 * tr * d * 8 + 3 * tr * d * 4 + (4 << 20)
    return pl.pallas_call(
        _norm_kernel,
        grid=(rows // tr,),
        in_specs=[
            pl.BlockSpec((tr, d), lambda i: (i + off, 0)),
            pl.BlockSpec((1, d), lambda i: (0, 0)),
        ],
        out_specs=pl.BlockSpec((tr, d), lambda i: (i, 0)),
        out_shape=jax.ShapeDtypeStruct((rows, d), out_dtype),
        compiler_params=_cparams(("arbitrary",), vmem),
        name=name,
    )(x, g.reshape(1, d))


def _mm_kernel(a_ref, b_ref, o_ref):
    o_ref[...] = _dot(a_ref[...], b_ref[...]).astype(o_ref.dtype)


def _mm_res_kernel(*refs, n_terms):
    x_ref, g_ref, o_ref = refs[2 * n_terms:]
    acc = _dot(refs[0][...], refs[1][...])
    for i in range(1, n_terms):
        acc = acc + _dot(refs[2 * i][...], refs[2 * i + 1][...])
    o_ref[...] = x_ref[...] + g_ref[0, 0] * acc


def _mm_call(a, w_stack, layer, dims, out_dtype=F32):
    m, k = a.shape
    n = w_stack.shape[2]
    b = w_stack
    tm = dims.tm
    tn = _pick(n, (1024, 768, 512, 384, 256, 128))
    vmem = 2 * (tm * k * 2 + k * tn * 2 + tm * tn * 4) + tm * tn * 4 + (4 << 20)
    return pl.pallas_call(
        _mm_kernel,
        grid=(m // tm, n // tn),
        in_specs=[
            pl.BlockSpec((tm, k), lambda i, j: (i, 0)),
            pl.BlockSpec((None, k, tn), lambda i, j: (layer, 0, j)),
        ],
        out_specs=pl.BlockSpec((tm, tn), lambda i, j: (i, j)),
        out_shape=jax.ShapeDtypeStruct((m, n), out_dtype),
        compiler_params=_cparams(("arbitrary", "arbitrary"), vmem),
        name="proj",
    )(a, b)


def _mm_res_call(terms, x, mod, g_idx, dims, tn_candidates, tm=None):
    m, n = x.shape
    k = sum(t[0].shape[1] for t in terms)
    tm = dims.tm if tm is None else tm
    tn = _pick(n, tn_candidates)
    row = functools.partial(_mod_row, tm=tm, dims=dims)
    vmem = 2 * (tm * k * 2 + k * tn * 2 + 2 * tm * tn * 4) + tm * tn * 4 + (4 << 20)
    in_specs, args = [], []
    for a, w_stack, layer, row_block in terms:
        ka = a.shape[1]
        in_specs.append(pl.BlockSpec((tm, ka), lambda i, j: (i, 0)))
        in_specs.append(pl.BlockSpec((None, ka, tn), lambda i, j, layer=layer, rb=row_block: (layer, rb, j)))
        args += [a, w_stack]
    in_specs += [pl.BlockSpec((tm, tn), lambda i, j: (i, j)),
                 pl.BlockSpec((1, 1, 1, tn), lambda i, j: (row(i), g_idx, 0, j))]
    return pl.pallas_call(
        functools.partial(_mm_res_kernel, n_terms=len(terms)),
        grid=(m // tm, n // tn),
        in_specs=in_specs,
        out_specs=pl.BlockSpec((tm, tn), lambda i, j: (i, j)),
        out_shape=jax.ShapeDtypeStruct((m, n), F32),
        compiler_params=_cparams(("arbitrary", "arbitrary"), vmem),
        name="proj_residual",
    )(*args, x, mod)


def _ffn_up_kernel(*refs, dims, nblk, n_alias):
    a_ref = refs[0]
    b_refs = refs[1:1 + 2 * nblk]
    cw_refs = refs[1 + 2 * nblk:1 + 4 * nblk]
    cb_refs = refs[1 + 4 * nblk:1 + 6 * nblk]
    o_ref = refs[1 + 6 * nblk + n_alias]
    a = a_ref[...]
    tm = a.shape[0]
    tn = b_refs[0].shape[1]
    seq = jnp.where(pl.program_id(0) * tm < dims.n_ctx, dims.l_ctx, dims.l_lat)
    t = lax.broadcasted_iota(jnp.int32, (tm, tn), 0) & (seq - 1)
    first = t == 0
    last = t == seq - 1
    for q in range(nblk):
        g = _conv3_rows(_dot(a, b_refs[2 * q][...]), cw_refs[2 * q], cb_refs[2 * q], first, last)
        v = _conv3_rows(_dot(a, b_refs[2 * q + 1][...]), cw_refs[2 * q + 1], cb_refs[2 * q + 1],
                        first, last)
        o_ref[:, q * tn:(q + 1) * tn] = (_silu(g) * v).astype(o_ref.dtype)


def _ffn_up_call(h, w_up, layer, conv_w, conv_b, dims):
    m, k = h.shape
    dff = w_up.shape[2] // 2
    tm = dims.tm
    tn = FFN_TN
    nb = dff // tn
    conv_b = conv_b.reshape(1, -1)

    def run(first_block, steps, nblk, carried):
        cols = []
        for q in range(nblk):
            cols += [lambda i, j, q=q: (0, first_block + nblk * j + q),
                     lambda i, j, q=q: (0, nb + first_block + nblk * j + q)]
        in_specs = [pl.BlockSpec((tm, k), lambda i, j: (i, 0), pipeline_mode=pl.Buffered(1))]
        in_specs += [pl.BlockSpec((None, k, tn), lambda i, j, c=c: (layer,) + c(i, j)) for c in cols]
        in_specs += [pl.BlockSpec((3, tn), c) for c in cols]
        in_specs += [pl.BlockSpec((1, tn), c) for c in cols]
        n_in = len(in_specs)
        vmem = (tm * k * 2 + 2 * nblk * (2 * k * tn * 2 + tm * tn * 2) + 16 * tm * tn * 4 + (4 << 20))
        return pl.pallas_call(
            functools.partial(_ffn_up_kernel, dims=dims, nblk=nblk, n_alias=len(carried)),
            grid=(m // tm, steps),
            in_specs=in_specs + [_ANY_SPEC] * len(carried),
            out_specs=pl.BlockSpec((tm, nblk * tn), lambda i, j: (i, first_block // nblk + j)),
            out_shape=jax.ShapeDtypeStruct((m, dff), BF16),
            input_output_aliases={n_in + i: i for i in range(len(carried))},
            compiler_params=_cparams(("arbitrary", "arbitrary"), vmem),
            name="ffn_up_conv_gate",
        )(h, *([w_up] * (2 * nblk)), *([conv_w] * (2 * nblk)), *([conv_b] * (2 * nblk)), *carried)

    nblk = min(FFN_BLOCKS_PER_STEP, nb)
    main_steps = nb // nblk
    act = run(0, main_steps, nblk, ())
    if main_steps * nblk < nb:
        act = run(main_steps * nblk, nb - main_steps * nblk, 1, (act,))
    return act


def _rope(x, cos, sin_signed):
    lane = lax.broadcasted_iota(jnp.int32, x.shape, 1)
    n = x.shape[1]
    swapped = jnp.where((lane & 1) == 0, pltpu.roll(x, n - 1, 1), pltpu.roll(x, 1, 1))
    return x * cos + swapped * sin_signed


def _softmax_parts(s, sc):
    m = jnp.max(s, axis=-1, keepdims=True)
    if sc is not None:
        m = jnp.maximum(m, jnp.max(sc, axis=-1, keepdims=True))
    e = jnp.exp(s - m)
    tot = jnp.sum(e, axis=-1, keepdims=True)
    ec = None
    if sc is not None:
        ec = jnp.exp(sc - m)
        tot = tot + jnp.sum(ec, axis=-1, keepdims=True)
    return e, ec, 1.0 / tot


def _attn_kernel(*refs, latent, heads, tq, lq, lam_init, n_alias):
    if latent:
        q_ref, k_ref, v_ref, kc_ref, vc_ref, cos_ref, sin_ref, lv_ref, g_ref = refs[:9]
        o_ref, kb_s, vb_s, kcb_s, vcb_s = refs[9 + n_alias:]
    else:
        q_ref, k_ref, v_ref, lv_ref, g_ref = refs[:5]
        o_ref, ko_ref, vo_ref, kb_s, vb_s = refs[5 + n_alias:]
        ko_ref[0, 0] = k_ref[...]
        vo_ref[0, 0] = v_ref[...]
    lv = lv_ref[...]
    lam = (jnp.exp(jnp.sum(lv[0:1] * lv[1:2], axis=-1, keepdims=True))
           - jnp.exp(jnp.sum(lv[2:3] * lv[3:4], axis=-1, keepdims=True)) + lam_init)
    gain = g_ref[...] * (1.0 - lam_init)
    scale = QK_DIM ** -0.5
    lo = lax.broadcasted_iota(jnp.int32, (tq, V_DIM), 1) < QK_DIM

    for h in range(heads):
        cols = slice(h * V_DIM, (h + 1) * V_DIM)
        k = k_ref[:, cols]
        if latent:
            k = _rope(k, cos_ref[...], sin_ref[...])
            kcb_s[...] = kc_ref[0, 0, :, cols].astype(BF16)
            vcb_s[...] = vc_ref[0, 0, :, cols].astype(BF16)
        kb_s[...] = k.astype(BF16)
        vb_s[...] = v_ref[:, cols].astype(BF16)

        def q_block(i, carry, cols=cols):
            rows = pl.ds(0, tq) if lq == tq else pl.ds(pl.multiple_of(i * tq, tq), tq)
            q = q_ref[rows, cols]
            if latent:
                q = _rope(q, cos_ref[rows, :], sin_ref[rows, :])
            q = q * scale
            q1 = jnp.where(lo, q, 0.0).astype(BF16)
            q2 = jnp.where(lo, 0.0, q).astype(BF16)
            kb = kb_s[...]
            s1 = _dot_nt(q1, kb)
            s2 = _dot_nt(q2, kb)
            s1c = s2c = None
            if latent:
                kcb = kcb_s[...]
                s1c = _dot_nt(q1, kcb)
                s2c = _dot_nt(q2, kcb)
            e1, e1c, r1 = _softmax_parts(s1, s1c)
            e2, e2c, r2 = _softmax_parts(s2, s2c)
            r2 = r2 * lam
            o = _dot((e1 * r1 - e2 * r2).astype(BF16), vb_s[...])
            if latent:
                o = o + _dot((e1c * r1 - e2c * r2).astype(BF16), vcb_s[...])
            rr = lax.rsqrt(jnp.mean(o * o, axis=-1, keepdims=True) + NORM_EPS)
            o_ref[rows, cols] = (o * rr * gain).astype(o_ref.dtype)
            return carry

        if lq == tq:
            q_block(0, 0)
        else:
            lax.fori_loop(0, lq // tq, q_block, 0, unroll=2)


def _attn_call(p, lv, g, dims, lam_init, latent, layer, carried, cache_k=None, cache_v=None,
               cos=None, sin=None):
    n_heads = dims.att_heads
    if latent:
        nb, lq, heads, row_off = dims.b_lat, dims.l_lat, 1, dims.n_ctx // dims.l_lat
    else:
        nb, lq, heads, row_off = dims.b_ctx, dims.l_ctx, _pick(n_heads, (4, 2, 1)), 0
    tq = min(lq, ATT_TQ)
    hb = n_heads // heads
    w = heads * V_DIM
    q_spec = pl.BlockSpec((lq, w), lambda b, h: (b + row_off, h))
    k_spec = pl.BlockSpec((lq, w), lambda b, h: (b + row_off, hb + h))
    v_spec = pl.BlockSpec((lq, w), lambda b, h: (b + row_off, 2 * hb + h))
    small = [pl.BlockSpec((4, QK_DIM), lambda b, h: (0, 0)),
             pl.BlockSpec((1, V_DIM), lambda b, h: (0, 0))]
    scratch = [pltpu.VMEM((lq, V_DIM), BF16), pltpu.VMEM((lq, V_DIM), BF16)]
    if latent:
        past = cache_k.shape[2]
        c_spec = pl.BlockSpec((1, 1, past, w), lambda b, h: (b, layer, 0, h))
        t_spec = pl.BlockSpec((lq, V_DIM), lambda b, h: (0, 0))
        in_specs = [q_spec, k_spec, v_spec, c_spec, c_spec, t_spec, t_spec] + small
        args = (p, p, p, cache_k, cache_v, cos, sin, lv, g.reshape(1, V_DIM))
        scratch += [pltpu.VMEM((past, V_DIM), BF16), pltpu.VMEM((past, V_DIM), BF16)]
        lk = lq + past
    else:
        in_specs = [q_spec, k_spec, v_spec] + small
        args = (p, p, p, lv, g.reshape(1, V_DIM))
        lk = lq
    n_in = len(args)
    att_shape = jax.ShapeDtypeStruct((dims.n_ctx + dims.n_lat, n_heads * V_DIM), BF16)
    att_spec = pl.BlockSpec((lq, w), lambda b, h: (b + row_off, h))
    if latent:
        out_shape, out_specs = att_shape, att_spec
        aliases = {n_in: 0}
    else:
        kv_shape = jax.ShapeDtypeStruct((nb, dims.depth, lq, n_heads * V_DIM), F32)
        kv_spec = pl.BlockSpec((1, 1, lq, w), lambda b, h: (b, layer, 0, h))
        out_shape, out_specs = (att_shape, kv_shape, kv_shape), (att_spec, kv_spec, kv_spec)
        aliases = {n_in + i: 1 + i for i in range(len(carried))}
    vmem = 2 * 6 * lq * w * 4 + 12 * tq * lk * 4 + (8 << 20)
    return pl.pallas_call(
        functools.partial(_attn_kernel, latent=latent, heads=heads, tq=tq, lq=lq,
                          lam_init=lam_init, n_alias=len(carried)),
        grid=(nb, hb),
        in_specs=in_specs + [_ANY_SPEC] * len(carried),
        out_specs=out_specs,
        out_shape=out_shape,
        input_output_aliases=aliases,
        scratch_shapes=scratch,
        compiler_params=_cparams(("arbitrary", "arbitrary"), vmem),
        name="diff_attention_latent" if latent else "diff_attention_context",
    )(*args, *carried)


def _softplus(x):
    return jnp.maximum(x, 0.0) + jnp.log1p(jnp.exp(-jnp.abs(x)))


def _ssd_kernel(*refs, seq_len, latent, n_alias, n_heads):
    it = iter(refs)
    x_ref, b_ref, c_ref, z_ref, dt_ref = (next(it) for _ in range(5))
    cwx_ref, cbx_ref, cwb_ref, cbb_ref, cwc_ref, cbc_ref = (next(it) for _ in range(6))
    par_ref, d_ref = next(it), next(it)
    if latent:
        sf0_ref, sb0_ref = next(it), next(it)
    for _ in range(n_alias):
        next(it)
    if latent:
        y_ref = next(it)
    else:
        y_ref, sf_ref, sb_ref = next(it), next(it), next(it)
    (xs_s, b_s, c_s, bt_s, cs_s, rcs_s, dtxf_s, dtxb_s, csxf_s, rcsxb_s,
     yf_s, yb_s, stf_s, stb_s) = it

    ch = SSD_CHUNK
    nc = seq_len // ch

    def conv_silu(u_ref, w_ref, bias_ref):
        u = u_ref[...]
        t = lax.broadcasted_iota(jnp.int32, u.shape, 0)
        return _silu(_conv3_rows(u, w_ref, bias_ref, t == 0, t == seq_len - 1))

    xs_s[...] = conv_silu(x_ref, cwx_ref, cbx_ref)
    bm = conv_silu(b_ref, cwb_ref, cbb_ref)
    b_s[...] = bm.astype(BF16)
    for c in range(nc):
        bt_s[c] = bm[c * ch:(c + 1) * ch, :].T.astype(BF16)
    c_s[...] = conv_silu(c_ref, cwc_ref, cbc_ref).astype(BF16)

    ri = lax.broadcasted_iota(jnp.int32, (ch, ch), 0)
    ci = lax.broadcasted_iota(jnp.int32, (ch, ch), 1)
    src = jnp.where(ci < SSD_HPG, ci, ci - SSD_HPG + n_heads) + pl.program_id(1) * SSD_HPG
    sel = jnp.where(ri == src, jnp.where(ci < 2 * SSD_HPG, 1.0, 0.0), 0.0).astype(BF16)
    par = _dot_f32_lhs(par_ref[...], sel)
    dt = _softplus(_dot_f32_lhs(dt_ref[...], sel) + par[0:1])
    dta = dt * (-jnp.exp(par[1:2]))
    lower = ri >= ci
    upper = ri <= ci
    t_low = jnp.where(lower, 1.0, 0.0).astype(BF16)
    t_up = jnp.where(upper, 1.0, 0.0).astype(BF16)
    for c in range(nc):
        blk = dta[c * ch:(c + 1) * ch, :]
        cs_s[c * ch:(c + 1) * ch, :] = _dot_f32_rhs(t_low, blk)
        rcs_s[c * ch:(c + 1) * ch, :] = _dot_f32_rhs(t_up, blk)
    er = lax.broadcasted_iota(jnp.int32, (LANE, SSD_GW), 0)
    eh = lax.broadcasted_iota(jnp.int32, (LANE, SSD_GW), 1) >> int(math.log2(SSD_P))
    exp_f = jnp.where(er == eh, 1.0, 0.0).astype(BF16)
    exp_b = jnp.where(er == eh + SSD_HPG, 1.0, 0.0).astype(BF16)
    dtxf_s[...] = _dot_f32_lhs(dt, exp_f, terms=2)
    dtxb_s[...] = _dot_f32_lhs(dt, exp_b, terms=2)
    csxf_s[...] = _dot_f32_lhs(cs_s[...], exp_f)
    rcsxb_s[...] = _dot_f32_lhs(rcs_s[...], exp_b)

    lo = lax.broadcasted_iota(jnp.int32, (ch, LANE), 1) < SSD_P

    def chunk_step(c, forward, y_buf, st_buf):
        rows = pl.ds(pl.multiple_of(c * ch, ch), ch)
        if forward:
            dtx, ex, cs, base, tri = dtxf_s[rows, :], csxf_s[rows, :], cs_s[rows, :], 0, lower
        else:
            dtx, ex, cs, base, tri = dtxb_s[rows, :], rcsxb_s[rows, :], rcs_s[rows, :], SSD_HPG, upper
        xdt = xs_s[rows, :] * dtx
        cc = c_s[rows, :]
        cb = _dot_nt(cc, b_s[rows, :])
        st = st_buf[...]
        y_off =_dot(cc, st.astype(BF16)) * jnp.exp(ex)
        cs_t = cs.T
        for j in range(SSD_GW // LANE):
            lanes = slice(j * LANE, (j + 1) * LANE)
            xp = xdt[:, lanes]
            acc = y_off[:, lanes]
            for k in range(LANE // SSD_P):
                hh = base + j * (LANE // SSD_P) + k
                diff = cs[:, hh:hh + 1] - cs_t[hh:hh + 1, :]
                decay = jnp.exp(jnp.where(tri, diff, -jnp.inf))
                g = (cb * decay).astype(BF16)
                xm = jnp.where(lo if k == 0 else jnp.logical_not(lo), xp, 0.0).astype(BF16)
                acc = acc + _dot(g, xm)
            y_buf[rows, lanes] = acc
        edge = ex[ch - 1:ch, :] if forward else ex[0:1, :]
        xd = (xdt * jnp.exp(edge - ex)).astype(BF16)
        st_buf[...] = jnp.exp(edge) * st + _dot(bt_s[c], xd)

    def both_directions(i, carry):
        chunk_step(i, True, yf_s, stf_s)
        chunk_step(nc - 1 - i, False, yb_s, stb_s)
        return carry

    stf_s[...] = sf0_ref[0, 0].T if latent else jnp.zeros(stf_s.shape, F32)
    stb_s[...] = sb0_ref[0, 0].T if latent else jnp.zeros(stb_s.shape, F32)
    lax.fori_loop(0, nc, both_directions, 0)
    if not latent:
        sf_ref[0, 0] = stf_s[...].T
        sb_ref[0, 0] = stb_s[...].T

    y = yf_s[...] + yb_s[...] + d_ref[...] * xs_s[...]
    y_ref[...] = y * _silu(z_ref[...])


def _ssd_call(p, lw, dims, latent, layer, carried, init_f=None, init_b=None):
    groups = dims.ssd_groups
    gw = SSD_GW
    if latent:
        nb, sl, row_off = dims.b_lat, dims.l_lat, dims.n_ctx // dims.l_lat
    else:
        nb, sl, row_off = dims.b_ctx, dims.l_ctx, 0
    nc = sl // SSD_CHUNK
    o = dims.offs
    zb, xb, bb, cb, db = o.z // gw, o.x // gw, o.b // LANE, o.c // LANE, o.dt // LANE
    in_specs = [
        pl.BlockSpec((sl, gw), lambda b, g: (b + row_off, xb + g)),
        pl.BlockSpec((sl, LANE), lambda b, g: (b + row_off, bb + g)),
        pl.BlockSpec((sl, LANE), lambda b, g: (b + row_off, cb + g)),
        pl.BlockSpec((sl, gw), lambda b, g: (b + row_off, zb + g)),
        pl.BlockSpec((sl, LANE), lambda b, g: (b + row_off, db)),
        pl.BlockSpec((3, gw), lambda b, g: (0, g)),
        pl.BlockSpec((1, gw), lambda b, g: (0, g)),
        pl.BlockSpec((3, LANE), lambda b, g: (0, dims.ssd_width // LANE + g)),
        pl.BlockSpec((1, LANE), lambda b, g: (0, dims.ssd_width // LANE + g)),
        pl.BlockSpec((3, LANE), lambda b, g: (0, dims.ssd_width // LANE + groups + g)),
        pl.BlockSpec((1, LANE), lambda b, g: (0, dims.ssd_width // LANE + groups + g)),
        pl.BlockSpec((None, 8, LANE), lambda b, g: (layer, 0, 0)),
        pl.BlockSpec((1, gw), lambda b, g: (0, g)),
    ]
    args = [p, p, p, p, p, lw["ssd_conv_w"], lw["ssd_conv_b"], lw["ssd_conv_w"], lw["ssd_conv_b"],
            lw["ssd_conv_w"], lw["ssd_conv_b"], lw["ssd_par"], lw["ssd_d"]]
    y_shape = jax.ShapeDtypeStruct((dims.n_ctx + dims.n_lat, dims.ssd_width), F32)
    y_spec = pl.BlockSpec((sl, gw), lambda b, g: (b + row_off, g))
    s_spec = pl.BlockSpec((1, 1, gw, SSD_N), lambda b, g: (b, layer, g, 0))
    if latent:
        in_specs += [s_spec, s_spec]
        args += [init_f, init_b]
        out_shape, out_specs = y_shape, y_spec
        aliases = {len(args): 0}
    else:
        s_shape = jax.ShapeDtypeStruct((nb, dims.depth, dims.ssd_width, SSD_N), F32)
        out_shape, out_specs = (y_shape, s_shape, s_shape), (y_spec, s_spec, s_spec)
        aliases = {len(args) + i: 1 + i for i in range(len(carried))}
    scratch = [
        pltpu.VMEM((sl, gw), F32), pltpu.VMEM((sl, LANE), BF16), pltpu.VMEM((sl, LANE), BF16),
        pltpu.VMEM((nc, SSD_CHUNK, SSD_CHUNK), BF16),
        pltpu.VMEM((sl, LANE), F32), pltpu.VMEM((sl, LANE), F32),
        pltpu.VMEM((sl, gw), F32), pltpu.VMEM((sl, gw), F32),
        pltpu.VMEM((sl, gw), F32), pltpu.VMEM((sl, gw), F32),
        pltpu.VMEM((sl, gw), F32), pltpu.VMEM((sl, gw), F32),
        pltpu.VMEM((SSD_N, gw), F32), pltpu.VMEM((SSD_N, gw), F32),
    ]
    vmem = 2 * sl * (3 * gw + 3 * LANE) * 4 + sl * (8 * gw + 4 * LANE) * 4 + 8 * sl * gw * 4 + (8 << 20)
    return pl.pallas_call(
        functools.partial(_ssd_kernel, seq_len=sl, latent=latent, n_alias=len(carried),
                          n_heads=dims.ssd_heads),
        grid=(nb, groups),
        in_specs=in_specs + [_ANY_SPEC] * len(carried),
        out_specs=out_specs,
        out_shape=out_shape,
        input_output_aliases=aliases,
        scratch_shapes=scratch,
        compiler_params=_cparams(("arbitrary", "arbitrary"), vmem),
        name="ssd_latent" if latent else "ssd_context",
    )(*args, *carried)


def _dft_mats(l):
    n = 3 * l // 2
    nh = n // 2
    k = np.arange(nh, dtype=np.int64)[:, None]
    t = np.arange(l, dtype=np.int64)[None, :]
    ang = 2.0 * np.pi * ((k * t) % n).astype(np.float64) / n
    fwd = np.concatenate([np.cos(ang), -np.sin(ang)], axis=0)
    fwd[nh, :] = np.where(np.arange(l) % 2 == 0, 1.0, -1.0)
    pos = (np.arange(l, dtype=np.int64) + l // 2)[:, None]
    kk = np.arange(nh, dtype=np.int64)[None, :]
    ang2 = 2.0 * np.pi * ((pos * kk) % n).astype(np.float64) / n
    inv = np.concatenate([2.0 * np.cos(ang2), -2.0 * np.sin(ang2)], axis=1) / n
    inv[:, 0] = 1.0 / n
    inv[:, nh] = np.where(pos[:, 0] % 2 == 0, 1.0, -1.0) / n

    def split(m):
        hi = m.astype(np.float32).astype(BF16)
        lo = (m - hi.astype(np.float64)).astype(np.float32).astype(BF16)
        return hi, lo

    return split(fwd) + split(inv)


def _hy_feats(l):
    off = jnp.arange(l, dtype=F32) - (l // 2)
    band = HY_FREQ_BASE ** (-jnp.arange(HY_BANDS, dtype=F32) / HY_BANDS)
    ang = off[:, None] * band
    feats = jnp.concatenate([off[:, None] / l, jnp.sin(ang), jnp.cos(ang)], axis=-1)
    return jnp.pad(feats, ((0, 0), (0, LANE - feats.shape[1])))


def _hy_filter_kernel(feats_ref, w1_ref, b1_ref, w2_ref, b2_ref, fr_ref, w3_ref, dl_ref,
                      fh_ref, fl_ref, o_ref, *, seq_len):
    fr = fr_ref[0]
    h1 = jnp.sin(fr[0:1] * (_dot_hp(feats_ref[...], w1_ref[0]) + b1_ref[0]))
    h2 = jnp.sin(fr[1:2] * (_dot_hp(h1, w2_ref[0]) + b2_ref[0]))
    filt = _dot_hp(h2, w3_ref[0])
    off = lax.broadcasted_iota(jnp.int32, filt.shape, 0).astype(F32) - (seq_len // 2)
    dist = jnp.abs(off) * (2.0 / seq_len)
    h = filt * jnp.exp(-dist * dl_ref[...])
    hh, hl = _split_bf16(h, 2)
    fh = fh_ref[...]
    o_ref[0, 0] = _dot(fh, hh) + _dot(fh, hl) + _dot(fl_ref[...], hh)


def _hy_filter_call(seq_len, hw, mats, dims):
    depth, hy = dims.depth, dims.hy_width
    cb = HY_CB
    ncb = hy // cb
    fh, fl = mats[0], mats[1]
    n = fh.shape[0]
    feats = _hy_feats(seq_len)
    deltas = jnp.linspace(HY_MIN_DECAY, HY_MAX_DECAY, hy, dtype=F32).reshape(1, hy)
    vmem = 2 * 2 * (n * seq_len * 2) + 6 * n * cb * 4 + (8 << 20)
    return pl.pallas_call(
        functools.partial(_hy_filter_kernel, seq_len=seq_len),
        grid=(depth, 2, ncb),
        in_specs=[
            pl.BlockSpec((seq_len, LANE), lambda l, o, c: (0, 0)),
            pl.BlockSpec((1, LANE, LANE), lambda l, o, c: (l, 0, 0)),
            pl.BlockSpec((1, 1, LANE), lambda l, o, c: (l, 0, 0)),
            pl.BlockSpec((1, LANE, LANE), lambda l, o, c: (l, 0, 0)),
            pl.BlockSpec((1, 1, LANE), lambda l, o, c: (l, 0, 0)),
            pl.BlockSpec((1, 2, LANE), lambda l, o, c: (l, 0, 0)),
            pl.BlockSpec((1, LANE, cb), lambda l, o, c: (l, 0, o * ncb + c)),
            pl.BlockSpec((1, cb), lambda l, o, c: (0, c)),
            pl.BlockSpec((n, seq_len), lambda l, o, c: (0, 0)),
            pl.BlockSpec((n, seq_len), lambda l, o, c: (0, 0)),
        ],
        out_specs=pl.BlockSpec((1, 1, n, cb), lambda l, o, c: (l, o, 0, c)),
        out_shape=jax.ShapeDtypeStruct((depth, 2, n, hy), F32),
        compiler_params=_cparams(("arbitrary",) * 3, vmem),
        name="hyena_filter_spectrum",
    )(feats, hw["w1"], hw["b1"], hw["w2"], hw["b2"], hw["freq"], hw["w3"], deltas, fh, fl)


def _hy_conv_kernel(*refs, seq_len, n_seq, n_alias):
    (x1_ref, x2_ref, v_ref, cw1_ref, cb1_ref, cw2_ref, cb2_ref, cwv_ref, cbv_ref,
     hf_ref, bias_ref, fh_ref, fl_ref, gh_ref, gl_ref) = refs[:15]
    o_ref = refs[15 + n_alias]
    nh = fh_ref.shape[0] // 2
    cb = o_ref.shape[1]
    t = lax.broadcasted_iota(jnp.int32, (seq_len, cb), 0)
    first = t == 0
    last = t == seq_len - 1
    bin0 = lax.broadcasted_iota(jnp.int32, (nh, cb), 0) == 0
    fh = fh_ref[...]
    gh = gh_ref[...]
    rows = [slice(s * seq_len, (s + 1) * seq_len) for s in range(n_seq)]
    gates = [(_conv3_rows(x1_ref[r, :], cw1_ref, cb1_ref, first, last),
              _conv3_rows(x2_ref[r, :], cw2_ref, cb2_ref, first, last)) for r in rows]
    zz = [_conv3_rows(v_ref[r, :], cwv_ref, cbv_ref, first, last) for r in rows]
    for o in range(2):
        hr, hi = hf_ref[0, o, :nh, :], hf_ref[0, o, nh:, :]
        uf = []
        for z in zz:
            zh, zl = _split_bf16(z, 2)
            uf.append(_dot(fh, zh) + _dot(fh, zl) + _dot(fl_ref[...], zh))
        ys = []
        for u in uf:
            ur, ui = u[:nh], u[nh:]
            ii = ui * hi
            yr = ur * hr - jnp.where(bin0, 0.0, ii)
            yi = jnp.where(bin0, ii, ur * hi + ui * hr)
            yh, yl = _split_bf16(jnp.concatenate([yr, yi], axis=0), 2)
            ys.append(_dot(gh, yh) + _dot(gh, yl) + _dot(gl_ref[...], yh))
        zz = [g[o] * (y + z * bias_ref[o:o + 1, :]) for g, y, z in zip(gates, ys, zz)]
    for r, z in zip(rows, zz):
        o_ref[r, :] = z


def _hy_conv_call(p, hf, lw, mats, dims, layer, latent, carried):
    hy = dims.hy_width
    cb = HY_CB
    ncb = hy // cb
    if latent:
        nb, sl, row_start = dims.b_lat, dims.l_lat, dims.n_ctx
    else:
        nb, sl, row_start = dims.b_ctx, dims.l_ctx, 0
    n_seq = max(n for n in (8, 4, 2, 1) if nb % n == 0 and (n == 1 or n * sl <= HY_ROWS_PER_STEP))
    rows = n_seq * sl
    row_off = row_start // rows
    n = mats[0].shape[0]
    hb = dims.offs.hy // cb
    const = lambda c, b: (0, 0)
    in_specs = [pl.BlockSpec((rows, cb), lambda c, b, k=k: (b + row_off, hb + k * ncb + c)) for k in range(3)]
    for k in range(3):
        in_specs += [pl.BlockSpec((3, cb), lambda c, b, k=k: (0, k * ncb + c)),
                     pl.BlockSpec((1, cb), lambda c, b, k=k: (0, k * ncb + c))]
    once = pl.Buffered(1)
    in_specs += [
        pl.BlockSpec((1, 2, n, cb), lambda c, b: (layer, 0, 0, c), pipeline_mode=once),
        pl.BlockSpec((2, cb), lambda c, b: (0, c)),
        pl.BlockSpec((n, sl), const, pipeline_mode=once),
        pl.BlockSpec((n, sl), const, pipeline_mode=once),
        pl.BlockSpec((sl, n), const, pipeline_mode=once),
        pl.BlockSpec((sl, n), const, pipeline_mode=once),
    ]
    cw, cbias = lw["hy_conv_w"], lw["hy_conv_b"]
    vmem = (4 * (n * sl * 2) + 2 * n * cb * 4 + 2 * 4 * rows * cb * 4
            + n_seq * (6 * n + 8 * sl) * cb * 4 + (8 << 20))
    return pl.pallas_call(
        functools.partial(_hy_conv_kernel, seq_len=sl, n_seq=n_seq, n_alias=len(carried)),
        grid=(ncb, nb // n_seq),
        in_specs=in_specs + [_ANY_SPEC] * len(carried),
        out_specs=pl.BlockSpec((rows, cb), lambda c, b: (b + row_off, c)),
        out_shape=jax.ShapeDtypeStruct((dims.n_ctx + dims.n_lat, hy), F32),
        input_output_aliases={15 + i: i for i in range(len(carried))},
        compiler_params=_cparams(("arbitrary", "arbitrary"), vmem),
        name="hyena_conv_latent" if latent else "hyena_conv_context",
    )(p, p, p, cw, cbias, cw, cbias, cw, cbias, hf, lw["hy_bias"], *mats, *carried)


class _Offsets:
    def __init__(self, att_w, ssd_w, groups, hy_w):
        self.q, self.k, self.v = 0, att_w, 2 * att_w
        self.z = 3 * att_w
        self.x = self.z + ssd_w
        self.b = self.x + ssd_w
        self.c = self.b + groups * SSD_N
        self.hy = self.c + groups * SSD_N
        self.dt = self.hy + 3 * hy_w
        used = self.dt + LANE
        self.total = PROJ_COL_ALIGN * ((used + PROJ_COL_ALIGN - 1) // PROJ_COL_ALIGN)


class _Dims:
    pass


def _make_dims(x_prompt, x_sample, cache_k, state_ssd_fwd, w_in, hy_norm_g, ffn_w_down):
    d = _Dims()
    d.b_ctx, d.l_ctx, d.d_model = x_prompt.shape
    d.b_lat, d.l_lat, _ = x_sample.shape
    d.n_ctx = d.b_ctx * d.l_ctx
    d.n_lat = d.b_lat * d.l_lat
    d.depth = w_in.shape[0]
    d.att_heads = cache_k.shape[3]
    d.att_width = d.att_heads * V_DIM
    d.ssd_heads = state_ssd_fwd.shape[2]
    d.ssd_width = d.ssd_heads * SSD_P
    d.ssd_groups = d.ssd_heads // SSD_HPG
    d.hy_width = hy_norm_g.shape[-1]
    d.d_ff = ffn_w_down.shape[1]
    d.offs = _Offsets(d.att_width, d.ssd_width, d.ssd_groups, d.hy_width)
    d.tm = d.l_lat
    assert d.l_lat % d.l_ctx == 0 and d.n_ctx % d.tm == 0 and d.l_ctx % 256 == 0
    assert d.l_lat & (d.l_lat - 1) == 0 and d.l_ctx & (d.l_ctx - 1) == 0
    assert d.offs.hy % HY_CB == 0 and d.hy_width % HY_CB == 0 and d.d_ff % FFN_TN == 0
    assert d.ssd_heads % SSD_HPG == 0
    return d


def _rope_tables(seq_len):
    rows = seq_len // GRID_W
    r, col = jnp.meshgrid(jnp.arange(rows), jnp.arange(GRID_W), indexing="ij")
    npair = QK_DIM // 4
    inv = ROPE_BASE ** (-jnp.arange(npair, dtype=F32) / npair)
    ang = jnp.concatenate([r.reshape(-1, 1).astype(F32) * inv, col.reshape(-1, 1).astype(F32) * inv], axis=-1)
    cos, sin = jnp.cos(ang), jnp.sin(ang)
    cos2 = jnp.repeat(cos, 2, axis=-1)
    sin2 = jnp.stack([-sin, sin], axis=-1).reshape(seq_len, QK_DIM)
    return jnp.tile(cos2, (1, 2)), jnp.tile(sin2, (1, 2))


def kernel(x_prompt, x_sample, cache_k, cache_v, state_ssd_fwd, state_ssd_bwd, c, c_ctx, norm1_g, norm2_g, w_ada, b_ada, w_in, att_lambda, att_subln_g, ssd_conv_w, ssd_conv_b, ssd_dt_bias, ssd_a_log, ssd_d, ssd_norm_g, hy_conv_w, hy_conv_b, hy_w1, hy_b1, hy_w2, hy_b2, hy_freq, hy_w3, hy_bias, hy_norm_g, w_out, ffn_w_up, ffn_conv_w, ffn_conv_b, ffn_w_down, final_norm_g):
    dims = _make_dims(x_prompt, x_sample, cache_k, state_ssd_fwd, w_in, hy_norm_g, ffn_w_down)
    d, depth, offs = dims.d_model, dims.depth, dims.offs
    n_ctx = dims.n_ctx

    mod_rows = 16 * ((1 + dims.b_lat + 15) // 16)
    cond = jnp.zeros((mod_rows, d), F32).at[0].set(c_ctx).at[1:1 + dims.b_lat].set(c)
    mod_all = _ada_call(cond, w_ada, b_ada).reshape(depth, mod_rows, 6, 1, d)

    pad_h = LANE - hy_w1.shape[-1]
    hw = {
        "w1": jnp.pad(hy_w1, ((0, 0), (0, LANE - hy_w1.shape[1]), (0, pad_h))),
        "b1": jnp.pad(hy_b1, ((0, 0), (0, pad_h)))[:, None, :],
        "w2": jnp.pad(hy_w2, ((0, 0), (0, pad_h), (0, pad_h))),
        "b2": jnp.pad(hy_b2, ((0, 0), (0, pad_h)))[:, None, :],
        "freq": jnp.pad(hy_freq, ((0, 0), (0, 0), (0, pad_h))),
        "w3": jnp.pad(hy_w3, ((0, 0), (0, pad_h), (0, 0))),
    }
    mats_ctx = _dft_mats(dims.l_ctx)
    mats_lat = _dft_mats(dims.l_lat)
    hf_ctx = _hy_filter_call(dims.l_ctx, hw, mats_ctx, dims)
    hf_lat = _hy_filter_call(dims.l_lat, hw, mats_lat, dims)

    cos_t, sin_t = _rope_tables(dims.l_lat)
    ck = cache_k.reshape(dims.b_lat, depth, cache_k.shape[2], dims.att_width)
    cv = cache_v.reshape(dims.b_lat, depth, cache_v.shape[2], dims.att_width)
    sf0 = state_ssd_fwd.reshape(dims.b_lat, depth, dims.ssd_width, SSD_N)
    sb0 = state_ssd_bwd.reshape(dims.b_lat, depth, dims.ssd_width, SSD_N)

    x = jnp.concatenate([x_prompt.reshape(n_ctx, d), x_sample.reshape(dims.n_lat, d)], axis=0)
    new_kv, new_st = (), ()

    dt0, dt_cols = offs.hy, 2 * dims.ssd_heads
    assert dt_cols <= LANE
    w_proj = jnp.concatenate(
        [w_in[:, :, :dt0], w_in[:, :, dt0 + dt_cols:], w_in[:, :, dt0:dt0 + dt_cols],
         jnp.zeros((depth, d, offs.total - w_in.shape[2]), F32)], axis=2).astype(BF16)
    ssd_par = jnp.zeros((depth, 8, LANE), F32)
    ssd_par = ssd_par.at[:, 0, :dt_cols].set(ssd_dt_bias.reshape(depth, dt_cols))
    ssd_par = ssd_par.at[:, 1, :dt_cols].set(ssd_a_log.reshape(depth, dt_cols))
    w_o = w_out.astype(BF16)
    w_up = ffn_w_up.astype(BF16)
    w_down = ffn_w_down.astype(BF16)
    a_w, s_w = dims.att_width, dims.ssd_width
    if a_w % s_w == 0 and (a_w + s_w) % dims.hy_width == 0:
        out_terms = lambda l, att, y_ssd, y_hy: [
            (att, w_o, l, 0), (y_ssd, w_o, l, a_w // s_w), (y_hy, w_o, l, (a_w + s_w) // dims.hy_width)]
    else:
        out_terms = lambda l, att, y_ssd, y_hy: [
            (att, w_o[l:l + 1, :a_w], 0, 0), (y_ssd, w_o[l:l + 1, a_w:a_w + s_w], 0, 0),
            (y_hy, w_o[l:l + 1, a_w + s_w:], 0, 0)]

    for l in range(depth):
        lam_init = 0.8 - 0.6 * math.exp(-0.3 * l)
        mod = mod_all[l]
        lw = {
            "ssd_conv_w": ssd_conv_w[l], "ssd_conv_b": ssd_conv_b[l].reshape(1, -1),
            "ssd_par": ssd_par,
            "ssd_d": jnp.repeat(ssd_d[l], SSD_P).reshape(1, -1),
            "hy_conv_w": hy_conv_w[l], "hy_conv_b": hy_conv_b[l].reshape(1, -1), "hy_bias": hy_bias[l],
        }

        h = _norm_mod_call(x, norm1_g[l], mod, 0, 1, dims)
        p = _mm_call(h, w_proj, l, dims)

        att, new_k, new_v = _attn_call(p, att_lambda[l], att_subln_g[l], dims, lam_init, False, l, new_kv)
        new_kv = (new_k, new_v)
        att = _attn_call(p, att_lambda[l], att_subln_g[l], dims, lam_init, True, l, (att,),
                         cache_k=ck, cache_v=cv, cos=cos_t, sin=sin_t)
        ys, new_sf, new_sb = _ssd_call(p, lw, dims, False, l, new_st)
        new_st = (new_sf, new_sb)
        ys = _ssd_call(p, lw, dims, True, l, (ys,), init_f=sf0, init_b=sb0)
        zz = _hy_conv_call(p, hf_ctx, lw, mats_ctx, dims, l, False, ())
        zz = _hy_conv_call(p, hf_lat, lw, mats_lat, dims, l, True, (zz,))

        y_ssd = _norm_call(ys, ssd_norm_g[l], BF16, name="ssd_norm")
        y_hy = _norm_call(zz, hy_norm_g[l], BF16, name="hyena_norm")
        x = _mm_res_call(out_terms(l, att, y_ssd, y_hy), x, mod, 2, dims,
                         (1024, 768, 512, 384, 256, 128))

        h2 = _norm_mod_call(x, norm2_g[l], mod, 3, 4, dims)
        act = _ffn_up_call(h2, w_up, l, ffn_conv_w[l], ffn_conv_b[l], dims)
        x = _mm_res_call([(act, w_down, l, 0)], x, mod, 5, dims, (DOWN_TILE, 256, 128),
                         tm=min(DOWN_TILE, dims.tm))

    y_prompt = _norm_call(x, final_norm_g, F32, 0, n_ctx, name="final_norm").reshape(x_prompt.shape)
    y_sample = _norm_call(x, final_norm_g, F32, n_ctx, dims.n_lat, name="final_norm").reshape(x_sample.shape)
    kv_shape = (dims.b_ctx, depth, dims.l_ctx, dims.att_heads, V_DIM)
    st_shape = (dims.b_ctx, depth, dims.ssd_heads, SSD_P, SSD_N)
    return (y_prompt, y_sample, new_kv[0].reshape(kv_shape), new_kv[1].reshape(kv_shape),
            new_st[0].reshape(st_shape), new_st[1].reshape(st_shape))
```

```python
import functools
import math

import numpy as np
import jax
import jax.numpy as jnp
from jax import lax
from jax.experimental import pallas as pl
from jax.experimental.pallas import tpu as pltpu

F32 = jnp.float32
BF16 = jnp.bfloat16

LANE = 128
V7X_VMEM_BYTES = 64 * 1024 * 1024
VMEM_CAP = V7X_VMEM_BYTES - 6 * 1024 * 1024

NORM_EPS = 1e-6
GRID_W = 64
ROPE_BASE = 10000.0
QK_DIM = 64
V_DIM = 128
SSD_P = 64
SSD_N = 128
SSD_CHUNK = 128
SSD_HPG = 6
SSD_GW = SSD_HPG * SSD_P
HY_BANDS = 16
HY_FREQ_BASE = 10000.0
HY_MIN_DECAY = -math.log(1e-2) / 1.5
HY_MAX_DECAY = -math.log(1e-2) / 0.3
HY_CB = 256
FFN_TN = 256
FFN_BLOCKS_PER_STEP = 3
PROJ_COL_ALIGN = 1024
ATT_TQ = 1024
DOWN_TILE = 512
HY_ROWS_PER_STEP = 2048

_ANY_SPEC = pl.BlockSpec(memory_space=pl.ANY)


def _cparams(sem, vmem_bytes):
    limit = int(min(VMEM_CAP, max(vmem_bytes, 16 * 1024 * 1024)))
    return pltpu.CompilerParams(dimension_semantics=sem, vmem_limit_bytes=limit)


def _pick(n, candidates):
    for c in candidates:
        if n % c == 0:
            return c
    raise ValueError(f"no block size in {candidates} divides {n}")


def _dot(a, b):
    return jnp.dot(a, b, preferred_element_type=F32)


def _dot_nt(a, b):
    return lax.dot_general(a, b, (((1,), (1,)), ((), ())), preferred_element_type=F32)


def _split_bf16(x, n):
    parts = []
    r = x
    for _ in range(n):
        h = r.astype(BF16)
        parts.append(h)
        r = r - h.astype(F32)
    return parts


def _dot_f32_lhs(a, b_bf16, terms=3):
    return sum(_dot(p, b_bf16) for p in _split_bf16(a, terms))


def _dot_f32_rhs(a_bf16, b):
    return sum(_dot(a_bf16, p) for p in _split_bf16(b, 3))


def _dot_hp(a, b):
    ah, al = _split_bf16(a, 2)
    bh, bl = _split_bf16(b, 2)
    return _dot(ah, bh) + _dot(ah, bl) + _dot(al, bh)


def _silu(x):
    return x * jax.nn.sigmoid(x)


def _conv3_rows(u, w_ref, b_ref, first, last):
    n = u.shape[0]
    prev = jnp.where(first, 0.0, pltpu.roll(u, 1, 0))
    nxt = jnp.where(last, 0.0, pltpu.roll(u, n - 1, 0))
    return prev * w_ref[0:1, :] + b_ref[...] + u * w_ref[1:2, :] + nxt * w_ref[2:3, :]


def _ada_kernel(c_ref, w_ref, b_ref, o_ref):
    s = _silu(c_ref[...]).astype(BF16)
    o_ref[0] = _dot(s, w_ref[0].astype(BF16)) + b_ref[0]


def _ada_call(cond, w_ada, b_ada):
    depth, d, n = w_ada.shape
    rows = cond.shape[0]
    tn = _pick(n, (512, 256, 128))
    vmem = 2 * (d * tn * 4) + d * tn * 2 + 4 * rows * (d + 2 * tn) * 4 + (4 << 20)
    return pl.pallas_call(
        _ada_kernel,
        grid=(depth, n // tn),
        in_specs=[
            pl.BlockSpec((rows, d), lambda l, j: (0, 0)),
            pl.BlockSpec((1, d, tn), lambda l, j: (l, 0, j)),
            pl.BlockSpec((1, 1, tn), lambda l, j: (l, 0, j)),
        ],
        out_specs=pl.BlockSpec((1, rows, tn), lambda l, j: (l, 0, j)),
        out_shape=jax.ShapeDtypeStruct((depth, rows, n), F32),
        compiler_params=_cparams(("arbitrary", "arbitrary"), vmem),
        name="ada_mod",
    )(cond, w_ada, b_ada.reshape(depth, 1, n))


def _norm_mod_kernel(x_ref, g_ref, sc_ref, sh_ref, o_ref):
    x = x_ref[...]
    r = lax.rsqrt(jnp.mean(x * x, axis=-1, keepdims=True) + NORM_EPS)
    y = x * r * g_ref[...]
    o_ref[...] = (y * (1.0 + sc_ref[0, 0]) + sh_ref[0, 0]).astype(o_ref.dtype)


def _norm_kernel(x_ref, g_ref, o_ref):
    x = x_ref[...]
    r = lax.rsqrt(jnp.mean(x * x, axis=-1, keepdims=True) + NORM_EPS)
    o_ref[...] = (x * r * g_ref[...]).astype(o_ref.dtype)


def _mod_row(i, tm, dims):
    start = i * tm
    lat = jnp.maximum(start - dims.n_ctx, 0) // dims.l_lat
    return jnp.where(start >= dims.n_ctx, lat + 1, 0)


def _norm_mod_call(x, g, mod, sh_idx, sc_idx, dims):
    m, d = x.shape
    tr = 256
    row = functools.partial(_mod_row, tm=tr, dims=dims)
    vmem = 2 * tr * d * (4 + 2) + 3 * tr * d * 4 + (4 << 20)
    return pl.pallas_call(
        _norm_mod_kernel,
        grid=(m // tr,),
        in_specs=[
            pl.BlockSpec((tr, d), lambda i: (i, 0)),
            pl.BlockSpec((1, d), lambda i: (0, 0)),
            pl.BlockSpec((1, 1, 1, d), lambda i: (row(i), sc_idx, 0, 0)),
            pl.BlockSpec((1, 1, 1, d), lambda i: (row(i), sh_idx, 0, 0)),
        ],
        out_specs=pl.BlockSpec((tr, d), lambda i: (i, 0)),
        out_shape=jax.ShapeDtypeStruct((m, d), BF16),
        compiler_params=_cparams(("arbitrary",), vmem),
        name="norm_mod",
    )(x, g.reshape(1, d), mod, mod)


def _norm_call(x, g, out_dtype, row_start=0, rows=None, name="norm"):
    m, d = x.shape
    rows = m if rows is None else rows
    tr = 256
    off = row_start // tr
    vmem = 2 * tr * d * 8 + 3 * tr * d * 4 + (4 << 20)
    return pl.pallas_call(
        _norm_kernel,
        grid=(rows // tr,),
        in_specs=[
            pl.BlockSpec((tr, d), lambda i: (i + off, 0)),
            pl.BlockSpec((1, d), lambda i: (0, 0)),
        ],
        out_specs=pl.BlockSpec((tr, d), lambda i: (i, 0)),
        out_shape=jax.ShapeDtypeStruct((rows, d), out_dtype),
        compiler_params=_cparams(("arbitrary",), vmem),
        name=name,
    )(x, g.reshape(1, d))


def _mm_kernel(a_ref, b_ref, o_ref):
    o_ref[...] = _dot(a_ref[...], b_ref[...]).astype(o_ref.dtype)


def _mm_res_kernel(*refs, n_terms):
    x_ref, g_ref, o_ref = refs[2 * n_terms:]
    acc = _dot(refs[0][...], refs[1][...])
    for i in range(1, n_terms):
        acc = acc + _dot(refs[2 * i][...], refs[2 * i + 1][...])
    o_ref[...] = x_ref[...] + g_ref[0, 0] * acc


def _mm_call(a, w_stack, layer, dims, out_dtype=F32):
    m, k = a.shape
    n = w_stack.shape[2]
    b = w_stack
    tm = dims.tm
    tn = _pick(n, (1024, 768, 512, 384, 256, 128))
    vmem = 2 * (tm * k * 2 + k * tn * 2 + tm * tn * 4) + tm * tn * 4 + (4 << 20)
    return pl.pallas_call(
        _mm_kernel,
        grid=(m // tm, n // tn),
        in_specs=[
            pl.BlockSpec((tm, k), lambda i, j: (i, 0)),
            pl.BlockSpec((None, k, tn), lambda i, j: (layer, 0, j)),
        ],
        out_specs=pl.BlockSpec((tm, tn), lambda i, j: (i, j)),
        out_shape=jax.ShapeDtypeStruct((m, n), out_dtype),
        compiler_params=_cparams(("arbitrary", "arbitrary"), vmem),
        name="proj",
    )(a, b)


def _mm_res_call(terms, x, mod, g_idx, dims, tn_candidates, tm=None):
    m, n = x.shape
    k = sum(t[0].shape[1] for t in terms)
    tm = dims.tm if tm is None else tm
    tn = _pick(n, tn_candidates)
    row = functools.partial(_mod_row, tm=tm, dims=dims)
    vmem = 2 * (tm * k * 2 + k * tn * 2 + 2 * tm * tn * 4) + tm * tn * 4 + (4 << 20)
    in_specs, args = [], []
    for a, w_stack, layer, row_block in terms:
        ka = a.shape[1]
        in_specs.append(pl.BlockSpec((tm, ka), lambda i, j: (i, 0)))
        in_specs.append(pl.BlockSpec((None, ka, tn), lambda i, j, layer=layer, rb=row_block: (layer, rb, j)))
        args += [a, w_stack]
    in_specs += [pl.BlockSpec((tm, tn), lambda i, j: (i, j)),
                 pl.BlockSpec((1, 1, 1, tn), lambda i, j: (row(i), g_idx, 0, j))]
    return pl.pallas_call(
        functools.partial(_mm_res_kernel, n_terms=len(terms)),
        grid=(m // tm, n // tn),
        in_specs=in_specs,
        out_specs=pl.BlockSpec((tm, tn), lambda i, j: (i, j)),
        out_shape=jax.ShapeDtypeStruct((m, n), F32),
        compiler_params=_cparams(("arbitrary", "arbitrary"), vmem),
        name="proj_residual",
    )(*args, x, mod)


def _ffn_up_kernel(*refs, dims, nblk, n_alias):
    a_ref = refs[0]
    b_refs = refs[1:1 + 2 * nblk]
    cw_refs = refs[1 + 2 * nblk:1 + 4 * nblk]
    cb_refs = refs[1 + 4 * nblk:1 + 6 * nblk]
    o_ref = refs[1 + 6 * nblk + n_alias]
    a = a_ref[...]
    tm = a.shape[0]
    tn = b_refs[0].shape[1]
    seq = jnp.where(pl.program_id(0) * tm < dims.n_ctx, dims.l_ctx, dims.l_lat)
    t = lax.broadcasted_iota(jnp.int32, (tm, tn), 0) & (seq - 1)
    first = t == 0
    last = t == seq - 1
    for q in range(nblk):
        g = _conv3_rows(_dot(a, b_refs[2 * q][...]), cw_refs[2 * q], cb_refs[2 * q], first, last)
        v = _conv3_rows(_dot(a, b_refs[2 * q + 1][...]), cw_refs[2 * q + 1], cb_refs[2 * q + 1],
                        first, last)
        o_ref[:, q * tn:(q + 1) * tn] = (_silu(g) * v).astype(o_ref.dtype)


def _ffn_up_call(h, w_up, layer, conv_w, conv_b, dims):
    m, k = h.shape
    dff = w_up.shape[2] // 2
    tm = dims.tm
    tn = FFN_TN
    nb = dff // tn
    conv_b = conv_b.reshape(1, -1)

    def run(first_block, steps, nblk, carried):
        cols = []
        for q in range(nblk):
            cols += [lambda i, j, q=q: (0, first_block + nblk * j + q),
                     lambda i, j, q=q: (0, nb + first_block + nblk * j + q)]
        in_specs = [pl.BlockSpec((tm, k), lambda i, j: (i, 0), pipeline_mode=pl.Buffered(1))]
        in_specs += [pl.BlockSpec((None, k, tn), lambda i, j, c=c: (layer,) + c(i, j)) for c in cols]
        in_specs += [pl.BlockSpec((3, tn), c) for c in cols]
        in_specs += [pl.BlockSpec((1, tn), c) for c in cols]
        n_in = len(in_specs)
        vmem = (tm * k * 2 + 2 * nblk * (2 * k * tn * 2 + tm * tn * 2) + 16 * tm * tn * 4 + (4 << 20))
        return pl.pallas_call(
            functools.partial(_ffn_up_kernel, dims=dims, nblk=nblk, n_alias=len(carried)),
            grid=(m // tm, steps),
            in_specs=in_specs + [_ANY_SPEC] * len(carried),
            out_specs=pl.BlockSpec((tm, nblk * tn), lambda i, j: (i, first_block // nblk + j)),
            out_shape=jax.ShapeDtypeStruct((m, dff), BF16),
            input_output_aliases={n_in + i: i for i in range(len(carried))},
            compiler_params=_cparams(("arbitrary", "arbitrary"), vmem),
            name="ffn_up_conv_gate",
        )(h, *([w_up] * (2 * nblk)), *([conv_w] * (2 * nblk)), *([conv_b] * (2 * nblk)), *carried)

    nblk = min(FFN_BLOCKS_PER_STEP, nb)
    main_steps = nb // nblk
    act = run(0, main_steps, nblk, ())
    if main_steps * nblk < nb:
        act = run(main_steps * nblk, nb - main_steps * nblk, 1, (act,))
    return act


def _rope(x, cos, sin_signed):
    lane = lax.broadcasted_iota(jnp.int32, x.shape, 1)
    n = x.shape[1]
    swapped = jnp.where((lane & 1) == 0, pltpu.roll(x, n - 1, 1), pltpu.roll(x, 1, 1))
    return x * cos + swapped * sin_signed


def _softmax_parts(s, sc):
    m = jnp.max(s, axis=-1, keepdims=True)
    if sc is not None:
        m = jnp.maximum(m, jnp.max(sc, axis=-1, keepdims=True))
    e = jnp.exp(s - m)
    tot = jnp.sum(e, axis=-1, keepdims=True)
    ec = None
    if sc is not None:
        ec = jnp.exp(sc - m)
        tot = tot + jnp.sum(ec, axis=-1, keepdims=True)
    return e, ec, 1.0 / tot


def _attn_kernel(*refs, latent, heads, tq, lq, lam_init, n_alias):
    if latent:
        q_ref, k_ref, v_ref, kc_ref, vc_ref, cos_ref, sin_ref, lv_ref, g_ref = refs[:9]
        o_ref, kb_s, vb_s, kcb_s, vcb_s = refs[9 + n_alias:]
    else:
        q_ref, k_ref, v_ref, lv_ref, g_ref = refs[:5]
        o_ref, ko_ref, vo_ref, kb_s, vb_s = refs[5 + n_alias:]
        ko_ref[0, 0] = k_ref[...]
        vo_ref[0, 0] = v_ref[...]
    lv = lv_ref[...]
    lam = (jnp.exp(jnp.sum(lv[0:1] * lv[1:2], axis=-1, keepdims=True))
           - jnp.exp(jnp.sum(lv[2:3] * lv[3:4], axis=-1, keepdims=True)) + lam_init)
    gain = g_ref[...] * (1.0 - lam_init)
    scale = QK_DIM ** -0.5
    lo = lax.broadcasted_iota(jnp.int32, (tq, V_DIM), 1) < QK_DIM

    for h in range(heads):
        cols = slice(h * V_DIM, (h + 1) * V_DIM)
        k = k_ref[:, cols]
        if latent:
            k = _rope(k, cos_ref[...], sin_ref[...])
            kcb_s[...] = kc_ref[0, 0, :, cols].astype(BF16)
            vcb_s[...] = vc_ref[0, 0, :, cols].astype(BF16)
        kb_s[...] = k.astype(BF16)
        vb_s[...] = v_ref[:, cols].astype(BF16)

        def q_block(i, carry, cols=cols):
            rows = pl.ds(0, tq) if lq == tq else pl.ds(pl.multiple_of(i * tq, tq), tq)
            q = q_ref[rows, cols]
            if latent:
                q = _rope(q, cos_ref[rows, :], sin_ref[rows, :])
            q = q * scale
            q1 = jnp.where(lo, q, 0.0).astype(BF16)
            q2 = jnp.where(lo, 0.0, q).astype(BF16)
            kb = kb_s[...]
            s1 = _dot_nt(q1, kb)
            s2 = _dot_nt(q2, kb)
            s1c = s2c = None
            if latent:
                kcb = kcb_s[...]
                s1c = _dot_nt(q1, kcb)
                s2c = _dot_nt(q2, kcb)
            e1, e1c, r1 = _softmax_parts(s1, s1c)
            e2, e2c, r2 = _softmax_parts(s2, s2c)
            r2 = r2 * lam
            o = _dot((e1 * r1 - e2 * r2).astype(BF16), vb_s[...])
            if latent:
                o = o + _dot((e1c * r1 - e2c * r2).astype(BF16), vcb_s[...])
            rr = lax.rsqrt(jnp.mean(o * o, axis=-1, keepdims=True) + NORM_EPS)
            o_ref[rows, cols] = (o * rr * gain).astype(o_ref.dtype)
            return carry

        if lq == tq:
            q_block(0, 0)
        else:
            lax.fori_loop(0, lq // tq, q_block, 0, unroll=2)


def _attn_call(p, lv, g, dims, lam_init, latent, layer, carried, cache_k=None, cache_v=None,
               cos=None, sin=None):
    n_heads = dims.att_heads
    if latent:
        nb, lq, heads, row_off = dims.b_lat, dims.l_lat, 1, dims.n_ctx // dims.l_lat
    else:
        nb, lq, heads, row_off = dims.b_ctx, dims.l_ctx, _pick(n_heads, (4, 2, 1)), 0
    tq = min(lq, ATT_TQ)
    hb = n_heads // heads
    w = heads * V_DIM
    q_spec = pl.BlockSpec((lq, w), lambda b, h: (b + row_off, h))
    k_spec = pl.BlockSpec((lq, w), lambda b, h: (b + row_off, hb + h))
    v_spec = pl.BlockSpec((lq, w), lambda b, h: (b + row_off, 2 * hb + h))
    small = [pl.BlockSpec((4, QK_DIM), lambda b, h: (0, 0)),
             pl.BlockSpec((1, V_DIM), lambda b, h: (0, 0))]
    scratch = [pltpu.VMEM((lq, V_DIM), BF16), pltpu.VMEM((lq, V_DIM), BF16)]
    if latent:
        past = cache_k.shape[2]
        c_spec = pl.BlockSpec((1, 1, past, w), lambda b, h: (b, layer, 0, h))
        t_spec = pl.BlockSpec((lq, V_DIM), lambda b, h: (0, 0))
        in_specs = [q_spec, k_spec, v_spec, c_spec, c_spec, t_spec, t_spec] + small
        args = (p, p, p, cache_k, cache_v, cos, sin, lv, g.reshape(1, V_DIM))
        scratch += [pltpu.VMEM((past, V_DIM), BF16), pltpu.VMEM((past, V_DIM), BF16)]
        lk = lq + past
    else:
        in_specs = [q_spec, k_spec, v_spec] + small
        args = (p, p, p, lv, g.reshape(1, V_DIM))
        lk = lq
    n_in = len(args)
    att_shape = jax.ShapeDtypeStruct((dims.n_ctx + dims.n_lat, n_heads * V_DIM), BF16)
    att_spec = pl.BlockSpec((lq, w), lambda b, h: (b + row_off, h))
    if latent:
        out_shape, out_specs = att_shape, att_spec
        aliases = {n_in: 0}
    else:
        kv_shape = jax.ShapeDtypeStruct((nb, dims.depth, lq, n_heads * V_DIM), F32)
        kv_spec = pl.BlockSpec((1, 1, lq, w), lambda b, h: (b, layer, 0, h))
        out_shape, out_specs = (att_shape, kv_shape, kv_shape), (att_spec, kv_spec, kv_spec)
        aliases = {n_in + i: 1 + i for i in range(len(carried))}
    vmem = 2 * 6 * lq * w * 4 + 12 * tq * lk * 4 + (8 << 20)
    return pl.pallas_call(
        functools.partial(_attn_kernel, latent=latent, heads=heads, tq=tq, lq=lq,
                          lam_init=lam_init, n_alias=len(carried)),
        grid=(nb, hb),
        in_specs=in_specs + [_ANY_SPEC] * len(carried),
        out_specs=out_specs,
        out_shape=out_shape,
        input_output_aliases=aliases,
        scratch_shapes=scratch,
        compiler_params=_cparams(("arbitrary", "arbitrary"), vmem),
        name="diff_attention_latent" if latent else "diff_attention_context",
    )(*args, *carried)


def _softplus(x):
    return jnp.maximum(x, 0.0) + jnp.log1p(jnp.exp(-jnp.abs(x)))


def _ssd_kernel(*refs, seq_len, latent, n_alias):
    it = iter(refs)
    x_ref, b_ref, c_ref, z_ref, dt_ref = (next(it) for _ in range(5))
    cwx_ref, cbx_ref, cwb_ref, cbb_ref, cwc_ref, cbc_ref = (next(it) for _ in range(6))
    par_ref, d_ref = next(it), next(it)
    if latent:
        sf0_ref, sb0_ref = next(it), next(it)
    for _ in range(n_alias):
        next(it)
    if latent:
        y_ref = next(it)
    else:
        y_ref, sf_ref, sb_ref = next(it), next(it), next(it)
    (xs_s, b_s, c_s, bt_s, cs_s, rcs_s, dtxf_s, dtxb_s, csxf_s, rcsxb_s,
     yf_s, yb_s, stf_s, stb_s) = it

    ch = SSD_CHUNK
    nc = seq_len // ch

    def conv_silu(u_ref, w_ref, bias_ref):
        u = u_ref[...]
        t = lax.broadcasted_iota(jnp.int32, u.shape, 0)
        return _silu(_conv3_rows(u, w_ref, bias_ref, t == 0, t == seq_len - 1))

    xs_s[...] = conv_silu(x_ref, cwx_ref, cbx_ref)
    bm = conv_silu(b_ref, cwb_ref, cbb_ref)
    b_s[...] = bm.astype(BF16)
    for c in range(nc):
        bt_s[c] = bm[c * ch:(c + 1) * ch, :].T.astype(BF16)
    c_s[...] = conv_silu(c_ref, cwc_ref, cbc_ref).astype(BF16)

    ri = lax.broadcasted_iota(jnp.int32, (ch, ch), 0)
    ci = lax.broadcasted_iota(jnp.int32, (ch, ch), 1)
    dt = _softplus(dt_ref[...] + par_ref[0:1, :])
    dta = dt * (-jnp.exp(par_ref[1:2, :]))
    lower = ri >= ci
    upper = ri <= ci
    t_low = jnp.where(lower, 1.0, 0.0).astype(BF16)
    t_up = jnp.where(upper, 1.0, 0.0).astype(BF16)
    for c in range(nc):
        blk = dta[c * ch:(c + 1) * ch, :]
        cs_s[c * ch:(c + 1) * ch, :] = _dot_f32_rhs(t_low, blk)
        rcs_s[c * ch:(c + 1) * ch, :] = _dot_f32_rhs(t_up, blk)
    er = lax.broadcasted_iota(jnp.int32, (LANE, SSD_GW), 0)
    eh = lax.broadcasted_iota(jnp.int32, (LANE, SSD_GW), 1) >> int(math.log2(SSD_P))
    exp_f = jnp.where(er == eh, 1.0, 0.0).astype(BF16)
    exp_b = jnp.where(er == eh + SSD_HPG, 1.0, 0.0).astype(BF16)
    dtxf_s[...] = _dot_f32_lhs(dt, exp_f, terms=2)
    dtxb_s[...] = _dot_f32_lhs(dt, exp_b, terms=2)
    csxf_s[...] = _dot_f32_lhs(cs_s[...], exp_f)
    rcsxb_s[...] = _dot_f32_lhs(rcs_s[...], exp_b)

    lo = lax.broadcasted_iota(jnp.int32, (ch, LANE), 1) < SSD_P

    def chunk_step(c, forward, y_buf, st_buf):
        rows = pl.ds(pl.multiple_of(c * ch, ch), ch)
        if forward:
            dtx, ex, cs, base, tri = dtxf_s[rows, :], csxf_s[rows, :], cs_s[rows, :], 0, lower
        else:
            dtx, ex, cs, base, tri = dtxb_s[rows, :], rcsxb_s[rows, :], rcs_s[rows, :], SSD_HPG, upper
        xdt = xs_s[rows, :] * dtx
        cc = c_s[rows, :]
        cb = _dot_nt(cc, b_s[rows, :])
        st = st_buf[...]
        y_off =_dot(cc, st.astype(BF16)) * jnp.exp(ex)
        cs_t = cs.T
        for j in range(SSD_GW // LANE):
            lanes = slice(j * LANE, (j + 1) * LANE)
            xp = xdt[:, lanes]
            acc = y_off[:, lanes]
            for k in range(LANE // SSD_P):
                hh = base + j * (LANE // SSD_P) + k
                diff = cs[:, hh:hh + 1] - cs_t[hh:hh + 1, :]
                decay = jnp.exp(jnp.where(tri, diff, -jnp.inf))
                g = (cb * decay).astype(BF16)
                xm = jnp.where(lo if k == 0 else jnp.logical_not(lo), xp, 0.0).astype(BF16)
                acc = acc + _dot(g, xm)
            y_buf[rows, lanes] = acc
        edge = ex[ch - 1:ch, :] if forward else ex[0:1, :]
        xd = (xdt * jnp.exp(edge - ex)).astype(BF16)
        st_buf[...] = jnp.exp(edge) * st + _dot(bt_s[c], xd)

    def both_directions(i, carry):
        chunk_step(i, True, yf_s, stf_s)
        chunk_step(nc - 1 - i, False, yb_s, stb_s)
        return carry

    stf_s[...] = sf0_ref[0, 0].T if latent else jnp.zeros(stf_s.shape, F32)
    stb_s[...] = sb0_ref[0, 0].T if latent else jnp.zeros(stb_s.shape, F32)
    lax.fori_loop(0, nc, both_directions, 0)
    if not latent:
        sf_ref[0, 0] = stf_s[...].T
        sb_ref[0, 0] = stb_s[...].T

    y = yf_s[...] + yb_s[...] + d_ref[...] * xs_s[...]
    y_ref[...] = y * _silu(z_ref[...])


def _ssd_call(p, lw, dims, latent, layer, carried, init_f=None, init_b=None):
    groups = dims.ssd_groups
    gw = SSD_GW
    if latent:
        nb, sl, row_off = dims.b_lat, dims.l_lat, dims.n_ctx // dims.l_lat
    else:
        nb, sl, row_off = dims.b_ctx, dims.l_ctx, 0
    nc = sl // SSD_CHUNK
    o = dims.offs
    zb, xb, bb, cb, db = o.z // gw, o.x // gw, o.b // LANE, o.c // LANE, o.dt // LANE
    in_specs = [
        pl.BlockSpec((sl, gw), lambda b, g: (b + row_off, xb + g)),
        pl.BlockSpec((sl, LANE), lambda b, g: (b + row_off, bb + g)),
        pl.BlockSpec((sl, LANE), lambda b, g: (b + row_off, cb + g)),
        pl.BlockSpec((sl, gw), lambda b, g: (b + row_off, zb + g)),
        pl.BlockSpec((sl, LANE), lambda b, g: (b + row_off, db + g)),
        pl.BlockSpec((3, gw), lambda b, g: (0, g)),
        pl.BlockSpec((1, gw), lambda b, g: (0, g)),
        pl.BlockSpec((3, LANE), lambda b, g: (0, dims.ssd_width // LANE + g)),
        pl.BlockSpec((1, LANE), lambda b, g: (0, dims.ssd_width // LANE + g)),
        pl.BlockSpec((3, LANE), lambda b, g: (0, dims.ssd_width // LANE + groups + g)),
        pl.BlockSpec((1, LANE), lambda b, g: (0, dims.ssd_width // LANE + groups + g)),
        pl.BlockSpec((None, 8, LANE), lambda b, g: (layer, 0, g)),
        pl.BlockSpec((1, gw), lambda b, g: (0, g)),
    ]
    args = [p, p, p, p, p, lw["ssd_conv_w"], lw["ssd_conv_b"], lw["ssd_conv_w"], lw["ssd_conv_b"],
            lw["ssd_conv_w"], lw["ssd_conv_b"], lw["ssd_par"], lw["ssd_d"]]
    y_shape = jax.ShapeDtypeStruct((dims.n_ctx + dims.n_lat, dims.ssd_width), F32)
    y_spec = pl.BlockSpec((sl, gw), lambda b, g: (b + row_off, g))
    s_spec = pl.BlockSpec((1, 1, gw, SSD_N), lambda b, g: (b, layer, g, 0))
    if latent:
        in_specs += [s_spec, s_spec]
        args += [init_f, init_b]
        out_shape, out_specs = y_shape, y_spec
        aliases = {len(args): 0}
    else:
        s_shape = jax.ShapeDtypeStruct((nb, dims.depth, dims.ssd_width, SSD_N), F32)
        out_shape, out_specs = (y_shape, s_shape, s_shape), (y_spec, s_spec, s_spec)
        aliases = {len(args) + i: 1 + i for i in range(len(carried))}
    scratch = [
        pltpu.VMEM((sl, gw), F32), pltpu.VMEM((sl, LANE), BF16), pltpu.VMEM((sl, LANE), BF16),
        pltpu.VMEM((nc, SSD_CHUNK, SSD_CHUNK), BF16),
        pltpu.VMEM((sl, LANE), F32), pltpu.VMEM((sl, LANE), F32),
        pltpu.VMEM((sl, gw), F32), pltpu.VMEM((sl, gw), F32),
        pltpu.VMEM((sl, gw), F32), pltpu.VMEM((sl, gw), F32),
        pltpu.VMEM((sl, gw), F32), pltpu.VMEM((sl, gw), F32),
        pltpu.VMEM((SSD_N, gw), F32), pltpu.VMEM((SSD_N, gw), F32),
    ]
    vmem = 2 * sl * (3 * gw + 3 * LANE) * 4 + sl * (8 * gw + 4 * LANE) * 4 + 8 * sl * gw * 4 + (8 << 20)
    return pl.pallas_call(
        functools.partial(_ssd_kernel, seq_len=sl, latent=latent, n_alias=len(carried)),
        grid=(nb, groups),
        in_specs=in_specs + [_ANY_SPEC] * len(carried),
        out_specs=out_specs,
        out_shape=out_shape,
        input_output_aliases=aliases,
        scratch_shapes=scratch,
        compiler_params=_cparams(("arbitrary", "arbitrary"), vmem),
        name="ssd_latent" if latent else "ssd_context",
    )(*args, *carried)


def _dft_mats(l):
    n = 3 * l // 2
    nh = n // 2
    k = np.arange(nh, dtype=np.int64)[:, None]
    t = np.arange(l, dtype=np.int64)[None, :]
    ang = 2.0 * np.pi * ((k * t) % n).astype(np.float64) / n
    fwd = np.concatenate([np.cos(ang), -np.sin(ang)], axis=0)
    fwd[nh, :] = np.where(np.arange(l) % 2 == 0, 1.0, -1.0)
    pos = (np.arange(l, dtype=np.int64) + l // 2)[:, None]
    kk = np.arange(nh, dtype=np.int64)[None, :]
    ang2 = 2.0 * np.pi * ((pos * kk) % n).astype(np.float64) / n
    inv = np.concatenate([2.0 * np.cos(ang2), -2.0 * np.sin(ang2)], axis=1) / n
    inv[:, 0] = 1.0 / n
    inv[:, nh] = np.where(pos[:, 0] % 2 == 0, 1.0, -1.0) / n

    def split(m):
        hi = m.astype(np.float32).astype(BF16)
        lo = (m - hi.astype(np.float64)).astype(np.float32).astype(BF16)
        return hi, lo

    return split(fwd) + split(inv)


def _hy_feats(l):
    off = jnp.arange(l, dtype=F32) - (l // 2)
    band = HY_FREQ_BASE ** (-jnp.arange(HY_BANDS, dtype=F32) / HY_BANDS)
    ang = off[:, None] * band
    feats = jnp.concatenate([off[:, None] / l, jnp.sin(ang), jnp.cos(ang)], axis=-1)
    return jnp.pad(feats, ((0, 0), (0, LANE - feats.shape[1])))


def _hy_filter_kernel(feats_ref, w1_ref, b1_ref, w2_ref, b2_ref, fr_ref, w3_ref, dl_ref,
                      fh_ref, fl_ref, o_ref, *, seq_len):
    fr = fr_ref[0]
    h1 = jnp.sin(fr[0:1] * (_dot_hp(feats_ref[...], w1_ref[0]) + b1_ref[0]))
    h2 = jnp.sin(fr[1:2] * (_dot_hp(h1, w2_ref[0]) + b2_ref[0]))
    filt = _dot_hp(h2, w3_ref[0])
    off = lax.broadcasted_iota(jnp.int32, filt.shape, 0).astype(F32) - (seq_len // 2)
    dist = jnp.abs(off) * (2.0 / seq_len)
    h = filt * jnp.exp(-dist * dl_ref[...])
    hh, hl = _split_bf16(h, 2)
    fh = fh_ref[...]
    o_ref[0, 0] = _dot(fh, hh) + _dot(fh, hl) + _dot(fl_ref[...], hh)


def _hy_filter_call(seq_len, hw, mats, dims):
    depth, hy = dims.depth, dims.hy_width
    cb = HY_CB
    ncb = hy // cb
    fh, fl = mats[0], mats[1]
    n = fh.shape[0]
    feats = _hy_feats(seq_len)
    deltas = jnp.linspace(HY_MIN_DECAY, HY_MAX_DECAY, hy, dtype=F32).reshape(1, hy)
    vmem = 2 * 2 * (n * seq_len * 2) + 6 * n * cb * 4 + (8 << 20)
    return pl.pallas_call(
        functools.partial(_hy_filter_kernel, seq_len=seq_len),
        grid=(depth, 2, ncb),
        in_specs=[
            pl.BlockSpec((seq_len, LANE), lambda l, o, c: (0, 0)),
            pl.BlockSpec((1, LANE, LANE), lambda l, o, c: (l, 0, 0)),
            pl.BlockSpec((1, 1, LANE), lambda l, o, c: (l, 0, 0)),
            pl.BlockSpec((1, LANE, LANE), lambda l, o, c: (l, 0, 0)),
            pl.BlockSpec((1, 1, LANE), lambda l, o, c: (l, 0, 0)),
            pl.BlockSpec((1, 2, LANE), lambda l, o, c: (l, 0, 0)),
            pl.BlockSpec((1, LANE, cb), lambda l, o, c: (l, 0, o * ncb + c)),
            pl.BlockSpec((1, cb), lambda l, o, c: (0, c)),
            pl.BlockSpec((n, seq_len), lambda l, o, c: (0, 0)),
            pl.BlockSpec((n, seq_len), lambda l, o, c: (0, 0)),
        ],
        out_specs=pl.BlockSpec((1, 1, n, cb), lambda l, o, c: (l, o, 0, c)),
        out_shape=jax.ShapeDtypeStruct((depth, 2, n, hy), F32),
        compiler_params=_cparams(("arbitrary",) * 3, vmem),
        name="hyena_filter_spectrum",
    )(feats, hw["w1"], hw["b1"], hw["w2"], hw["b2"], hw["freq"], hw["w3"], deltas, fh, fl)


def _hy_conv_kernel(*refs, seq_len, n_seq, n_alias):
    (x1_ref, x2_ref, v_ref, cw1_ref, cb1_ref, cw2_ref, cb2_ref, cwv_ref, cbv_ref,
     hf_ref, bias_ref, fh_ref, fl_ref, gh_ref, gl_ref) = refs[:15]
    o_ref = refs[15 + n_alias]
    nh = fh_ref.shape[0] // 2
    cb = o_ref.shape[1]
    t = lax.broadcasted_iota(jnp.int32, (seq_len, cb), 0)
    first = t == 0
    last = t == seq_len - 1
    bin0 = lax.broadcasted_iota(jnp.int32, (nh, cb), 0) == 0
    fh = fh_ref[...]
    gh = gh_ref[...]
    rows = [slice(s * seq_len, (s + 1) * seq_len) for s in range(n_seq)]
    gates = [(_conv3_rows(x1_ref[r, :], cw1_ref, cb1_ref, first, last),
              _conv3_rows(x2_ref[r, :], cw2_ref, cb2_ref, first, last)) for r in rows]
    zz = [_conv3_rows(v_ref[r, :], cwv_ref, cbv_ref, first, last) for r in rows]
    for o in range(2):
        hr, hi = hf_ref[0, o, :nh, :], hf_ref[0, o, nh:, :]
        uf = []
        for z in zz:
            zh, zl = _split_bf16(z, 2)
            uf.append(_dot(fh, zh) + _dot(fh, zl) + _dot(fl_ref[...], zh))
        ys = []
        for u in uf:
            ur, ui = u[:nh], u[nh:]
            ii = ui * hi
            yr = ur * hr - jnp.where(bin0, 0.0, ii)
            yi = jnp.where(bin0, ii, ur * hi + ui * hr)
            yh, yl = _split_bf16(jnp.concatenate([yr, yi], axis=0), 2)
            ys.append(_dot(gh, yh) + _dot(gh, yl) + _dot(gl_ref[...], yh))
        zz = [g[o] * (y + z * bias_ref[o:o + 1, :]) for g, y, z in zip(gates, ys, zz)]
    for r, z in zip(rows, zz):
        o_ref[r, :] = z


def _hy_conv_call(p, hf, lw, mats, dims, layer, latent, carried):
    hy = dims.hy_width
    cb = HY_CB
    ncb = hy // cb
    if latent:
        nb, sl, row_start = dims.b_lat, dims.l_lat, dims.n_ctx
    else:
        nb, sl, row_start = dims.b_ctx, dims.l_ctx, 0
    n_seq = max(n for n in (8, 4, 2, 1) if nb % n == 0 and (n == 1 or n * sl <= HY_ROWS_PER_STEP))
    rows = n_seq * sl
    row_off = row_start // rows
    n = mats[0].shape[0]
    const = lambda c, b: (0, 0)
    in_specs = [pl.BlockSpec((rows, cb), lambda c, b, k=k: (b + row_off, k * ncb + c)) for k in range(3)]
    for k in range(3):
        in_specs += [pl.BlockSpec((3, cb), lambda c, b, k=k: (0, k * ncb + c)),
                     pl.BlockSpec((1, cb), lambda c, b, k=k: (0, k * ncb + c))]
    once = pl.Buffered(1)
    in_specs += [
        pl.BlockSpec((1, 2, n, cb), lambda c, b: (layer, 0, 0, c), pipeline_mode=once),
        pl.BlockSpec((2, cb), lambda c, b: (0, c)),
        pl.BlockSpec((n, sl), const, pipeline_mode=once),
        pl.BlockSpec((n, sl), const, pipeline_mode=once),
        pl.BlockSpec((sl, n), const, pipeline_mode=once),
        pl.BlockSpec((sl, n), const, pipeline_mode=once),
    ]
    cw, cbias = lw["hy_conv_w"], lw["hy_conv_b"]
    vmem = (4 * (n * sl * 2) + 2 * n * cb * 4 + 2 * 4 * rows * cb * 4
            + n_seq * (6 * n + 8 * sl) * cb * 4 + (8 << 20))
    return pl.pallas_call(
        functools.partial(_hy_conv_kernel, seq_len=sl, n_seq=n_seq, n_alias=len(carried)),
        grid=(ncb, nb // n_seq),
        in_specs=in_specs + [_ANY_SPEC] * len(carried),
        out_specs=pl.BlockSpec((rows, cb), lambda c, b: (b + row_off, c)),
        out_shape=jax.ShapeDtypeStruct((dims.n_ctx + dims.n_lat, hy), F32),
        input_output_aliases={15 + i: i for i in range(len(carried))},
        compiler_params=_cparams(("arbitrary", "arbitrary"), vmem),
        name="hyena_conv_latent" if latent else "hyena_conv_context",
    )(p, p, p, cw, cbias, cw, cbias, cw, cbias, hf, lw["hy_bias"], *mats, *carried)


class _Offsets:
    def __init__(self, att_w, ssd_w, groups, hy_w):
        self.q, self.k, self.v = 0, att_w, 2 * att_w
        self.z = 3 * att_w
        self.x = self.z + ssd_w
        self.b = self.x + ssd_w
        self.c = self.b + groups * SSD_N
        self.dt = self.c + groups * SSD_N
        used = self.dt + groups * LANE
        self.total = PROJ_COL_ALIGN * ((used + PROJ_COL_ALIGN - 1) // PROJ_COL_ALIGN)


class _Dims:
    pass


def _make_dims(x_prompt, x_sample, cache_k, state_ssd_fwd, w_in, hy_norm_g, ffn_w_down):
    d = _Dims()
    d.b_ctx, d.l_ctx, d.d_model = x_prompt.shape
    d.b_lat, d.l_lat, _ = x_sample.shape
    d.n_ctx = d.b_ctx * d.l_ctx
    d.n_lat = d.b_lat * d.l_lat
    d.depth = w_in.shape[0]
    d.att_heads = cache_k.shape[3]
    d.att_width = d.att_heads * V_DIM
    d.ssd_heads = state_ssd_fwd.shape[2]
    d.ssd_width = d.ssd_heads * SSD_P
    d.ssd_groups = d.ssd_heads // SSD_HPG
    d.hy_width = hy_norm_g.shape[-1]
    d.d_ff = ffn_w_down.shape[1]
    d.offs = _Offsets(d.att_width, d.ssd_width, d.ssd_groups, d.hy_width)
    d.tm = d.l_lat
    assert d.l_lat % d.l_ctx == 0 and d.n_ctx % d.tm == 0 and d.l_ctx % 256 == 0
    assert d.l_lat & (d.l_lat - 1) == 0 and d.l_ctx & (d.l_ctx - 1) == 0
    assert d.hy_width % HY_CB == 0 and d.d_ff % FFN_TN == 0
    assert d.ssd_heads % SSD_HPG == 0
    return d


def _rope_tables(seq_len):
    rows = seq_len // GRID_W
    r, col = jnp.meshgrid(jnp.arange(rows), jnp.arange(GRID_W), indexing="ij")
    npair = QK_DIM // 4
    inv = ROPE_BASE ** (-jnp.arange(npair, dtype=F32) / npair)
    ang = jnp.concatenate([r.reshape(-1, 1).astype(F32) * inv, col.reshape(-1, 1).astype(F32) * inv], axis=-1)
    cos, sin = jnp.cos(ang), jnp.sin(ang)
    cos2 = jnp.repeat(cos, 2, axis=-1)
    sin2 = jnp.stack([-sin, sin], axis=-1).reshape(seq_len, QK_DIM)
    return jnp.tile(cos2, (1, 2)), jnp.tile(sin2, (1, 2))


def kernel(x_prompt, x_sample, cache_k, cache_v, state_ssd_fwd, state_ssd_bwd, c, c_ctx, norm1_g, norm2_g, w_ada, b_ada, w_in, att_lambda, att_subln_g, ssd_conv_w, ssd_conv_b, ssd_dt_bias, ssd_a_log, ssd_d, ssd_norm_g, hy_conv_w, hy_conv_b, hy_w1, hy_b1, hy_w2, hy_b2, hy_freq, hy_w3, hy_bias, hy_norm_g, w_out, ffn_w_up, ffn_conv_w, ffn_conv_b, ffn_w_down, final_norm_g):
    dims = _make_dims(x_prompt, x_sample, cache_k, state_ssd_fwd, w_in, hy_norm_g, ffn_w_down)
    d, depth, offs = dims.d_model, dims.depth, dims.offs
    n_ctx = dims.n_ctx

    mod_rows = 16 * ((1 + dims.b_lat + 15) // 16)
    cond = jnp.zeros((mod_rows, d), F32).at[0].set(c_ctx).at[1:1 + dims.b_lat].set(c)
    mod_all = _ada_call(cond, w_ada, b_ada).reshape(depth, mod_rows, 6, 1, d)

    pad_h = LANE - hy_w1.shape[-1]
    hw = {
        "w1": jnp.pad(hy_w1, ((0, 0), (0, LANE - hy_w1.shape[1]), (0, pad_h))),
        "b1": jnp.pad(hy_b1, ((0, 0), (0, pad_h)))[:, None, :],
        "w2": jnp.pad(hy_w2, ((0, 0), (0, pad_h), (0, pad_h))),
        "b2": jnp.pad(hy_b2, ((0, 0), (0, pad_h)))[:, None, :],
        "freq": jnp.pad(hy_freq, ((0, 0), (0, 0), (0, pad_h))),
        "w3": jnp.pad(hy_w3, ((0, 0), (0, pad_h), (0, 0))),
    }
    mats_ctx = _dft_mats(dims.l_ctx)
    mats_lat = _dft_mats(dims.l_lat)
    hf_ctx = _hy_filter_call(dims.l_ctx, hw, mats_ctx, dims)
    hf_lat = _hy_filter_call(dims.l_lat, hw, mats_lat, dims)

    cos_t, sin_t = _rope_tables(dims.l_lat)
    ck = cache_k.reshape(dims.b_lat, depth, cache_k.shape[2], dims.att_width)
    cv = cache_v.reshape(dims.b_lat, depth, cache_v.shape[2], dims.att_width)
    sf0 = state_ssd_fwd.reshape(dims.b_lat, depth, dims.ssd_width, SSD_N)
    sb0 = state_ssd_bwd.reshape(dims.b_lat, depth, dims.ssd_width, SSD_N)

    x = jnp.concatenate([x_prompt.reshape(n_ctx, d), x_sample.reshape(dims.n_lat, d)], axis=0)
    new_kv, new_st = (), ()

    dt0, dt_cols = offs.dt, 2 * dims.ssd_heads
    pick = np.zeros((dt_cols, dims.ssd_groups * LANE), np.float32)
    for g in range(dims.ssd_groups):
        for direction in range(2):
            for hh in range(SSD_HPG):
                pick[direction * dims.ssd_heads + g * SSD_HPG + hh,
                     g * LANE + direction * SSD_HPG + hh] = 1.0
    regroup = lambda t: jnp.einsum("...k,kn->...n", t, pick, precision=lax.Precision.HIGHEST)
    w_dt = regroup(w_in[:, :, dt0:dt0 + dt_cols])
    pad_cols = offs.total - dt0 - w_dt.shape[2]
    w_proj = jnp.concatenate([w_in[:, :, :dt0].astype(BF16), w_dt.astype(BF16),
                              jnp.zeros((depth, d, pad_cols), BF16)], axis=2)
    w_hy = w_in[:, :, dt0 + dt_cols:].astype(BF16)
    ssd_par = jnp.zeros((depth, 8, dims.ssd_groups * LANE), F32)
    ssd_par = ssd_par.at[:, 0].set(regroup(ssd_dt_bias.reshape(depth, dt_cols)))
    ssd_par = ssd_par.at[:, 1].set(regroup(ssd_a_log.reshape(depth, dt_cols)))
    w_o = w_out.astype(BF16)
    w_up = ffn_w_up.astype(BF16)
    w_down = ffn_w_down.astype(BF16)
    a_w, s_w = dims.att_width, dims.ssd_width
    if a_w % s_w == 0 and (a_w + s_w) % dims.hy_width == 0:
        out_terms = lambda l, att, y_ssd, y_hy: [
            (att, w_o, l, 0), (y_ssd, w_o, l, a_w // s_w), (y_hy, w_o, l, (a_w + s_w) // dims.hy_width)]
    else:
        out_terms = lambda l, att, y_ssd, y_hy: [
            (att, w_o[l:l + 1, :a_w], 0, 0), (y_ssd, w_o[l:l + 1, a_w:a_w + s_w], 0, 0),
            (y_hy, w_o[l:l + 1, a_w + s_w:], 0, 0)]

    for l in range(depth):
        lam_init = 0.8 - 0.6 * math.exp(-0.3 * l)
        mod = mod_all[l]
        lw = {
            "ssd_conv_w": ssd_conv_w[l], "ssd_conv_b": ssd_conv_b[l].reshape(1, -1),
            "ssd_par": ssd_par,
            "ssd_d": jnp.repeat(ssd_d[l], SSD_P).reshape(1, -1),
            "hy_conv_w": hy_conv_w[l], "hy_conv_b": hy_conv_b[l].reshape(1, -1), "hy_bias": hy_bias[l],
        }

        h = _norm_mod_call(x, norm1_g[l], mod, 0, 1, dims)
        p = _mm_call(h, w_proj, l, dims)
        p_hy = _mm_call(h, w_hy, l, dims)

        att, new_k, new_v = _attn_call(p, att_lambda[l], att_subln_g[l], dims, lam_init, False, l, new_kv)
        new_kv = (new_k, new_v)
        att = _attn_call(p, att_lambda[l], att_subln_g[l], dims, lam_init, True, l, (att,),
                         cache_k=ck, cache_v=cv, cos=cos_t, sin=sin_t)
        ys, new_sf, new_sb = _ssd_call(p, lw, dims, False, l, new_st)
        new_st = (new_sf, new_sb)
        ys = _ssd_call(p, lw, dims, True, l, (ys,), init_f=sf0, init_b=sb0)
        zz = _hy_conv_call(p_hy, hf_ctx, lw, mats_ctx, dims, l, False, ())
        zz = _hy_conv_call(p_hy, hf_lat, lw, mats_lat, dims, l, True, (zz,))

        y_ssd = _norm_call(ys, ssd_norm_g[l], BF16, name="ssd_norm")
        y_hy = _norm_call(zz, hy_norm_g[l], BF16, name="hyena_norm")
        x = _mm_res_call(out_terms(l, att, y_ssd, y_hy), x, mod, 2, dims,
                         (1024, 768, 512, 384, 256, 128))

        h2 = _norm_mod_call(x, norm2_g[l], mod, 3, 4, dims)
        act = _ffn_up_call(h2, w_up, l, ffn_conv_w[l], ffn_conv_b[l], dims)
        x = _mm_res_call([(act, w_down, l, 0)], x, mod, 5, dims, (DOWN_TILE, 256, 128),
                         tm=min(DOWN_TILE, dims.tm))

    y_prompt = _norm_call(x, final_norm_g, F32, 0, n_ctx, name="final_norm").reshape(x_prompt.shape)
    y_sample = _norm_call(x, final_norm_g, F32, n_ctx, dims.n_lat, name="final_norm").reshape(x_sample.shape)
    kv_shape = (dims.b_ctx, depth, dims.l_ctx, dims.att_heads, V_DIM)
    st_shape = (dims.b_ctx, depth, dims.ssd_heads, SSD_P, SSD_N)
    return (y_prompt, y_sample, new_kv[0].reshape(kv_shape), new_kv[1].reshape(kv_shape),
            new_st[0].reshape(st_shape), new_st[1].reshape(st_shape))
```

```python
import functools
import math

import numpy as np
import jax
import jax.numpy as jnp
from jax import lax
from jax.experimental import pallas as pl
from jax.experimental.pallas import tpu as pltpu

F32 = jnp.float32
BF16 = jnp.bfloat16

LANE = 128
V7X_VMEM_BYTES = 64 * 1024 * 1024
VMEM_CAP = V7X_VMEM_BYTES - 6 * 1024 * 1024

NORM_EPS = 1e-6
GRID_W = 64
ROPE_BASE = 10000.0
QK_DIM = 64
V_DIM = 128
SSD_P = 64
SSD_N = 128
SSD_CHUNK = 128
SSD_HPG = 6
SSD_GW = SSD_HPG * SSD_P
HY_BANDS = 16
HY_FREQ_BASE = 10000.0
HY_MIN_DECAY = -math.log(1e-2) / 1.5
HY_MAX_DECAY = -math.log(1e-2) / 0.3
HY_CB = 256
FFN_TN = 256
FFN_BLOCKS_PER_STEP = 3
PROJ_COL_ALIGN = 1024
ATT_TQ = 1024
DOWN_TILE = 512
SSD_PAIR_MAX_ROWS = 512
HY_ROWS_PER_STEP = 2048

_ANY_SPEC = pl.BlockSpec(memory_space=pl.ANY)


def _cparams(sem, vmem_bytes):
    limit = int(min(VMEM_CAP, max(vmem_bytes, 16 * 1024 * 1024)))
    return pltpu.CompilerParams(dimension_semantics=sem, vmem_limit_bytes=limit)


def _pick(n, candidates):
    for c in candidates:
        if n % c == 0:
            return c
    raise ValueError(f"no block size in {candidates} divides {n}")


def _dot(a, b):
    return jnp.dot(a, b, preferred_element_type=F32)


def _dot_nt(a, b):
    return lax.dot_general(a, b, (((1,), (1,)), ((), ())), preferred_element_type=F32)


def _split_bf16(x, n):
    parts = []
    r = x
    for _ in range(n):
        h = r.astype(BF16)
        parts.append(h)
        r = r - h.astype(F32)
    return parts


def _dot_f32_lhs(a, b_bf16, terms=3):
    return sum(_dot(p, b_bf16) for p in _split_bf16(a, terms))


def _dot_f32_rhs(a_bf16, b):
    return sum(_dot(a_bf16, p) for p in _split_bf16(b, 3))


def _dot_hp(a, b):
    ah, al = _split_bf16(a, 2)
    bh, bl = _split_bf16(b, 2)
    return _dot(ah, bh) + _dot(ah, bl) + _dot(al, bh)


def _silu(x):
    return x * jax.nn.sigmoid(x)


def _conv3_rows(u, w_ref, b_ref, first, last):
    n = u.shape[0]
    prev = jnp.where(first, 0.0, pltpu.roll(u, 1, 0))
    nxt = jnp.where(last, 0.0, pltpu.roll(u, n - 1, 0))
    return prev * w_ref[0:1, :] + b_ref[...] + u * w_ref[1:2, :] + nxt * w_ref[2:3, :]


def _ada_kernel(c_ref, w_ref, b_ref, o_ref):
    s = _silu(c_ref[...]).astype(BF16)
    o_ref[0] = _dot(s, w_ref[0].astype(BF16)) + b_ref[0]


def _ada_call(cond, w_ada, b_ada):
    depth, d, n = w_ada.shape
    rows = cond.shape[0]
    tn = _pick(n, (512, 256, 128))
    vmem = 2 * (d * tn * 4) + d * tn * 2 + 4 * rows * (d + 2 * tn) * 4 + (4 << 20)
    return pl.pallas_call(
        _ada_kernel,
        grid=(depth, n // tn),
        in_specs=[
            pl.BlockSpec((rows, d), lambda l, j: (0, 0)),
            pl.BlockSpec((1, d, tn), lambda l, j: (l, 0, j)),
            pl.BlockSpec((1, 1, tn), lambda l, j: (l, 0, j)),
        ],
        out_specs=pl.BlockSpec((1, rows, tn), lambda l, j: (l, 0, j)),
        out_shape=jax.ShapeDtypeStruct((depth, rows, n), F32),
        compiler_params=_cparams(("arbitrary", "arbitrary"), vmem),
        name="ada_mod",
    )(cond, w_ada, b_ada.reshape(depth, 1, n))


def _norm_mod_kernel(x_ref, g_ref, sc_ref, sh_ref, o_ref):
    x = x_ref[...]
    r = lax.rsqrt(jnp.mean(x * x, axis=-1, keepdims=True) + NORM_EPS)
    y = x * r * g_ref[...]
    o_ref[...] = (y * (1.0 + sc_ref[0, 0]) + sh_ref[0, 0]).astype(o_ref.dtype)


def _norm_kernel(x_ref, g_ref, o_ref):
    x = x_ref[...]
    r = lax.rsqrt(jnp.mean(x * x, axis=-1, keepdims=True) + NORM_EPS)
    o_ref[...] = (x * r * g_ref[...]).astype(o_ref.dtype)


def _mod_row(i, tm, dims):
    start = i * tm
    lat = jnp.maximum(start - dims.n_ctx, 0) // dims.l_lat
    return jnp.where(start >= dims.n_ctx, lat + 1, 0)


def _norm_mod_call(x, g, mod, sh_idx, sc_idx, dims):
    m, d = x.shape
    tr = 256
    row = functools.partial(_mod_row, tm=tr, dims=dims)
    vmem = 2 * tr * d * (4 + 2) + 3 * tr * d * 4 + (4 << 20)
    return pl.pallas_call(
        _norm_mod_kernel,
        grid=(m // tr,),
        in_specs=[
            pl.BlockSpec((tr, d), lambda i: (i, 0)),
            pl.BlockSpec((1, d), lambda i: (0, 0)),
            pl.BlockSpec((1, 1, 1, d), lambda i: (row(i), sc_idx, 0, 0)),
            pl.BlockSpec((1, 1, 1, d), lambda i: (row(i), sh_idx, 0, 0)),
        ],
        out_specs=pl.BlockSpec((tr, d), lambda i: (i, 0)),
        out_shape=jax.ShapeDtypeStruct((m, d), BF16),
        compiler_params=_cparams(("arbitrary",), vmem),
        name="norm_mod",
    )(x, g.reshape(1, d), mod, mod)


def _norm_call(x, g, out_dtype, row_start=0, rows=None, name="norm"):
    m, d = x.shape
    rows = m if rows is None else rows
    tr = 256
    off = row_start // tr
    vmem = 2 * tr * d * 8 + 3 * tr * d * 4 + (4 << 20)
    return pl.pallas_call(
        _norm_kernel,
        grid=(rows // tr,),
        in_specs=[
            pl.BlockSpec((tr, d), lambda i: (i + off, 0)),
            pl.BlockSpec((1, d), lambda i: (0, 0)),
        ],
        out_specs=pl.BlockSpec((tr, d), lambda i: (i, 0)),
        out_shape=jax.ShapeDtypeStruct((rows, d), out_dtype),
        compiler_params=_cparams(("arbitrary",), vmem),
        name=name,
    )(x, g.reshape(1, d))


def _mm_kernel(a_ref, b_ref, o_ref):
    o_ref[...] = _dot(a_ref[...], b_ref[...]).astype(o_ref.dtype)


def _mm_res_kernel(*refs, n_terms):
    x_ref, g_ref, o_ref = refs[2 * n_terms:]
    acc = _dot(refs[0][...], refs[1][...])
    for i in range(1, n_terms):
        acc = acc + _dot(refs[2 * i][...], refs[2 * i + 1][...])
    o_ref[...] = x_ref[...] + g_ref[0, 0] * acc


def _mm_call(a, w_stack, layer, dims, out_dtype=F32):
    m, k = a.shape
    n = w_stack.shape[2]
    b = w_stack
    tm = dims.tm
    tn = _pick(n, (1024, 768, 512, 384, 256, 128))
    vmem = 2 * (tm * k * 2 + k * tn * 2 + tm * tn * 4) + tm * tn * 4 + (4 << 20)
    return pl.pallas_call(
        _mm_kernel,
        grid=(m // tm, n // tn),
        in_specs=[
            pl.BlockSpec((tm, k), lambda i, j: (i, 0)),
            pl.BlockSpec((None, k, tn), lambda i, j: (layer, 0, j)),
        ],
        out_specs=pl.BlockSpec((tm, tn), lambda i, j: (i, j)),
        out_shape=jax.ShapeDtypeStruct((m, n), out_dtype),
        compiler_params=_cparams(("arbitrary", "arbitrary"), vmem),
        name="proj",
    )(a, b)


def _mm_res_call(terms, x, mod, g_idx, dims, tn_candidates, tm=None):
    m, n = x.shape
    k = sum(t[0].shape[1] for t in terms)
    tm = dims.tm if tm is None else tm
    tn = _pick(n, tn_candidates)
    row = functools.partial(_mod_row, tm=tm, dims=dims)
    vmem = 2 * (tm * k * 2 + k * tn * 2 + 2 * tm * tn * 4) + tm * tn * 4 + (4 << 20)
    in_specs, args = [], []
    for a, w_stack, layer, row_block in terms:
        ka = a.shape[1]
        in_specs.append(pl.BlockSpec((tm, ka), lambda i, j: (i, 0)))
        in_specs.append(pl.BlockSpec((None, ka, tn), lambda i, j, layer=layer, rb=row_block: (layer, rb, j)))
        args += [a, w_stack]
    in_specs += [pl.BlockSpec((tm, tn), lambda i, j: (i, j)),
                 pl.BlockSpec((1, 1, 1, tn), lambda i, j: (row(i), g_idx, 0, j))]
    return pl.pallas_call(
        functools.partial(_mm_res_kernel, n_terms=len(terms)),
        grid=(m // tm, n // tn),
        in_specs=in_specs,
        out_specs=pl.BlockSpec((tm, tn), lambda i, j: (i, j)),
        out_shape=jax.ShapeDtypeStruct((m, n), F32),
        compiler_params=_cparams(("arbitrary", "arbitrary"), vmem),
        name="proj_residual",
    )(*args, x, mod)


def _ffn_up_kernel(*refs, dims, nblk, n_alias):
    a_ref = refs[0]
    b_refs = refs[1:1 + 2 * nblk]
    cw_refs = refs[1 + 2 * nblk:1 + 4 * nblk]
    cb_refs = refs[1 + 4 * nblk:1 + 6 * nblk]
    o_ref = refs[1 + 6 * nblk + n_alias]
    a = a_ref[...]
    tm = a.shape[0]
    tn = b_refs[0].shape[1]
    seq = jnp.where(pl.program_id(0) * tm < dims.n_ctx, dims.l_ctx, dims.l_lat)
    t = lax.broadcasted_iota(jnp.int32, (tm, tn), 0) & (seq - 1)
    first = t == 0
    last = t == seq - 1
    for q in range(nblk):
        g = _conv3_rows(_dot(a, b_refs[2 * q][...]), cw_refs[2 * q], cb_refs[2 * q], first, last)
        v = _conv3_rows(_dot(a, b_refs[2 * q + 1][...]), cw_refs[2 * q + 1], cb_refs[2 * q + 1],
                        first, last)
        o_ref[:, q * tn:(q + 1) * tn] = (_silu(g) * v).astype(o_ref.dtype)


def _ffn_up_call(h, w_up, layer, conv_w, conv_b, dims):
    m, k = h.shape
    dff = w_up.shape[2] // 2
    tm = dims.tm
    tn = FFN_TN
    nb = dff // tn
    conv_b = conv_b.reshape(1, -1)

    def run(first_block, steps, nblk, carried):
        cols = []
        for q in range(nblk):
            cols += [lambda i, j, q=q: (0, first_block + nblk * j + q),
                     lambda i, j, q=q: (0, nb + first_block + nblk * j + q)]
        in_specs = [pl.BlockSpec((tm, k), lambda i, j: (i, 0), pipeline_mode=pl.Buffered(1))]
        in_specs += [pl.BlockSpec((None, k, tn), lambda i, j, c=c: (layer,) + c(i, j)) for c in cols]
        in_specs += [pl.BlockSpec((3, tn), c) for c in cols]
        in_specs += [pl.BlockSpec((1, tn), c) for c in cols]
        n_in = len(in_specs)
        vmem = (tm * k * 2 + 2 * nblk * (2 * k * tn * 2 + tm * tn * 2) + 16 * tm * tn * 4 + (4 << 20))
        return pl.pallas_call(
            functools.partial(_ffn_up_kernel, dims=dims, nblk=nblk, n_alias=len(carried)),
            grid=(m // tm, steps),
            in_specs=in_specs + [_ANY_SPEC] * len(carried),
            out_specs=pl.BlockSpec((tm, nblk * tn), lambda i, j: (i, first_block // nblk + j)),
            out_shape=jax.ShapeDtypeStruct((m, dff), BF16),
            input_output_aliases={n_in + i: i for i in range(len(carried))},
            compiler_params=_cparams(("arbitrary", "arbitrary"), vmem),
            name="ffn_up_conv_gate",
        )(h, *([w_up] * (2 * nblk)), *([conv_w] * (2 * nblk)), *([conv_b] * (2 * nblk)), *carried)

    nblk = min(FFN_BLOCKS_PER_STEP, nb)
    main_steps = nb // nblk
    act = run(0, main_steps, nblk, ())
    if main_steps * nblk < nb:
        act = run(main_steps * nblk, nb - main_steps * nblk, 1, (act,))
    return act


def _rope(x, cos, sin_signed):
    lane = lax.broadcasted_iota(jnp.int32, x.shape, 1)
    n = x.shape[1]
    swapped = jnp.where((lane & 1) == 0, pltpu.roll(x, n - 1, 1), pltpu.roll(x, 1, 1))
    return x * cos + swapped * sin_signed


def _softmax_parts(s, sc):
    m = jnp.max(s, axis=-1, keepdims=True)
    if sc is not None:
        m = jnp.maximum(m, jnp.max(sc, axis=-1, keepdims=True))
    e = jnp.exp(s - m)
    tot = jnp.sum(e, axis=-1, keepdims=True)
    ec = None
    if sc is not None:
        ec = jnp.exp(sc - m)
        tot = tot + jnp.sum(ec, axis=-1, keepdims=True)
    return e, ec, 1.0 / tot


def _attn_kernel(*refs, latent, heads, tq, lq, lam_init, n_alias):
    if latent:
        q_ref, k_ref, v_ref, kc_ref, vc_ref, cos_ref, sin_ref, lv_ref, g_ref = refs[:9]
        o_ref, kb_s, vb_s, kcb_s, vcb_s = refs[9 + n_alias:]
    else:
        q_ref, k_ref, v_ref, lv_ref, g_ref = refs[:5]
        o_ref, ko_ref, vo_ref, kb_s, vb_s = refs[5 + n_alias:]
        ko_ref[0, 0] = k_ref[...]
        vo_ref[0, 0] = v_ref[...]
    lv = lv_ref[...]
    lam = (jnp.exp(jnp.sum(lv[0:1] * lv[1:2], axis=-1, keepdims=True))
           - jnp.exp(jnp.sum(lv[2:3] * lv[3:4], axis=-1, keepdims=True)) + lam_init)
    gain = g_ref[...] * (1.0 - lam_init)
    scale = QK_DIM ** -0.5
    lo = lax.broadcasted_iota(jnp.int32, (tq, V_DIM), 1) < QK_DIM

    for h in range(heads):
        cols = slice(h * V_DIM, (h + 1) * V_DIM)
        k = k_ref[:, cols]
        if latent:
            k = _rope(k, cos_ref[...], sin_ref[...])
            kcb_s[...] = kc_ref[0, 0, :, cols].astype(BF16)
            vcb_s[...] = vc_ref[0, 0, :, cols].astype(BF16)
        kb_s[...] = k.astype(BF16)
        vb_s[...] = v_ref[:, cols].astype(BF16)

        def q_block(i, carry, cols=cols):
            rows = pl.ds(0, tq) if lq == tq else pl.ds(pl.multiple_of(i * tq, tq), tq)
            q = q_ref[rows, cols]
            if latent:
                q = _rope(q, cos_ref[rows, :], sin_ref[rows, :])
            q = q * scale
            q1 = jnp.where(lo, q, 0.0).astype(BF16)
            q2 = jnp.where(lo, 0.0, q).astype(BF16)
            kb = kb_s[...]
            s1 = _dot_nt(q1, kb)
            s2 = _dot_nt(q2, kb)
            s1c = s2c = None
            if latent:
                kcb = kcb_s[...]
                s1c = _dot_nt(q1, kcb)
                s2c = _dot_nt(q2, kcb)
            e1, e1c, r1 = _softmax_parts(s1, s1c)
            e2, e2c, r2 = _softmax_parts(s2, s2c)
            r2 = r2 * lam
            o = _dot((e1 * r1 - e2 * r2).astype(BF16), vb_s[...])
            if latent:
                o = o + _dot((e1c * r1 - e2c * r2).astype(BF16), vcb_s[...])
            rr = lax.rsqrt(jnp.mean(o * o, axis=-1, keepdims=True) + NORM_EPS)
            o_ref[rows, cols] = (o * rr * gain).astype(o_ref.dtype)
            return carry

        if lq == tq:
            q_block(0, 0)
        else:
            lax.fori_loop(0, lq // tq, q_block, 0, unroll=2)


def _attn_call(p, lv, g, dims, lam_init, latent, layer, carried, cache_k=None, cache_v=None,
               cos=None, sin=None):
    n_heads = dims.att_heads
    if latent:
        nb, lq, heads, row_off = dims.b_lat, dims.l_lat, 1, dims.n_ctx // dims.l_lat
    else:
        nb, lq, heads, row_off = dims.b_ctx, dims.l_ctx, _pick(n_heads, (4, 2, 1)), 0
    tq = min(lq, ATT_TQ)
    hb = n_heads // heads
    w = heads * V_DIM
    q_spec = pl.BlockSpec((lq, w), lambda b, h: (b + row_off, h))
    k_spec = pl.BlockSpec((lq, w), lambda b, h: (b + row_off, hb + h))
    v_spec = pl.BlockSpec((lq, w), lambda b, h: (b + row_off, 2 * hb + h))
    small = [pl.BlockSpec((4, QK_DIM), lambda b, h: (0, 0)),
             pl.BlockSpec((1, V_DIM), lambda b, h: (0, 0))]
    scratch = [pltpu.VMEM((lq, V_DIM), BF16), pltpu.VMEM((lq, V_DIM), BF16)]
    if latent:
        past = cache_k.shape[2]
        c_spec = pl.BlockSpec((1, 1, past, w), lambda b, h: (b, layer, 0, h))
        t_spec = pl.BlockSpec((lq, V_DIM), lambda b, h: (0, 0))
        in_specs = [q_spec, k_spec, v_spec, c_spec, c_spec, t_spec, t_spec] + small
        args = (p, p, p, cache_k, cache_v, cos, sin, lv, g.reshape(1, V_DIM))
        scratch += [pltpu.VMEM((past, V_DIM), BF16), pltpu.VMEM((past, V_DIM), BF16)]
        lk = lq + past
    else:
        in_specs = [q_spec, k_spec, v_spec] + small
        args = (p, p, p, lv, g.reshape(1, V_DIM))
        lk = lq
    n_in = len(args)
    att_shape = jax.ShapeDtypeStruct((dims.n_ctx + dims.n_lat, n_heads * V_DIM), BF16)
    att_spec = pl.BlockSpec((lq, w), lambda b, h: (b + row_off, h))
    if latent:
        out_shape, out_specs = att_shape, att_spec
        aliases = {n_in: 0}
    else:
        kv_shape = jax.ShapeDtypeStruct((nb, dims.depth, lq, n_heads * V_DIM), F32)
        kv_spec = pl.BlockSpec((1, 1, lq, w), lambda b, h: (b, layer, 0, h))
        out_shape, out_specs = (att_shape, kv_shape, kv_shape), (att_spec, kv_spec, kv_spec)
        aliases = {n_in + i: 1 + i for i in range(len(carried))}
    vmem = 2 * 6 * lq * w * 4 + 12 * tq * lk * 4 + (8 << 20)
    return pl.pallas_call(
        functools.partial(_attn_kernel, latent=latent, heads=heads, tq=tq, lq=lq,
                          lam_init=lam_init, n_alias=len(carried)),
        grid=(nb, hb),
        in_specs=in_specs + [_ANY_SPEC] * len(carried),
        out_specs=out_specs,
        out_shape=out_shape,
        input_output_aliases=aliases,
        scratch_shapes=scratch,
        compiler_params=_cparams(("arbitrary", "arbitrary"), vmem),
        name="diff_attention_latent" if latent else "diff_attention_context",
    )(*args, *carried)


def _softplus(x):
    return jnp.maximum(x, 0.0) + jnp.log1p(jnp.exp(-jnp.abs(x)))


def _ssd_kernel(*refs, seq_len, latent, n_alias, n_sub):
    it = iter(refs)
    x_ref, b_ref, c_ref, z_ref, dt_ref = (next(it) for _ in range(5))
    cwx_ref, cbx_ref, cwb_ref, cbb_ref, cwc_ref, cbc_ref = (next(it) for _ in range(6))
    par_ref, d_ref = next(it), next(it)
    if latent:
        sf0_ref, sb0_ref = next(it), next(it)
    for _ in range(n_alias):
        next(it)
    if latent:
        y_ref = next(it)
    else:
        y_ref, sf_ref, sb_ref = next(it), next(it), next(it)
    (xs_s, b_s, c_s, bt_s, cs_s, rcs_s, dtxf_s, dtxb_s, csxf_s, rcsxb_s,
     yf_s, yb_s, stf_s, stb_s) = it

    ch = SSD_CHUNK
    nc = seq_len // ch
    gw = SSD_GW

    def sub(ref, s, w):
        return ref.at[:, s * w:(s + 1) * w]

    def conv_silu(u_ref, w_ref, bias_ref):
        u = u_ref[...]
        t = lax.broadcasted_iota(jnp.int32, u.shape, 0)
        return _silu(_conv3_rows(u, w_ref, bias_ref, t == 0, t == seq_len - 1))

    ri = lax.broadcasted_iota(jnp.int32, (ch, ch), 0)
    ci = lax.broadcasted_iota(jnp.int32, (ch, ch), 1)
    lower = ri >= ci
    upper = ri <= ci
    t_low = jnp.where(lower, 1.0, 0.0).astype(BF16)
    t_up = jnp.where(upper, 1.0, 0.0).astype(BF16)
    er = lax.broadcasted_iota(jnp.int32, (LANE, gw), 0)
    eh = lax.broadcasted_iota(jnp.int32, (LANE, gw), 1) >> int(math.log2(SSD_P))
    exp_f = jnp.where(er == eh, 1.0, 0.0).astype(BF16)
    exp_b = jnp.where(er == eh + SSD_HPG, 1.0, 0.0).astype(BF16)
    lo = lax.broadcasted_iota(jnp.int32, (ch, LANE), 1) < SSD_P

    for s in range(n_sub):
        sub(xs_s, s, gw)[...] = conv_silu(sub(x_ref, s, gw), sub(cwx_ref, s, gw), sub(cbx_ref, s, gw))
        bm = conv_silu(sub(b_ref, s, LANE), sub(cwb_ref, s, LANE), sub(cbb_ref, s, LANE))
        sub(b_s, s, LANE)[...] = bm.astype(BF16)
        for c in range(nc):
            bt_s[s * nc + c] = bm[c * ch:(c + 1) * ch, :].T.astype(BF16)
        sub(c_s, s, LANE)[...] = conv_silu(sub(c_ref, s, LANE), sub(cwc_ref, s, LANE),
                                           sub(cbc_ref, s, LANE)).astype(BF16)
        par = sub(par_ref, s, LANE)
        dt = _softplus(sub(dt_ref, s, LANE)[...] + par[0:1, :])
        dta = dt * (-jnp.exp(par[1:2, :]))
        cs_v, rcs_v = sub(cs_s, s, LANE), sub(rcs_s, s, LANE)
        for c in range(nc):
            blk = dta[c * ch:(c + 1) * ch, :]
            cs_v[c * ch:(c + 1) * ch, :] = _dot_f32_rhs(t_low, blk)
            rcs_v[c * ch:(c + 1) * ch, :] = _dot_f32_rhs(t_up, blk)
        sub(dtxf_s, s, gw)[...] = _dot_f32_lhs(dt, exp_f, terms=2)
        sub(dtxb_s, s, gw)[...] = _dot_f32_lhs(dt, exp_b, terms=2)
        sub(csxf_s, s, gw)[...] = _dot_f32_lhs(cs_v[...], exp_f)
        sub(rcsxb_s, s, gw)[...] = _dot_f32_lhs(rcs_v[...], exp_b)

    def chunk_step(s, c, forward):
        rows = pl.ds(pl.multiple_of(c * ch, ch), ch)
        if forward:
            dtx_r, ex_r, cs_r, base, tri, y_buf, st_buf = dtxf_s, csxf_s, cs_s, 0, lower, yf_s, stf_s
        else:
            dtx_r, ex_r, cs_r, base, tri, y_buf, st_buf = dtxb_s, rcsxb_s, rcs_s, SSD_HPG, upper, yb_s, stb_s
        dtx, ex, cs = sub(dtx_r, s, gw)[rows, :], sub(ex_r, s, gw)[rows, :], sub(cs_r, s, LANE)[rows, :]
        y_buf, st_buf = sub(y_buf, s, gw), sub(st_buf, s, gw)
        xdt = sub(xs_s, s, gw)[rows, :] * dtx
        cc = sub(c_s, s, LANE)[rows, :]
        cb = _dot_nt(cc, sub(b_s, s, LANE)[rows, :])
        st = st_buf[...]
        y_off =_dot(cc, st.astype(BF16)) * jnp.exp(ex)
        cs_t = cs.T
        for j in range(SSD_GW // LANE):
            lanes = slice(j * LANE, (j + 1) * LANE)
            xp = xdt[:, lanes]
            acc = y_off[:, lanes]
            for k in range(LANE // SSD_P):
                hh = base + j * (LANE // SSD_P) + k
                diff = cs[:, hh:hh + 1] - cs_t[hh:hh + 1, :]
                decay = jnp.exp(jnp.where(tri, diff, -jnp.inf))
                g = (cb * decay).astype(BF16)
                xm = jnp.where(lo if k == 0 else jnp.logical_not(lo), xp, 0.0).astype(BF16)
                acc = acc + _dot(g, xm)
            y_buf[rows, lanes] = acc
        edge = ex[ch - 1:ch, :] if forward else ex[0:1, :]
        xd = (xdt * jnp.exp(edge - ex)).astype(BF16)
        st_buf[...] = jnp.exp(edge) * st + _dot(bt_s[s * nc + c], xd)

    def all_chains(i, carry):
        for s in range(n_sub):
            chunk_step(s, i, True)
            chunk_step(s, nc - 1 - i, False)
        return carry

    for s in range(n_sub):
        st_rows = slice(s * gw, (s + 1) * gw)
        sub(stf_s, s, gw)[...] = sf0_ref[0, 0, st_rows, :].T if latent else jnp.zeros((SSD_N, gw), F32)
        sub(stb_s, s, gw)[...] = sb0_ref[0, 0, st_rows, :].T if latent else jnp.zeros((SSD_N, gw), F32)
    lax.fori_loop(0, nc, all_chains, 0)
    if not latent:
        for s in range(n_sub):
            sf_ref[0, 0, s * gw:(s + 1) * gw, :] = sub(stf_s, s, gw)[...].T
            sb_ref[0, 0, s * gw:(s + 1) * gw, :] = sub(stb_s, s, gw)[...].T

    y = yf_s[...] + yb_s[...] + d_ref[...] * xs_s[...]
    y_ref[...] = y * _silu(z_ref[...])


def _ssd_call(p, lw, dims, latent, layer, carried, init_f=None, init_b=None):
    groups = dims.ssd_groups
    gw = SSD_GW
    if latent:
        nb, sl, row_off = dims.b_lat, dims.l_lat, dims.n_ctx // dims.l_lat
    else:
        nb, sl, row_off = dims.b_ctx, dims.l_ctx, 0
    nc = sl // SSD_CHUNK
    o = dims.offs
    starts_w = [(o.z, gw), (o.x, gw), (o.b, LANE), (o.c, LANE), (o.dt, LANE),
                (dims.ssd_width, LANE), (dims.ssd_width + groups * LANE, LANE)]
    n_sub = 2 if (groups % 2 == 0 and sl <= SSD_PAIR_MAX_ROWS
                  and all(st % (2 * w) == 0 for st, w in starts_w)) else 1
    wx, wl = n_sub * gw, n_sub * LANE
    zb, xb, bb, cb, db = o.z // wx, o.x // wx, o.b // wl, o.c // wl, o.dt // wl
    cwb, cwc = dims.ssd_width // wl, (dims.ssd_width + groups * LANE) // wl
    in_specs = [
        pl.BlockSpec((sl, wx), lambda b, g: (b + row_off, xb + g)),
        pl.BlockSpec((sl, wl), lambda b, g: (b + row_off, bb + g)),
        pl.BlockSpec((sl, wl), lambda b, g: (b + row_off, cb + g)),
        pl.BlockSpec((sl, wx), lambda b, g: (b + row_off, zb + g)),
        pl.BlockSpec((sl, wl), lambda b, g: (b + row_off, db + g)),
        pl.BlockSpec((3, wx), lambda b, g: (0, g)),
        pl.BlockSpec((1, wx), lambda b, g: (0, g)),
        pl.BlockSpec((3, wl), lambda b, g: (0, cwb + g)),
        pl.BlockSpec((1, wl), lambda b, g: (0, cwb + g)),
        pl.BlockSpec((3, wl), lambda b, g: (0, cwc + g)),
        pl.BlockSpec((1, wl), lambda b, g: (0, cwc + g)),
        pl.BlockSpec((None, 8, wl), lambda b, g: (layer, 0, g)),
        pl.BlockSpec((1, wx), lambda b, g: (0, g)),
    ]
    args = [p, p, p, p, p, lw["ssd_conv_w"], lw["ssd_conv_b"], lw["ssd_conv_w"], lw["ssd_conv_b"],
            lw["ssd_conv_w"], lw["ssd_conv_b"], lw["ssd_par"], lw["ssd_d"]]
    y_shape = jax.ShapeDtypeStruct((dims.n_ctx + dims.n_lat, dims.ssd_width), F32)
    y_spec = pl.BlockSpec((sl, wx), lambda b, g: (b + row_off, g))
    s_spec = pl.BlockSpec((1, 1, wx, SSD_N), lambda b, g: (b, layer, g, 0))
    if latent:
        in_specs += [s_spec, s_spec]
        args += [init_f, init_b]
        out_shape, out_specs = y_shape, y_spec
        aliases = {len(args): 0}
    else:
        s_shape = jax.ShapeDtypeStruct((nb, dims.depth, dims.ssd_width, SSD_N), F32)
        out_shape, out_specs = (y_shape, s_shape, s_shape), (y_spec, s_spec, s_spec)
        aliases = {len(args) + i: 1 + i for i in range(len(carried))}
    scratch = [
        pltpu.VMEM((sl, wx), F32), pltpu.VMEM((sl, wl), BF16), pltpu.VMEM((sl, wl), BF16),
        pltpu.VMEM((n_sub * nc, SSD_CHUNK, SSD_CHUNK), BF16),
        pltpu.VMEM((sl, wl), F32), pltpu.VMEM((sl, wl), F32),
        pltpu.VMEM((sl, wx), F32), pltpu.VMEM((sl, wx), F32),
        pltpu.VMEM((sl, wx), F32), pltpu.VMEM((sl, wx), F32),
        pltpu.VMEM((sl, wx), F32), pltpu.VMEM((sl, wx), F32),
        pltpu.VMEM((SSD_N, wx), F32), pltpu.VMEM((SSD_N, wx), F32),
    ]
    vmem = (2 * sl * (3 * wx + 3 * wl) * 4 + sl * (8 * wx + 4 * wl) * 4 + 8 * sl * wx * 4
            + (8 << 20))
    return pl.pallas_call(
        functools.partial(_ssd_kernel, seq_len=sl, latent=latent, n_alias=len(carried),
                          n_sub=n_sub),
        grid=(nb, groups // n_sub),
        in_specs=in_specs + [_ANY_SPEC] * len(carried),
        out_specs=out_specs,
        out_shape=out_shape,
        input_output_aliases=aliases,
        scratch_shapes=scratch,
        compiler_params=_cparams(("arbitrary", "arbitrary"), vmem),
        name="ssd_latent" if latent else "ssd_context",
    )(*args, *carried)


def _dft_mats(l):
    n = 3 * l // 2
    nh = n // 2
    k = np.arange(nh, dtype=np.int64)[:, None]
    t = np.arange(l, dtype=np.int64)[None, :]
    ang = 2.0 * np.pi * ((k * t) % n).astype(np.float64) / n
    fwd = np.concatenate([np.cos(ang), -np.sin(ang)], axis=0)
    fwd[nh, :] = np.where(np.arange(l) % 2 == 0, 1.0, -1.0)
    pos = (np.arange(l, dtype=np.int64) + l // 2)[:, None]
    kk = np.arange(nh, dtype=np.int64)[None, :]
    ang2 = 2.0 * np.pi * ((pos * kk) % n).astype(np.float64) / n
    inv = np.concatenate([2.0 * np.cos(ang2), -2.0 * np.sin(ang2)], axis=1) / n
    inv[:, 0] = 1.0 / n
    inv[:, nh] = np.where(pos[:, 0] % 2 == 0, 1.0, -1.0) / n

    def split(m):
        hi = m.astype(np.float32).astype(BF16)
        lo = (m - hi.astype(np.float64)).astype(np.float32).astype(BF16)
        return hi, lo

    return split(fwd) + split(inv)


def _hy_feats(l):
    off = jnp.arange(l, dtype=F32) - (l // 2)
    band = HY_FREQ_BASE ** (-jnp.arange(HY_BANDS, dtype=F32) / HY_BANDS)
    ang = off[:, None] * band
    feats = jnp.concatenate([off[:, None] / l, jnp.sin(ang), jnp.cos(ang)], axis=-1)
    return jnp.pad(feats, ((0, 0), (0, LANE - feats.shape[1])))


def _hy_filter_kernel(feats_ref, w1_ref, b1_ref, w2_ref, b2_ref, fr_ref, w3_ref, dl_ref,
                      fh_ref, fl_ref, o_ref, *, seq_len):
    fr = fr_ref[0]
    h1 = jnp.sin(fr[0:1] * (_dot_hp(feats_ref[...], w1_ref[0]) + b1_ref[0]))
    h2 = jnp.sin(fr[1:2] * (_dot_hp(h1, w2_ref[0]) + b2_ref[0]))
    filt = _dot_hp(h2, w3_ref[0])
    off = lax.broadcasted_iota(jnp.int32, filt.shape, 0).astype(F32) - (seq_len // 2)
    dist = jnp.abs(off) * (2.0 / seq_len)
    h = filt * jnp.exp(-dist * dl_ref[...])
    hh, hl = _split_bf16(h, 2)
    fh = fh_ref[...]
    o_ref[0, 0] = _dot(fh, hh) + _dot(fh, hl) + _dot(fl_ref[...], hh)


def _hy_filter_call(seq_len, hw, mats, dims):
    depth, hy = dims.depth, dims.hy_width
    cb = HY_CB
    ncb = hy // cb
    fh, fl = mats[0], mats[1]
    n = fh.shape[0]
    feats = _hy_feats(seq_len)
    deltas = jnp.linspace(HY_MIN_DECAY, HY_MAX_DECAY, hy, dtype=F32).reshape(1, hy)
    vmem = 2 * 2 * (n * seq_len * 2) + 6 * n * cb * 4 + (8 << 20)
    return pl.pallas_call(
        functools.partial(_hy_filter_kernel, seq_len=seq_len),
        grid=(depth, 2, ncb),
        in_specs=[
            pl.BlockSpec((seq_len, LANE), lambda l, o, c: (0, 0)),
            pl.BlockSpec((1, LANE, LANE), lambda l, o, c: (l, 0, 0)),
            pl.BlockSpec((1, 1, LANE), lambda l, o, c: (l, 0, 0)),
            pl.BlockSpec((1, LANE, LANE), lambda l, o, c: (l, 0, 0)),
            pl.BlockSpec((1, 1, LANE), lambda l, o, c: (l, 0, 0)),
            pl.BlockSpec((1, 2, LANE), lambda l, o, c: (l, 0, 0)),
            pl.BlockSpec((1, LANE, cb), lambda l, o, c: (l, 0, o * ncb + c)),
            pl.BlockSpec((1, cb), lambda l, o, c: (0, c)),
            pl.BlockSpec((n, seq_len), lambda l, o, c: (0, 0)),
            pl.BlockSpec((n, seq_len), lambda l, o, c: (0, 0)),
        ],
        out_specs=pl.BlockSpec((1, 1, n, cb), lambda l, o, c: (l, o, 0, c)),
        out_shape=jax.ShapeDtypeStruct((depth, 2, n, hy), F32),
        compiler_params=_cparams(("arbitrary",) * 3, vmem),
        name="hyena_filter_spectrum",
    )(feats, hw["w1"], hw["b1"], hw["w2"], hw["b2"], hw["freq"], hw["w3"], deltas, fh, fl)


def _hy_conv_kernel(*refs, seq_len, n_seq, n_alias):
    (x1_ref, x2_ref, v_ref, cw1_ref, cb1_ref, cw2_ref, cb2_ref, cwv_ref, cbv_ref,
     hf_ref, bias_ref, fh_ref, gh_ref) = refs[:13]
    o_ref = refs[13 + n_alias]
    nh = fh_ref.shape[0] // 2
    cb = o_ref.shape[1]
    t = lax.broadcasted_iota(jnp.int32, (seq_len, cb), 0)
    first = t == 0
    last = t == seq_len - 1
    bin0 = lax.broadcasted_iota(jnp.int32, (nh, cb), 0) == 0
    fh = fh_ref[...]
    gh = gh_ref[...]
    rows = [slice(s * seq_len, (s + 1) * seq_len) for s in range(n_seq)]
    gates = [(_conv3_rows(x1_ref[r, :], cw1_ref, cb1_ref, first, last),
              _conv3_rows(x2_ref[r, :], cw2_ref, cb2_ref, first, last)) for r in rows]
    zz = [_conv3_rows(v_ref[r, :], cwv_ref, cbv_ref, first, last) for r in rows]
    for o in range(2):
        hr, hi = hf_ref[0, o, :nh, :], hf_ref[0, o, nh:, :]
        uf = []
        for z in zz:
            zh, zl = _split_bf16(z, 2)
            uf.append(_dot(fh, zh) + _dot(fh, zl))
        ys = []
        for u in uf:
            ur, ui = u[:nh], u[nh:]
            ii = ui * hi
            yr = ur * hr - jnp.where(bin0, 0.0, ii)
            yi = jnp.where(bin0, ii, ur * hi + ui * hr)
            yh, yl = _split_bf16(jnp.concatenate([yr, yi], axis=0), 2)
            ys.append(_dot(gh, yh) + _dot(gh, yl))
        zz = [g[o] * (y + z * bias_ref[o:o + 1, :]) for g, y, z in zip(gates, ys, zz)]
    for r, z in zip(rows, zz):
        o_ref[r, :] = z


def _hy_conv_call(p, hf, lw, mats, dims, layer, latent, carried):
    hy = dims.hy_width
    cb = HY_CB
    ncb = hy // cb
    if latent:
        nb, sl, row_start = dims.b_lat, dims.l_lat, dims.n_ctx
    else:
        nb, sl, row_start = dims.b_ctx, dims.l_ctx, 0
    n_seq = max(n for n in (8, 4, 2, 1) if nb % n == 0 and (n == 1 or n * sl <= HY_ROWS_PER_STEP))
    rows = n_seq * sl
    row_off = row_start // rows
    n = mats[0].shape[0]
    const = lambda c, b: (0, 0)
    in_specs = [pl.BlockSpec((rows, cb), lambda c, b, k=k: (b + row_off, k * ncb + c)) for k in range(3)]
    for k in range(3):
        in_specs += [pl.BlockSpec((3, cb), lambda c, b, k=k: (0, k * ncb + c)),
                     pl.BlockSpec((1, cb), lambda c, b, k=k: (0, k * ncb + c))]
    once = pl.Buffered(1)
    in_specs += [
        pl.BlockSpec((1, 2, n, cb), lambda c, b: (layer, 0, 0, c), pipeline_mode=once),
        pl.BlockSpec((2, cb), lambda c, b: (0, c)),
        pl.BlockSpec((n, sl), const, pipeline_mode=once),
        pl.BlockSpec((sl, n), const, pipeline_mode=once),
    ]
    cw, cbias = lw["hy_conv_w"], lw["hy_conv_b"]
    vmem = (2 * (n * sl * 2) + 2 * n * cb * 4 + 2 * 4 * rows * cb * 4
            + n_seq * (6 * n + 8 * sl) * cb * 4 + (8 << 20))
    return pl.pallas_call(
        functools.partial(_hy_conv_kernel, seq_len=sl, n_seq=n_seq, n_alias=len(carried)),
        grid=(ncb, nb // n_seq),
        in_specs=in_specs + [_ANY_SPEC] * len(carried),
        out_specs=pl.BlockSpec((rows, cb), lambda c, b: (b + row_off, c)),
        out_shape=jax.ShapeDtypeStruct((dims.n_ctx + dims.n_lat, hy), F32),
        input_output_aliases={13 + i: i for i in range(len(carried))},
        compiler_params=_cparams(("arbitrary", "arbitrary"), vmem),
        name="hyena_conv_latent" if latent else "hyena_conv_context",
    )(p, p, p, cw, cbias, cw, cbias, cw, cbias, hf, lw["hy_bias"], mats[0], mats[2], *carried)


class _Offsets:
    def __init__(self, att_w, ssd_w, groups, hy_w):
        self.q, self.k, self.v = 0, att_w, 2 * att_w
        self.z = 3 * att_w
        self.x = self.z + ssd_w
        self.b = self.x + ssd_w
        self.c = self.b + groups * SSD_N
        self.dt = self.c + groups * SSD_N
        used = self.dt + groups * LANE
        self.total = PROJ_COL_ALIGN * ((used + PROJ_COL_ALIGN - 1) // PROJ_COL_ALIGN)


class _Dims:
    pass


def _make_dims(x_prompt, x_sample, cache_k, state_ssd_fwd, w_in, hy_norm_g, ffn_w_down):
    d = _Dims()
    d.b_ctx, d.l_ctx, d.d_model = x_prompt.shape
    d.b_lat, d.l_lat, _ = x_sample.shape
    d.n_ctx = d.b_ctx * d.l_ctx
    d.n_lat = d.b_lat * d.l_lat
    d.depth = w_in.shape[0]
    d.att_heads = cache_k.shape[3]
    d.att_width = d.att_heads * V_DIM
    d.ssd_heads = state_ssd_fwd.shape[2]
    d.ssd_width = d.ssd_heads * SSD_P
    d.ssd_groups = d.ssd_heads // SSD_HPG
    d.hy_width = hy_norm_g.shape[-1]
    d.d_ff = ffn_w_down.shape[1]
    d.offs = _Offsets(d.att_width, d.ssd_width, d.ssd_groups, d.hy_width)
    d.tm = d.l_lat
    assert d.l_lat % d.l_ctx == 0 and d.n_ctx % d.tm == 0 and d.l_ctx % 256 == 0
    assert d.l_lat & (d.l_lat - 1) == 0 and d.l_ctx & (d.l_ctx - 1) == 0
    assert d.hy_width % HY_CB == 0 and d.d_ff % FFN_TN == 0
    assert d.ssd_heads % SSD_HPG == 0
    return d


def _rope_tables(seq_len):
    rows = seq_len // GRID_W
    r, col = jnp.meshgrid(jnp.arange(rows), jnp.arange(GRID_W), indexing="ij")
    npair = QK_DIM // 4
    inv = ROPE_BASE ** (-jnp.arange(npair, dtype=F32) / npair)
    ang = jnp.concatenate([r.reshape(-1, 1).astype(F32) * inv, col.reshape(-1, 1).astype(F32) * inv], axis=-1)
    cos, sin = jnp.cos(ang), jnp.sin(ang)
    cos2 = jnp.repeat(cos, 2, axis=-1)
    sin2 = jnp.stack([-sin, sin], axis=-1).reshape(seq_len, QK_DIM)
    return jnp.tile(cos2, (1, 2)), jnp.tile(sin2, (1, 2))


def kernel(x_prompt, x_sample, cache_k, cache_v, state_ssd_fwd, state_ssd_bwd, c, c_ctx, norm1_g, norm2_g, w_ada, b_ada, w_in, att_lambda, att_subln_g, ssd_conv_w, ssd_conv_b, ssd_dt_bias, ssd_a_log, ssd_d, ssd_norm_g, hy_conv_w, hy_conv_b, hy_w1, hy_b1, hy_w2, hy_b2, hy_freq, hy_w3, hy_bias, hy_norm_g, w_out, ffn_w_up, ffn_conv_w, ffn_conv_b, ffn_w_down, final_norm_g):
    dims = _make_dims(x_prompt, x_sample, cache_k, state_ssd_fwd, w_in, hy_norm_g, ffn_w_down)
    d, depth, offs = dims.d_model, dims.depth, dims.offs
    n_ctx = dims.n_ctx

    mod_rows = 16 * ((1 + dims.b_lat + 15) // 16)
    cond = jnp.zeros((mod_rows, d), F32).at[0].set(c_ctx).at[1:1 + dims.b_lat].set(c)
    mod_all = _ada_call(cond, w_ada, b_ada).reshape(depth, mod_rows, 6, 1, d)

    pad_h = LANE - hy_w1.shape[-1]
    hw = {
        "w1": jnp.pad(hy_w1, ((0, 0), (0, LANE - hy_w1.shape[1]), (0, pad_h))),
        "b1": jnp.pad(hy_b1, ((0, 0), (0, pad_h)))[:, None, :],
        "w2": jnp.pad(hy_w2, ((0, 0), (0, pad_h), (0, pad_h))),
        "b2": jnp.pad(hy_b2, ((0, 0), (0, pad_h)))[:, None, :],
        "freq": jnp.pad(hy_freq, ((0, 0), (0, 0), (0, pad_h))),
        "w3": jnp.pad(hy_w3, ((0, 0), (0, pad_h), (0, 0))),
    }
    mats_ctx = _dft_mats(dims.l_ctx)
    mats_lat = _dft_mats(dims.l_lat)
    hf_ctx = _hy_filter_call(dims.l_ctx, hw, mats_ctx, dims)
    hf_lat = _hy_filter_call(dims.l_lat, hw, mats_lat, dims)

    cos_t, sin_t = _rope_tables(dims.l_lat)
    ck = cache_k.reshape(dims.b_lat, depth, cache_k.shape[2], dims.att_width)
    cv = cache_v.reshape(dims.b_lat, depth, cache_v.shape[2], dims.att_width)
    sf0 = state_ssd_fwd.reshape(dims.b_lat, depth, dims.ssd_width, SSD_N)
    sb0 = state_ssd_bwd.reshape(dims.b_lat, depth, dims.ssd_width, SSD_N)

    x = jnp.concatenate([x_prompt.reshape(n_ctx, d), x_sample.reshape(dims.n_lat, d)], axis=0)
    new_kv, new_st = (), ()

    dt0, dt_cols = offs.dt, 2 * dims.ssd_heads
    pick = np.zeros((dt_cols, dims.ssd_groups * LANE), np.float32)
    for g in range(dims.ssd_groups):
        for direction in range(2):
            for hh in range(SSD_HPG):
                pick[direction * dims.ssd_heads + g * SSD_HPG + hh,
                     g * LANE + direction * SSD_HPG + hh] = 1.0
    regroup = lambda t: jnp.einsum("...k,kn->...n", t, pick, precision=lax.Precision.HIGHEST)
    w_dt = regroup(w_in[:, :, dt0:dt0 + dt_cols])
    pad_cols = offs.total - dt0 - w_dt.shape[2]
    w_proj = jnp.concatenate([w_in[:, :, :dt0].astype(BF16), w_dt.astype(BF16),
                              jnp.zeros((depth, d, pad_cols), BF16)], axis=2)
    w_hy = w_in[:, :, dt0 + dt_cols:].astype(BF16)
    ssd_par = jnp.zeros((depth, 8, dims.ssd_groups * LANE), F32)
    ssd_par = ssd_par.at[:, 0].set(regroup(ssd_dt_bias.reshape(depth, dt_cols)))
    ssd_par = ssd_par.at[:, 1].set(regroup(ssd_a_log.reshape(depth, dt_cols)))
    w_o = w_out.astype(BF16)
    w_up = ffn_w_up.astype(BF16)
    w_down = ffn_w_down.astype(BF16)
    a_w, s_w = dims.att_width, dims.ssd_width
    if a_w % s_w == 0 and (a_w + s_w) % dims.hy_width == 0:
        out_terms = lambda l, att, y_ssd, y_hy: [
            (att, w_o, l, 0), (y_ssd, w_o, l, a_w // s_w), (y_hy, w_o, l, (a_w + s_w) // dims.hy_width)]
    else:
        out_terms = lambda l, att, y_ssd, y_hy: [
            (att, w_o[l:l + 1, :a_w], 0, 0), (y_ssd, w_o[l:l + 1, a_w:a_w + s_w], 0, 0),
            (y_hy, w_o[l:l + 1, a_w + s_w:], 0, 0)]

    for l in range(depth):
        lam_init = 0.8 - 0.6 * math.exp(-0.3 * l)
        mod = mod_all[l]
        lw = {
            "ssd_conv_w": ssd_conv_w[l], "ssd_conv_b": ssd_conv_b[l].reshape(1, -1),
            "ssd_par": ssd_par,
            "ssd_d": jnp.repeat(ssd_d[l], SSD_P).reshape(1, -1),
            "hy_conv_w": hy_conv_w[l], "hy_conv_b": hy_conv_b[l].reshape(1, -1), "hy_bias": hy_bias[l],
        }

        h = _norm_mod_call(x, norm1_g[l], mod, 0, 1, dims)
        p = _mm_call(h, w_proj, l, dims)
        p_hy = _mm_call(h, w_hy, l, dims)

        att, new_k, new_v = _attn_call(p, att_lambda[l], att_subln_g[l], dims, lam_init, False, l, new_kv)
        new_kv = (new_k, new_v)
        att = _attn_call(p, att_lambda[l], att_subln_g[l], dims, lam_init, True, l, (att,),
                         cache_k=ck, cache_v=cv, cos=cos_t, sin=sin_t)
        ys, new_sf, new_sb = _ssd_call(p, lw, dims, False, l, new_st)
        new_st = (new_sf, new_sb)
        ys = _ssd_call(p, lw, dims, True, l, (ys,), init_f=sf0, init_b=sb0)
        zz = _hy_conv_call(p_hy, hf_ctx, lw, mats_ctx, dims, l, False, ())
        zz = _hy_conv_call(p_hy, hf_lat, lw, mats_lat, dims, l, True, (zz,))

        y_ssd = _norm_call(ys, ssd_norm_g[l], BF16, name="ssd_norm")
        y_hy = _norm_call(zz, hy_norm_g[l], BF16, name="hyena_norm")
        x = _mm_res_call(out_terms(l, att, y_ssd, y_hy), x, mod, 2, dims,
                         (1024, 768, 512, 384, 256, 128))

        h2 = _norm_mod_call(x, norm2_g[l], mod, 3, 4, dims)
        act = _ffn_up_call(h2, w_up, l, ffn_conv_w[l], ffn_conv_b[l], dims)
        x = _mm_res_call([(act, w_down, l, 0)], x, mod, 5, dims, (DOWN_TILE, 256, 128),
                         tm=min(DOWN_TILE, dims.tm))

    y_prompt = _norm_call(x, final_norm_g, F32, 0, n_ctx, name="final_norm").reshape(x_prompt.shape)
    y_sample = _norm_call(x, final_norm_g, F32, n_ctx, dims.n_lat, name="final_norm").reshape(x_sample.shape)
    kv_shape = (dims.b_ctx, depth, dims.l_ctx, dims.att_heads, V_DIM)
    st_shape = (dims.b_ctx, depth, dims.ssd_heads, SSD_P, SSD_N)
    return (y_prompt, y_sample, new_kv[0].reshape(kv_shape), new_kv[1].reshape(kv_shape),
            new_st[0].reshape(st_shape), new_st[1].reshape(st_shape))
```

```python
import functools
import math

import numpy as np
import jax
import jax.numpy as jnp
from jax import lax
from jax.experimental import pallas as pl
from jax.experimental.pallas import tpu as pltpu

F32 = jnp.float32
BF16 = jnp.bfloat16

LANE = 128
V7X_VMEM_BYTES = 64 * 1024 * 1024
VMEM_CAP = V7X_VMEM_BYTES - 6 * 1024 * 1024

NORM_EPS = 1e-6
GRID_W = 64
ROPE_BASE = 10000.0
QK_DIM = 64
V_DIM = 128
SSD_P = 64
SSD_N = 128
SSD_CHUNK = 128
SSD_HPG = 6
SSD_GW = SSD_HPG * SSD_P
HY_BANDS = 16
HY_FREQ_BASE = 10000.0
HY_MIN_DECAY = -math.log(1e-2) / 1.5
HY_MAX_DECAY = -math.log(1e-2) / 0.3
HY_CB = 256
FFN_TN = 256
FFN_BLOCKS_PER_STEP = 3
PROJ_COL_ALIGN = 1024
ATT_TQ = 1024
DOWN_TILE = 512
SSD_PAIR_MAX_ROWS = 512
HY_ROWS_PER_STEP = 2048

_ANY_SPEC = pl.BlockSpec(memory_space=pl.ANY)


def _cparams(sem, vmem_bytes):
    limit = int(min(VMEM_CAP, max(vmem_bytes, 16 * 1024 * 1024)))
    return pltpu.CompilerParams(dimension_semantics=sem, vmem_limit_bytes=limit)


def _pick(n, candidates):
    for c in candidates:
        if n % c == 0:
            return c
    raise ValueError(f"no block size in {candidates} divides {n}")


def _dot(a, b):
    return jnp.dot(a, b, preferred_element_type=F32)


def _dot_nt(a, b):
    return lax.dot_general(a, b, (((1,), (1,)), ((), ())), preferred_element_type=F32)


def _split_bf16(x, n):
    parts = []
    r = x
    for _ in range(n):
        h = r.astype(BF16)
        parts.append(h)
        r = r - h.astype(F32)
    return parts


def _dot_f32_lhs(a, b_bf16, terms=3):
    return sum(_dot(p, b_bf16) for p in _split_bf16(a, terms))


def _dot_f32_rhs(a_bf16, b):
    return sum(_dot(a_bf16, p) for p in _split_bf16(b, 3))


def _dot_hp(a, b):
    ah, al = _split_bf16(a, 2)
    bh, bl = _split_bf16(b, 2)
    return _dot(ah, bh) + _dot(ah, bl) + _dot(al, bh)


def _silu(x):
    return x * jax.nn.sigmoid(x)


def _conv3_rows(u, w_ref, b_ref, first, last):
    n = u.shape[0]
    prev = jnp.where(first, 0.0, pltpu.roll(u, 1, 0))
    nxt = jnp.where(last, 0.0, pltpu.roll(u, n - 1, 0))
    return prev * w_ref[0:1, :] + b_ref[...] + u * w_ref[1:2, :] + nxt * w_ref[2:3, :]


def _ada_kernel(c_ref, w_ref, b_ref, o_ref):
    s = _silu(c_ref[...]).astype(BF16)
    o_ref[0] = _dot(s, w_ref[0].astype(BF16)) + b_ref[0]


def _ada_call(cond, w_ada, b_ada):
    depth, d, n = w_ada.shape
    rows = cond.shape[0]
    tn = _pick(n, (512, 256, 128))
    vmem = 2 * (d * tn * 4) + d * tn * 2 + 4 * rows * (d + 2 * tn) * 4 + (4 << 20)
    return pl.pallas_call(
        _ada_kernel,
        grid=(depth, n // tn),
        in_specs=[
            pl.BlockSpec((rows, d), lambda l, j: (0, 0)),
            pl.BlockSpec((1, d, tn), lambda l, j: (l, 0, j)),
            pl.BlockSpec((1, 1, tn), lambda l, j: (l, 0, j)),
        ],
        out_specs=pl.BlockSpec((1, rows, tn), lambda l, j: (l, 0, j)),
        out_shape=jax.ShapeDtypeStruct((depth, rows, n), F32),
        compiler_params=_cparams(("arbitrary", "arbitrary"), vmem),
        name="ada_mod",
    )(cond, w_ada, b_ada.reshape(depth, 1, n))


def _norm_mod_kernel(x_ref, g_ref, sc_ref, sh_ref, o_ref):
    x = x_ref[...]
    r = lax.rsqrt(jnp.mean(x * x, axis=-1, keepdims=True) + NORM_EPS)
    y = x * r * g_ref[...]
    o_ref[...] = (y * (1.0 + sc_ref[0, 0]) + sh_ref[0, 0]).astype(o_ref.dtype)


def _norm_kernel(x_ref, g_ref, o_ref):
    x = x_ref[...]
    r = lax.rsqrt(jnp.mean(x * x, axis=-1, keepdims=True) + NORM_EPS)
    o_ref[...] = (x * r * g_ref[...]).astype(o_ref.dtype)


def _mod_row(i, tm, dims):
    start = i * tm
    lat = jnp.maximum(start - dims.n_ctx, 0) // dims.l_lat
    return jnp.where(start >= dims.n_ctx, lat + 1, 0)


def _norm_mod_call(x, g, mod, sh_idx, sc_idx, dims):
    m, d = x.shape
    tr = 256
    row = functools.partial(_mod_row, tm=tr, dims=dims)
    vmem = 2 * tr * d * (4 + 2) + 3 * tr * d * 4 + (4 << 20)
    return pl.pallas_call(
        _norm_mod_kernel,
        grid=(m // tr,),
        in_specs=[
            pl.BlockSpec((tr, d), lambda i: (i, 0)),
            pl.BlockSpec((1, d), lambda i: (0, 0)),
            pl.BlockSpec((1, 1, 1, d), lambda i: (row(i), sc_idx, 0, 0)),
            pl.BlockSpec((1, 1, 1, d), lambda i: (row(i), sh_idx, 0, 0)),
        ],
        out_specs=pl.BlockSpec((tr, d), lambda i: (i, 0)),
        out_shape=jax.ShapeDtypeStruct((m, d), BF16),
        compiler_params=_cparams(("arbitrary",), vmem),
        name="norm_mod",
    )(x, g.reshape(1, d), mod, mod)


def _norm_call(x, g, out_dtype, row_start=0, rows=None, name="norm"):
    m, d = x.shape
    rows = m if rows is None else rows
    tr = 256
    off = row_start // tr
    vmem = 2 * tr * d * 8 + 3 * tr * d * 4 + (4 << 20)
    return pl.pallas_call(
        _norm_kernel,
        grid=(rows // tr,),
        in_specs=[
            pl.BlockSpec((tr, d), lambda i: (i + off, 0)),
            pl.BlockSpec((1, d), lambda i: (0, 0)),
        ],
        out_specs=pl.BlockSpec((tr, d), lambda i: (i, 0)),
        out_shape=jax.ShapeDtypeStruct((rows, d), out_dtype),
        compiler_params=_cparams(("arbitrary",), vmem),
        name=name,
    )(x, g.reshape(1, d))


def _mm_kernel(a_ref, b_ref, o_ref):
    o_ref[...] = _dot(a_ref[...], b_ref[...]).astype(o_ref.dtype)


def _mm_res_kernel(*refs, n_terms):
    x_ref, g_ref, o_ref = refs[2 * n_terms:]
    acc = _dot(refs[0][...], refs[1][...])
    for i in range(1, n_terms):
        acc = acc + _dot(refs[2 * i][...], refs[2 * i + 1][...])
    o_ref[...] = x_ref[...] + g_ref[0, 0] * acc


def _mm_call(a, w_stack, layer, dims, out_dtype=F32):
    m, k = a.shape
    n = w_stack.shape[2]
    b = w_stack
    tm = dims.tm
    tn = _pick(n, (1024, 768, 512, 384, 256, 128))
    vmem = 2 * (tm * k * 2 + k * tn * 2 + tm * tn * 4) + tm * tn * 4 + (4 << 20)
    return pl.pallas_call(
        _mm_kernel,
        grid=(m // tm, n // tn),
        in_specs=[
            pl.BlockSpec((tm, k), lambda i, j: (i, 0)),
            pl.BlockSpec((None, k, tn), lambda i, j: (layer, 0, j)),
        ],
        out_specs=pl.BlockSpec((tm, tn), lambda i, j: (i, j)),
        out_shape=jax.ShapeDtypeStruct((m, n), out_dtype),
        compiler_params=_cparams(("arbitrary", "arbitrary"), vmem),
        name="proj",
    )(a, b)


def _mm_res_call(terms, x, mod, g_idx, dims, tn_candidates, tm=None):
    m, n = x.shape
    k = sum(t[0].shape[1] for t in terms)
    tm = dims.tm if tm is None else tm
    tn = _pick(n, tn_candidates)
    row = functools.partial(_mod_row, tm=tm, dims=dims)
    vmem = 2 * (tm * k * 2 + k * tn * 2 + 2 * tm * tn * 4) + tm * tn * 4 + (4 << 20)
    in_specs, args = [], []
    for a, w_stack, layer, row_block in terms:
        ka = a.shape[1]
        in_specs.append(pl.BlockSpec((tm, ka), lambda i, j: (i, 0)))
        in_specs.append(pl.BlockSpec((None, ka, tn), lambda i, j, layer=layer, rb=row_block: (layer, rb, j)))
        args += [a, w_stack]
    in_specs += [pl.BlockSpec((tm, tn), lambda i, j: (i, j)),
                 pl.BlockSpec((1, 1, 1, tn), lambda i, j: (row(i), g_idx, 0, j))]
    return pl.pallas_call(
        functools.partial(_mm_res_kernel, n_terms=len(terms)),
        grid=(m // tm, n // tn),
        in_specs=in_specs,
        out_specs=pl.BlockSpec((tm, tn), lambda i, j: (i, j)),
        out_shape=jax.ShapeDtypeStruct((m, n), F32),
        compiler_params=_cparams(("arbitrary", "arbitrary"), vmem),
        name="proj_residual",
    )(*args, x, mod)


def _ffn_up_kernel(*refs, dims, nblk, n_alias):
    a_ref = refs[0]
    b_refs = refs[1:1 + 2 * nblk]
    cw_refs = refs[1 + 2 * nblk:1 + 4 * nblk]
    cb_refs = refs[1 + 4 * nblk:1 + 6 * nblk]
    o_ref = refs[1 + 6 * nblk + n_alias]
    a = a_ref[...]
    tm = a.shape[0]
    tn = b_refs[0].shape[1]
    seq = jnp.where(pl.program_id(0) * tm < dims.n_ctx, dims.l_ctx, dims.l_lat)
    t = lax.broadcasted_iota(jnp.int32, (tm, tn), 0) & (seq - 1)
    first = t == 0
    last = t == seq - 1
    for q in range(nblk):
        g = _conv3_rows(_dot(a, b_refs[2 * q][...]), cw_refs[2 * q], cb_refs[2 * q], first, last)
        v = _conv3_rows(_dot(a, b_refs[2 * q + 1][...]), cw_refs[2 * q + 1], cb_refs[2 * q + 1],
                        first, last)
        o_ref[:, q * tn:(q + 1) * tn] = (_silu(g) * v).astype(o_ref.dtype)


def _ffn_up_call(h, w_up, layer, conv_w, conv_b, dims):
    m, k = h.shape
    dff = w_up.shape[2] // 2
    tm = dims.tm
    tn = FFN_TN
    nb = dff // tn
    conv_b = conv_b.reshape(1, -1)

    def run(first_block, steps, nblk, carried):
        cols = []
        for q in range(nblk):
            cols += [lambda i, j, q=q: (0, first_block + nblk * j + q),
                     lambda i, j, q=q: (0, nb + first_block + nblk * j + q)]
        in_specs = [pl.BlockSpec((tm, k), lambda i, j: (i, 0), pipeline_mode=pl.Buffered(1))]
        in_specs += [pl.BlockSpec((None, k, tn), lambda i, j, c=c: (layer,) + c(i, j)) for c in cols]
        in_specs += [pl.BlockSpec((3, tn), c) for c in cols]
        in_specs += [pl.BlockSpec((1, tn), c) for c in cols]
        n_in = len(in_specs)
        vmem = (tm * k * 2 + 2 * nblk * (2 * k * tn * 2 + tm * tn * 2) + 16 * tm * tn * 4 + (4 << 20))
        return pl.pallas_call(
            functools.partial(_ffn_up_kernel, dims=dims, nblk=nblk, n_alias=len(carried)),
            grid=(m // tm, steps),
            in_specs=in_specs + [_ANY_SPEC] * len(carried),
            out_specs=pl.BlockSpec((tm, nblk * tn), lambda i, j: (i, first_block // nblk + j)),
            out_shape=jax.ShapeDtypeStruct((m, dff), BF16),
            input_output_aliases={n_in + i: i for i in range(len(carried))},
            compiler_params=_cparams(("arbitrary", "arbitrary"), vmem),
            name="ffn_up_conv_gate",
        )(h, *([w_up] * (2 * nblk)), *([conv_w] * (2 * nblk)), *([conv_b] * (2 * nblk)), *carried)

    nblk = min(FFN_BLOCKS_PER_STEP, nb)
    main_steps = nb // nblk
    act = run(0, main_steps, nblk, ())
    if main_steps * nblk < nb:
        act = run(main_steps * nblk, nb - main_steps * nblk, 1, (act,))
    return act


def _rope(x, cos, sin_signed):
    lane = lax.broadcasted_iota(jnp.int32, x.shape, 1)
    n = x.shape[1]
    swapped = jnp.where((lane & 1) == 0, pltpu.roll(x, n - 1, 1), pltpu.roll(x, 1, 1))
    return x * cos + swapped * sin_signed


def _softmax_parts(s, sc):
    m = jnp.max(s, axis=-1, keepdims=True)
    if sc is not None:
        m = jnp.maximum(m, jnp.max(sc, axis=-1, keepdims=True))
    e = jnp.exp(s - m)
    tot = jnp.sum(e, axis=-1, keepdims=True)
    ec = None
    if sc is not None:
        ec = jnp.exp(sc - m)
        tot = tot + jnp.sum(ec, axis=-1, keepdims=True)
    return e, ec, 1.0 / tot


def _attn_kernel(*refs, latent, heads, tq, lq, lam_init, n_alias):
    if latent:
        q_ref, k_ref, v_ref, kc_ref, vc_ref, cos_ref, sin_ref, lv_ref, g_ref = refs[:9]
        o_ref, kb_s, vb_s, kcb_s, vcb_s = refs[9 + n_alias:]
    else:
        q_ref, k_ref, v_ref, lv_ref, g_ref = refs[:5]
        o_ref, ko_ref, vo_ref = refs[5 + n_alias:]
        ko_ref[0, 0] = k_ref[...]
        vo_ref[0, 0] = v_ref[...]
    lv = lv_ref[...]
    lam = (jnp.exp(jnp.sum(lv[0:1] * lv[1:2], axis=-1, keepdims=True))
           - jnp.exp(jnp.sum(lv[2:3] * lv[3:4], axis=-1, keepdims=True)) + lam_init)
    gain = g_ref[...] * (1.0 - lam_init)
    scale = QK_DIM ** -0.5
    lo = lax.broadcasted_iota(jnp.int32, (tq, V_DIM), 1) < QK_DIM

    if not latent:
        hs = [slice(h * V_DIM, (h + 1) * V_DIM) for h in range(heads)]
        kb = [k_ref[:, c].astype(BF16) for c in hs]
        vb = [v_ref[:, c].astype(BF16) for c in hs]
        qs = [q_ref[:, c] * scale for c in hs]
        q1 = [jnp.where(lo, q, 0.0).astype(BF16) for q in qs]
        q2 = [jnp.where(lo, 0.0, q).astype(BF16) for q in qs]
        p1 = [_softmax_parts(_dot_nt(a, b), None) for a, b in zip(q1, kb)]
        p2 = [_softmax_parts(_dot_nt(a, b), None) for a, b in zip(q2, kb)]
        ws = [(a[0] * a[2] - b[0] * (b[2] * lam)).astype(BF16) for a, b in zip(p1, p2)]
        outs = [_dot(w, v) for w, v in zip(ws, vb)]
        for c, o in zip(hs, outs):
            rr = lax.rsqrt(jnp.mean(o * o, axis=-1, keepdims=True) + NORM_EPS)
            o_ref[:, c] = (o * rr * gain).astype(o_ref.dtype)
        return

    for h in range(heads):
        cols = slice(h * V_DIM, (h + 1) * V_DIM)
        k = k_ref[:, cols]
        if latent:
            k = _rope(k, cos_ref[...], sin_ref[...])
            kcb_s[...] = kc_ref[0, 0, :, cols].astype(BF16)
            vcb_s[...] = vc_ref[0, 0, :, cols].astype(BF16)
        kb_s[...] = k.astype(BF16)
        vb_s[...] = v_ref[:, cols].astype(BF16)

        def q_block(i, carry, cols=cols):
            rows = pl.ds(0, tq) if lq == tq else pl.ds(pl.multiple_of(i * tq, tq), tq)
            q = q_ref[rows, cols]
            if latent:
                q = _rope(q, cos_ref[rows, :], sin_ref[rows, :])
            q = q * scale
            q1 = jnp.where(lo, q, 0.0).astype(BF16)
            q2 = jnp.where(lo, 0.0, q).astype(BF16)
            kb = kb_s[...]
            s1 = _dot_nt(q1, kb)
            s2 = _dot_nt(q2, kb)
            s1c = s2c = None
            if latent:
                kcb = kcb_s[...]
                s1c = _dot_nt(q1, kcb)
                s2c = _dot_nt(q2, kcb)
            e1, e1c, r1 = _softmax_parts(s1, s1c)
            e2, e2c, r2 = _softmax_parts(s2, s2c)
            r2 = r2 * lam
            o = _dot((e1 * r1 - e2 * r2).astype(BF16), vb_s[...])
            if latent:
                o = o + _dot((e1c * r1 - e2c * r2).astype(BF16), vcb_s[...])
            rr = lax.rsqrt(jnp.mean(o * o, axis=-1, keepdims=True) + NORM_EPS)
            o_ref[rows, cols] = (o * rr * gain).astype(o_ref.dtype)
            return carry

        if lq == tq:
            q_block(0, 0)
        else:
            lax.fori_loop(0, lq // tq, q_block, 0, unroll=2)


def _attn_call(p, lv, g, dims, lam_init, latent, layer, carried, cache_k=None, cache_v=None,
               cos=None, sin=None):
    n_heads = dims.att_heads
    if latent:
        nb, lq, heads, row_off = dims.b_lat, dims.l_lat, 1, dims.n_ctx // dims.l_lat
    else:
        nb, lq, heads, row_off = dims.b_ctx, dims.l_ctx, _pick(n_heads, (6, 4, 2, 1)), 0
        assert lq <= ATT_TQ
    tq = min(lq, ATT_TQ)
    hb = n_heads // heads
    w = heads * V_DIM
    q_spec = pl.BlockSpec((lq, w), lambda b, h: (b + row_off, h))
    k_spec = pl.BlockSpec((lq, w), lambda b, h: (b + row_off, hb + h))
    v_spec = pl.BlockSpec((lq, w), lambda b, h: (b + row_off, 2 * hb + h))
    small = [pl.BlockSpec((4, QK_DIM), lambda b, h: (0, 0)),
             pl.BlockSpec((1, V_DIM), lambda b, h: (0, 0))]
    scratch = []
    if latent:
        scratch += [pltpu.VMEM((lq, V_DIM), BF16), pltpu.VMEM((lq, V_DIM), BF16)]
        past = cache_k.shape[2]
        c_spec = pl.BlockSpec((1, 1, past, w), lambda b, h: (b, layer, 0, h))
        t_spec = pl.BlockSpec((lq, V_DIM), lambda b, h: (0, 0))
        in_specs = [q_spec, k_spec, v_spec, c_spec, c_spec, t_spec, t_spec] + small
        args = (p, p, p, cache_k, cache_v, cos, sin, lv, g.reshape(1, V_DIM))
        scratch += [pltpu.VMEM((past, V_DIM), BF16), pltpu.VMEM((past, V_DIM), BF16)]
        lk = lq + past
    else:
        in_specs = [q_spec, k_spec, v_spec] + small
        args = (p, p, p, lv, g.reshape(1, V_DIM))
        lk = lq
    n_in = len(args)
    att_shape = jax.ShapeDtypeStruct((dims.n_ctx + dims.n_lat, n_heads * V_DIM), BF16)
    att_spec = pl.BlockSpec((lq, w), lambda b, h: (b + row_off, h))
    if latent:
        out_shape, out_specs = att_shape, att_spec
        aliases = {n_in: 0}
    else:
        kv_shape = jax.ShapeDtypeStruct((nb, dims.depth, lq, n_heads * V_DIM), F32)
        kv_spec = pl.BlockSpec((1, 1, lq, w), lambda b, h: (b, layer, 0, h))
        out_shape, out_specs = (att_shape, kv_shape, kv_shape), (att_spec, kv_spec, kv_spec)
        aliases = {n_in + i: 1 + i for i in range(len(carried))}
    vmem = 2 * 6 * lq * w * 4 + 12 * heads * tq * lk * 4 + (8 << 20)
    return pl.pallas_call(
        functools.partial(_attn_kernel, latent=latent, heads=heads, tq=tq, lq=lq,
                          lam_init=lam_init, n_alias=len(carried)),
        grid=(nb, hb),
        in_specs=in_specs + [_ANY_SPEC] * len(carried),
        out_specs=out_specs,
        out_shape=out_shape,
        input_output_aliases=aliases,
        scratch_shapes=scratch,
        compiler_params=_cparams(("arbitrary", "arbitrary"), vmem),
        name="diff_attention_latent" if latent else "diff_attention_context",
    )(*args, *carried)


def _softplus(x):
    return jnp.maximum(x, 0.0) + jnp.log1p(jnp.exp(-jnp.abs(x)))


def _ssd_kernel(*refs, seq_len, latent, n_alias, n_sub):
    it = iter(refs)
    x_ref, b_ref, c_ref, z_ref, dt_ref = (next(it) for _ in range(5))
    cwx_ref, cbx_ref, cwb_ref, cbb_ref, cwc_ref, cbc_ref = (next(it) for _ in range(6))
    par_ref, d_ref = next(it), next(it)
    if latent:
        sf0_ref, sb0_ref = next(it), next(it)
    for _ in range(n_alias):
        next(it)
    if latent:
        y_ref = next(it)
    else:
        y_ref, sf_ref, sb_ref = next(it), next(it), next(it)
    (xs_s, b_s, c_s, bt_s, cs_s, rcs_s, dtxf_s, dtxb_s, csxf_s, rcsxb_s,
     yf_s, yb_s, stf_s, stb_s) = it

    ch = SSD_CHUNK
    nc = seq_len // ch
    gw = SSD_GW

    def sub(ref, s, w):
        return ref.at[:, s * w:(s + 1) * w]

    def conv_silu(u_ref, w_ref, bias_ref):
        u = u_ref[...]
        t = lax.broadcasted_iota(jnp.int32, u.shape, 0)
        return _silu(_conv3_rows(u, w_ref, bias_ref, t == 0, t == seq_len - 1))

    ri = lax.broadcasted_iota(jnp.int32, (ch, ch), 0)
    ci = lax.broadcasted_iota(jnp.int32, (ch, ch), 1)
    lower = ri >= ci
    upper = ri <= ci
    t_low = jnp.where(lower, 1.0, 0.0).astype(BF16)
    t_up = jnp.where(upper, 1.0, 0.0).astype(BF16)
    er = lax.broadcasted_iota(jnp.int32, (LANE, gw), 0)
    eh = lax.broadcasted_iota(jnp.int32, (LANE, gw), 1) >> int(math.log2(SSD_P))
    exp_f = jnp.where(er == eh, 1.0, 0.0).astype(BF16)
    exp_b = jnp.where(er == eh + SSD_HPG, 1.0, 0.0).astype(BF16)
    lo = lax.broadcasted_iota(jnp.int32, (ch, LANE), 1) < SSD_P

    for s in range(n_sub):
        sub(xs_s, s, gw)[...] = conv_silu(sub(x_ref, s, gw), sub(cwx_ref, s, gw), sub(cbx_ref, s, gw))
        bm = conv_silu(sub(b_ref, s, LANE), sub(cwb_ref, s, LANE), sub(cbb_ref, s, LANE))
        sub(b_s, s, LANE)[...] = bm.astype(BF16)
        for c in range(nc):
            bt_s[s * nc + c] = bm[c * ch:(c + 1) * ch, :].T.astype(BF16)
        sub(c_s, s, LANE)[...] = conv_silu(sub(c_ref, s, LANE), sub(cwc_ref, s, LANE),
                                           sub(cbc_ref, s, LANE)).astype(BF16)
        par = sub(par_ref, s, LANE)
        dt = _softplus(sub(dt_ref, s, LANE)[...] + par[0:1, :])
        dta = dt * (-jnp.exp(par[1:2, :]))
        cs_v, rcs_v = sub(cs_s, s, LANE), sub(rcs_s, s, LANE)
        for c in range(nc):
            blk = dta[c * ch:(c + 1) * ch, :]
            cs_v[c * ch:(c + 1) * ch, :] = _dot_f32_rhs(t_low, blk)
            rcs_v[c * ch:(c + 1) * ch, :] = _dot_f32_rhs(t_up, blk)
        sub(dtxf_s, s, gw)[...] = _dot_f32_lhs(dt, exp_f, terms=2)
        sub(dtxb_s, s, gw)[...] = _dot_f32_lhs(dt, exp_b, terms=2)
        sub(csxf_s, s, gw)[...] = _dot_f32_lhs(cs_v[...], exp_f)
        sub(rcsxb_s, s, gw)[...] = _dot_f32_lhs(rcs_v[...], exp_b)

    def chunk_step(s, c, forward):
        rows = pl.ds(pl.multiple_of(c * ch, ch), ch)
        if forward:
            dtx_r, ex_r, cs_r, base, tri, y_buf, st_buf = dtxf_s, csxf_s, cs_s, 0, lower, yf_s, stf_s
        else:
            dtx_r, ex_r, cs_r, base, tri, y_buf, st_buf = dtxb_s, rcsxb_s, rcs_s, SSD_HPG, upper, yb_s, stb_s
        dtx, ex, cs = sub(dtx_r, s, gw)[rows, :], sub(ex_r, s, gw)[rows, :], sub(cs_r, s, LANE)[rows, :]
        y_buf, st_buf = sub(y_buf, s, gw), sub(st_buf, s, gw)
        xdt = sub(xs_s, s, gw)[rows, :] * dtx
        cc = sub(c_s, s, LANE)[rows, :]
        cb = _dot_nt(cc, sub(b_s, s, LANE)[rows, :])
        st = st_buf[...]
        y_off =_dot(cc, st.astype(BF16)) * jnp.exp(ex)
        cs_t = cs.T
        for j in range(SSD_GW // LANE):
            lanes = slice(j * LANE, (j + 1) * LANE)
            xp = xdt[:, lanes]
            acc = y_off[:, lanes]
            for k in range(LANE // SSD_P):
                hh = base + j * (LANE // SSD_P) + k
                diff = cs[:, hh:hh + 1] - cs_t[hh:hh + 1, :]
                decay = jnp.exp(jnp.where(tri, diff, -jnp.inf))
                g = (cb * decay).astype(BF16)
                xm = jnp.where(lo if k == 0 else jnp.logical_not(lo), xp, 0.0).astype(BF16)
                acc = acc + _dot(g, xm)
            y_buf[rows, lanes] = acc
        edge = ex[ch - 1:ch, :] if forward else ex[0:1, :]
        xd = (xdt * jnp.exp(edge - ex)).astype(BF16)
        st_buf[...] = jnp.exp(edge) * st + _dot(bt_s[s * nc + c], xd)

    def all_chains(i, carry):
        for s in range(n_sub):
            chunk_step(s, i, True)
            chunk_step(s, nc - 1 - i, False)
        return carry

    for s in range(n_sub):
        st_rows = slice(s * gw, (s + 1) * gw)
        sub(stf_s, s, gw)[...] = sf0_ref[0, 0, st_rows, :].T if latent else jnp.zeros((SSD_N, gw), F32)
        sub(stb_s, s, gw)[...] = sb0_ref[0, 0, st_rows, :].T if latent else jnp.zeros((SSD_N, gw), F32)
    lax.fori_loop(0, nc, all_chains, 0)
    if not latent:
        for s in range(n_sub):
            sf_ref[0, 0, s * gw:(s + 1) * gw, :] = sub(stf_s, s, gw)[...].T
            sb_ref[0, 0, s * gw:(s + 1) * gw, :] = sub(stb_s, s, gw)[...].T

    y = yf_s[...] + yb_s[...] + d_ref[...] * xs_s[...]
    y_ref[...] = y * _silu(z_ref[...])


def _ssd_call(p, lw, dims, latent, layer, carried, init_f=None, init_b=None):
    groups = dims.ssd_groups
    gw = SSD_GW
    if latent:
        nb, sl, row_off = dims.b_lat, dims.l_lat, dims.n_ctx // dims.l_lat
    else:
        nb, sl, row_off = dims.b_ctx, dims.l_ctx, 0
    nc = sl // SSD_CHUNK
    o = dims.offs
    starts_w = [(o.z, gw), (o.x, gw), (o.b, LANE), (o.c, LANE), (o.dt, LANE),
                (dims.ssd_width, LANE), (dims.ssd_width + groups * LANE, LANE)]
    n_sub = 2 if (groups % 2 == 0 and sl <= SSD_PAIR_MAX_ROWS
                  and all(st % (2 * w) == 0 for st, w in starts_w)) else 1
    wx, wl = n_sub * gw, n_sub * LANE
    zb, xb, bb, cb, db = o.z // wx, o.x // wx, o.b // wl, o.c // wl, o.dt // wl
    cwb, cwc = dims.ssd_width // wl, (dims.ssd_width + groups * LANE) // wl
    in_specs = [
        pl.BlockSpec((sl, wx), lambda b, g: (b + row_off, xb + g)),
        pl.BlockSpec((sl, wl), lambda b, g: (b + row_off, bb + g)),
        pl.BlockSpec((sl, wl), lambda b, g: (b + row_off, cb + g)),
        pl.BlockSpec((sl, wx), lambda b, g: (b + row_off, zb + g)),
        pl.BlockSpec((sl, wl), lambda b, g: (b + row_off, db + g)),
        pl.BlockSpec((3, wx), lambda b, g: (0, g)),
        pl.BlockSpec((1, wx), lambda b, g: (0, g)),
        pl.BlockSpec((3, wl), lambda b, g: (0, cwb + g)),
        pl.BlockSpec((1, wl), lambda b, g: (0, cwb + g)),
        pl.BlockSpec((3, wl), lambda b, g: (0, cwc + g)),
        pl.BlockSpec((1, wl), lambda b, g: (0, cwc + g)),
        pl.BlockSpec((None, 8, wl), lambda b, g: (layer, 0, g)),
        pl.BlockSpec((1, wx), lambda b, g: (0, g)),
    ]
    args = [p, p, p, p, p, lw["ssd_conv_w"], lw["ssd_conv_b"], lw["ssd_conv_w"], lw["ssd_conv_b"],
            lw["ssd_conv_w"], lw["ssd_conv_b"], lw["ssd_par"], lw["ssd_d"]]
    y_shape = jax.ShapeDtypeStruct((dims.n_ctx + dims.n_lat, dims.ssd_width), F32)
    y_spec = pl.BlockSpec((sl, wx), lambda b, g: (b + row_off, g))
    s_spec = pl.BlockSpec((1, 1, wx, SSD_N), lambda b, g: (b, layer, g, 0))
    if latent:
        in_specs += [s_spec, s_spec]
        args += [init_f, init_b]
        out_shape, out_specs = y_shape, y_spec
        aliases = {len(args): 0}
    else:
        s_shape = jax.ShapeDtypeStruct((nb, dims.depth, dims.ssd_width, SSD_N), F32)
        out_shape, out_specs = (y_shape, s_shape, s_shape), (y_spec, s_spec, s_spec)
        aliases = {len(args) + i: 1 + i for i in range(len(carried))}
    scratch = [
        pltpu.VMEM((sl, wx), F32), pltpu.VMEM((sl, wl), BF16), pltpu.VMEM((sl, wl), BF16),
        pltpu.VMEM((n_sub * nc, SSD_CHUNK, SSD_CHUNK), BF16),
        pltpu.VMEM((sl, wl), F32), pltpu.VMEM((sl, wl), F32),
        pltpu.VMEM((sl, wx), F32), pltpu.VMEM((sl, wx), F32),
        pltpu.VMEM((sl, wx), F32), pltpu.VMEM((sl, wx), F32),
        pltpu.VMEM((sl, wx), F32), pltpu.VMEM((sl, wx), F32),
        pltpu.VMEM((SSD_N, wx), F32), pltpu.VMEM((SSD_N, wx), F32),
    ]
    vmem = (2 * sl * (3 * wx + 3 * wl) * 4 + sl * (8 * wx + 4 * wl) * 4 + 8 * sl * wx * 4
            + (8 << 20))
    return pl.pallas_call(
        functools.partial(_ssd_kernel, seq_len=sl, latent=latent, n_alias=len(carried),
                          n_sub=n_sub),
        grid=(nb, groups // n_sub),
        in_specs=in_specs + [_ANY_SPEC] * len(carried),
        out_specs=out_specs,
        out_shape=out_shape,
        input_output_aliases=aliases,
        scratch_shapes=scratch,
        compiler_params=_cparams(("arbitrary", "arbitrary"), vmem),
        name="ssd_latent" if latent else "ssd_context",
    )(*args, *carried)


def _dft_mats(l):
    n = 3 * l // 2
    nh = n // 2
    k = np.arange(nh, dtype=np.int64)[:, None]
    t = np.arange(l, dtype=np.int64)[None, :]
    ang = 2.0 * np.pi * ((k * t) % n).astype(np.float64) / n
    fwd = np.concatenate([np.cos(ang), -np.sin(ang)], axis=0)
    fwd[nh, :] = np.where(np.arange(l) % 2 == 0, 1.0, -1.0)
    pos = (np.arange(l, dtype=np.int64) + l // 2)[:, None]
    kk = np.arange(nh, dtype=np.int64)[None, :]
    ang2 = 2.0 * np.pi * ((pos * kk) % n).astype(np.float64) / n
    inv = np.concatenate([2.0 * np.cos(ang2), -2.0 * np.sin(ang2)], axis=1) / n
    inv[:, 0] = 1.0 / n
    inv[:, nh] = np.where(pos[:, 0] % 2 == 0, 1.0, -1.0) / n

    def split(m):
        hi = m.astype(np.float32).astype(BF16)
        lo = (m - hi.astype(np.float64)).astype(np.float32).astype(BF16)
        return hi, lo

    return split(fwd) + split(inv)


def _hy_feats(l):
    off = jnp.arange(l, dtype=F32) - (l // 2)
    band = HY_FREQ_BASE ** (-jnp.arange(HY_BANDS, dtype=F32) / HY_BANDS)
    ang = off[:, None] * band
    feats = jnp.concatenate([off[:, None] / l, jnp.sin(ang), jnp.cos(ang)], axis=-1)
    return jnp.pad(feats, ((0, 0), (0, LANE - feats.shape[1])))


def _hy_filter_kernel(feats_ref, w1_ref, b1_ref, w2_ref, b2_ref, fr_ref, w3_ref, dl_ref,
                      fh_ref, fl_ref, o_ref, *, seq_len):
    fr = fr_ref[0]
    h1 = jnp.sin(fr[0:1] * (_dot_hp(feats_ref[...], w1_ref[0]) + b1_ref[0]))
    h2 = jnp.sin(fr[1:2] * (_dot_hp(h1, w2_ref[0]) + b2_ref[0]))
    filt = _dot_hp(h2, w3_ref[0])
    off = lax.broadcasted_iota(jnp.int32, filt.shape, 0).astype(F32) - (seq_len // 2)
    dist = jnp.abs(off) * (2.0 / seq_len)
    h = filt * jnp.exp(-dist * dl_ref[...])
    hh, hl = _split_bf16(h, 2)
    fh = fh_ref[...]
    o_ref[0, 0] = _dot(fh, hh) + _dot(fh, hl) + _dot(fl_ref[...], hh)


def _hy_filter_call(seq_len, hw, mats, dims):
    depth, hy = dims.depth, dims.hy_width
    cb = HY_CB
    ncb = hy // cb
    fh, fl = mats[0], mats[1]
    n = fh.shape[0]
    feats = _hy_feats(seq_len)
    deltas = jnp.linspace(HY_MIN_DECAY, HY_MAX_DECAY, hy, dtype=F32).reshape(1, hy)
    vmem = 2 * 2 * (n * seq_len * 2) + 6 * n * cb * 4 + (8 << 20)
    return pl.pallas_call(
        functools.partial(_hy_filter_kernel, seq_len=seq_len),
        grid=(depth, 2, ncb),
        in_specs=[
            pl.BlockSpec((seq_len, LANE), lambda l, o, c: (0, 0)),
            pl.BlockSpec((1, LANE, LANE), lambda l, o, c: (l, 0, 0)),
            pl.BlockSpec((1, 1, LANE), lambda l, o, c: (l, 0, 0)),
            pl.BlockSpec((1, LANE, LANE), lambda l, o, c: (l, 0, 0)),
            pl.BlockSpec((1, 1, LANE), lambda l, o, c: (l, 0, 0)),
            pl.BlockSpec((1, 2, LANE), lambda l, o, c: (l, 0, 0)),
            pl.BlockSpec((1, LANE, cb), lambda l, o, c: (l, 0, o * ncb + c)),
            pl.BlockSpec((1, cb), lambda l, o, c: (0, c)),
            pl.BlockSpec((n, seq_len), lambda l, o, c: (0, 0)),
            pl.BlockSpec((n, seq_len), lambda l, o, c: (0, 0)),
        ],
        out_specs=pl.BlockSpec((1, 1, n, cb), lambda l, o, c: (l, o, 0, c)),
        out_shape=jax.ShapeDtypeStruct((depth, 2, n, hy), F32),
        compiler_params=_cparams(("arbitrary",) * 3, vmem),
        name="hyena_filter_spectrum",
    )(feats, hw["w1"], hw["b1"], hw["w2"], hw["b2"], hw["freq"], hw["w3"], deltas, fh, fl)


def _hy_conv_kernel(*refs, seq_len, n_seq, n_alias):
    (x1_ref, x2_ref, v_ref, cw1_ref, cb1_ref, cw2_ref, cb2_ref, cwv_ref, cbv_ref,
     hf_ref, bias_ref, fh_ref, gh_ref) = refs[:13]
    o_ref = refs[13 + n_alias]
    nh = fh_ref.shape[0] // 2
    cb = o_ref.shape[1]
    t = lax.broadcasted_iota(jnp.int32, (seq_len, cb), 0)
    first = t == 0
    last = t == seq_len - 1
    bin0 = lax.broadcasted_iota(jnp.int32, (nh, cb), 0) == 0
    fh = fh_ref[...]
    gh = gh_ref[...]
    rows = [slice(s * seq_len, (s + 1) * seq_len) for s in range(n_seq)]
    gates = [(_conv3_rows(x1_ref[r, :], cw1_ref, cb1_ref, first, last),
              _conv3_rows(x2_ref[r, :], cw2_ref, cb2_ref, first, last)) for r in rows]
    zz = [_conv3_rows(v_ref[r, :], cwv_ref, cbv_ref, first, last) for r in rows]
    for o in range(2):
        hr, hi = hf_ref[0, o, :nh, :], hf_ref[0, o, nh:, :]
        uf = []
        for z in zz:
            zh, zl = _split_bf16(z, 2)
            uf.append(_dot(fh, zh) + _dot(fh, zl))
        ys = []
        for u in uf:
            ur, ui = u[:nh], u[nh:]
            ii = ui * hi
            yr = ur * hr - jnp.where(bin0, 0.0, ii)
            yi = jnp.where(bin0, ii, ur * hi + ui * hr)
            yh, yl = _split_bf16(jnp.concatenate([yr, yi], axis=0), 2)
            ys.append(_dot(gh, yh) + _dot(gh, yl))
        zz = [g[o] * (y + z * bias_ref[o:o + 1, :]) for g, y, z in zip(gates, ys, zz)]
    for r, z in zip(rows, zz):
        o_ref[r, :] = z


def _hy_conv_call(p, hf, lw, mats, dims, layer, latent, carried):
    hy = dims.hy_width
    cb = HY_CB
    ncb = hy // cb
    if latent:
        nb, sl, row_start = dims.b_lat, dims.l_lat, dims.n_ctx
    else:
        nb, sl, row_start = dims.b_ctx, dims.l_ctx, 0
    n_seq = max(n for n in (8, 4, 2, 1) if nb % n == 0 and (n == 1 or n * sl <= HY_ROWS_PER_STEP))
    rows = n_seq * sl
    row_off = row_start // rows
    n = mats[0].shape[0]
    const = lambda c, b: (0, 0)
    in_specs = [pl.BlockSpec((rows, cb), lambda c, b, k=k: (b + row_off, k * ncb + c)) for k in range(3)]
    for k in range(3):
        in_specs += [pl.BlockSpec((3, cb), lambda c, b, k=k: (0, k * ncb + c)),
                     pl.BlockSpec((1, cb), lambda c, b, k=k: (0, k * ncb + c))]
    once = pl.Buffered(1)
    in_specs += [
        pl.BlockSpec((1, 2, n, cb), lambda c, b: (layer, 0, 0, c), pipeline_mode=once),
        pl.BlockSpec((2, cb), lambda c, b: (0, c)),
        pl.BlockSpec((n, sl), const, pipeline_mode=once),
        pl.BlockSpec((sl, n), const, pipeline_mode=once),
    ]
    cw, cbias = lw["hy_conv_w"], lw["hy_conv_b"]
    vmem = (2 * (n * sl * 2) + 2 * n * cb * 4 + 2 * 4 * rows * cb * 4
            + n_seq * (6 * n + 8 * sl) * cb * 4 + (8 << 20))
    return pl.pallas_call(
        functools.partial(_hy_conv_kernel, seq_len=sl, n_seq=n_seq, n_alias=len(carried)),
        grid=(ncb, nb // n_seq),
        in_specs=in_specs + [_ANY_SPEC] * len(carried),
        out_specs=pl.BlockSpec((rows, cb), lambda c, b: (b + row_off, c)),
        out_shape=jax.ShapeDtypeStruct((dims.n_ctx + dims.n_lat, hy), F32),
        input_output_aliases={13 + i: i for i in range(len(carried))},
        compiler_params=_cparams(("arbitrary", "arbitrary"), vmem),
        name="hyena_conv_latent" if latent else "hyena_conv_context",
    )(p, p, p, cw, cbias, cw, cbias, cw, cbias, hf, lw["hy_bias"], mats[0], mats[2], *carried)


class _Offsets:
    def __init__(self, att_w, ssd_w, groups, hy_w):
        self.q, self.k, self.v = 0, att_w, 2 * att_w
        self.z = 3 * att_w
        self.x = self.z + ssd_w
        self.b = self.x + ssd_w
        self.c = self.b + groups * SSD_N
        self.dt = self.c + groups * SSD_N
        used = self.dt + groups * LANE
        self.total = PROJ_COL_ALIGN * ((used + PROJ_COL_ALIGN - 1) // PROJ_COL_ALIGN)


class _Dims:
    pass


def _make_dims(x_prompt, x_sample, cache_k, state_ssd_fwd, w_in, hy_norm_g, ffn_w_down):
    d = _Dims()
    d.b_ctx, d.l_ctx, d.d_model = x_prompt.shape
    d.b_lat, d.l_lat, _ = x_sample.shape
    d.n_ctx = d.b_ctx * d.l_ctx
    d.n_lat = d.b_lat * d.l_lat
    d.depth = w_in.shape[0]
    d.att_heads = cache_k.shape[3]
    d.att_width = d.att_heads * V_DIM
    d.ssd_heads = state_ssd_fwd.shape[2]
    d.ssd_width = d.ssd_heads * SSD_P
    d.ssd_groups = d.ssd_heads // SSD_HPG
    d.hy_width = hy_norm_g.shape[-1]
    d.d_ff = ffn_w_down.shape[1]
    d.offs = _Offsets(d.att_width, d.ssd_width, d.ssd_groups, d.hy_width)
    d.tm = d.l_lat
    assert d.l_lat % d.l_ctx == 0 and d.n_ctx % d.tm == 0 and d.l_ctx % 256 == 0
    assert d.l_lat & (d.l_lat - 1) == 0 and d.l_ctx & (d.l_ctx - 1) == 0
    assert d.hy_width % HY_CB == 0 and d.d_ff % FFN_TN == 0
    assert d.ssd_heads % SSD_HPG == 0
    return d


def _rope_tables(seq_len):
    rows = seq_len // GRID_W
    r, col = jnp.meshgrid(jnp.arange(rows), jnp.arange(GRID_W), indexing="ij")
    npair = QK_DIM // 4
    inv = ROPE_BASE ** (-jnp.arange(npair, dtype=F32) / npair)
    ang = jnp.concatenate([r.reshape(-1, 1).astype(F32) * inv, col.reshape(-1, 1).astype(F32) * inv], axis=-1)
    cos, sin = jnp.cos(ang), jnp.sin(ang)
    cos2 = jnp.repeat(cos, 2, axis=-1)
    sin2 = jnp.stack([-sin, sin], axis=-1).reshape(seq_len, QK_DIM)
    return jnp.tile(cos2, (1, 2)), jnp.tile(sin2, (1, 2))


def kernel(x_prompt, x_sample, cache_k, cache_v, state_ssd_fwd, state_ssd_bwd, c, c_ctx, norm1_g, norm2_g, w_ada, b_ada, w_in, att_lambda, att_subln_g, ssd_conv_w, ssd_conv_b, ssd_dt_bias, ssd_a_log, ssd_d, ssd_norm_g, hy_conv_w, hy_conv_b, hy_w1, hy_b1, hy_w2, hy_b2, hy_freq, hy_w3, hy_bias, hy_norm_g, w_out, ffn_w_up, ffn_conv_w, ffn_conv_b, ffn_w_down, final_norm_g):
    dims = _make_dims(x_prompt, x_sample, cache_k, state_ssd_fwd, w_in, hy_norm_g, ffn_w_down)
    d, depth, offs = dims.d_model, dims.depth, dims.offs
    n_ctx = dims.n_ctx

    mod_rows = 16 * ((1 + dims.b_lat + 15) // 16)
    cond = jnp.zeros((mod_rows, d), F32).at[0].set(c_ctx).at[1:1 + dims.b_lat].set(c)
    mod_all = _ada_call(cond, w_ada, b_ada).reshape(depth, mod_rows, 6, 1, d)

    pad_h = LANE - hy_w1.shape[-1]
    hw = {
        "w1": jnp.pad(hy_w1, ((0, 0), (0, LANE - hy_w1.shape[1]), (0, pad_h))),
        "b1": jnp.pad(hy_b1, ((0, 0), (0, pad_h)))[:, None, :],
        "w2": jnp.pad(hy_w2, ((0, 0), (0, pad_h), (0, pad_h))),
        "b2": jnp.pad(hy_b2, ((0, 0), (0, pad_h)))[:, None, :],
        "freq": jnp.pad(hy_freq, ((0, 0), (0, 0), (0, pad_h))),
        "w3": jnp.pad(hy_w3, ((0, 0), (0, pad_h), (0, 0))),
    }
    mats_ctx = _dft_mats(dims.l_ctx)
    mats_lat = _dft_mats(dims.l_lat)
    hf_ctx = _hy_filter_call(dims.l_ctx, hw, mats_ctx, dims)
    hf_lat = _hy_filter_call(dims.l_lat, hw, mats_lat, dims)

    cos_t, sin_t = _rope_tables(dims.l_lat)
    ck = cache_k.reshape(dims.b_lat, depth, cache_k.shape[2], dims.att_width)
    cv = cache_v.reshape(dims.b_lat, depth, cache_v.shape[2], dims.att_width)
    sf0 = state_ssd_fwd.reshape(dims.b_lat, depth, dims.ssd_width, SSD_N)
    sb0 = state_ssd_bwd.reshape(dims.b_lat, depth, dims.ssd_width, SSD_N)

    x = jnp.concatenate([x_prompt.reshape(n_ctx, d), x_sample.reshape(dims.n_lat, d)], axis=0)
    new_kv, new_st = (), ()

    dt0, dt_cols = offs.dt, 2 * dims.ssd_heads
    pick = np.zeros((dt_cols, dims.ssd_groups * LANE), np.float32)
    for g in range(dims.ssd_groups):
        for direction in range(2):
            for hh in range(SSD_HPG):
                pick[direction * dims.ssd_heads + g * SSD_HPG + hh,
                     g * LANE + direction * SSD_HPG + hh] = 1.0
    regroup = lambda t: jnp.einsum("...k,kn->...n", t, pick, precision=lax.Precision.HIGHEST)
    w_dt = regroup(w_in[:, :, dt0:dt0 + dt_cols])
    pad_cols = offs.total - dt0 - w_dt.shape[2]
    w_proj = jnp.concatenate([w_in[:, :, :dt0].astype(BF16), w_dt.astype(BF16),
                              jnp.zeros((depth, d, pad_cols), BF16)], axis=2)
    w_hy = w_in[:, :, dt0 + dt_cols:].astype(BF16)
    ssd_par = jnp.zeros((depth, 8, dims.ssd_groups * LANE), F32)
    ssd_par = ssd_par.at[:, 0].set(regroup(ssd_dt_bias.reshape(depth, dt_cols)))
    ssd_par = ssd_par.at[:, 1].set(regroup(ssd_a_log.reshape(depth, dt_cols)))
    w_o = w_out.astype(BF16)
    w_up = ffn_w_up.astype(BF16)
    w_down = ffn_w_down.astype(BF16)
    a_w, s_w = dims.att_width, dims.ssd_width
    if a_w % s_w == 0 and (a_w + s_w) % dims.hy_width == 0:
        out_terms = lambda l, att, y_ssd, y_hy: [
            (att, w_o, l, 0), (y_ssd, w_o, l, a_w // s_w), (y_hy, w_o, l, (a_w + s_w) // dims.hy_width)]
    else:
        out_terms = lambda l, att, y_ssd, y_hy: [
            (att, w_o[l:l + 1, :a_w], 0, 0), (y_ssd, w_o[l:l + 1, a_w:a_w + s_w], 0, 0),
            (y_hy, w_o[l:l + 1, a_w + s_w:], 0, 0)]

    for l in range(depth):
        lam_init = 0.8 - 0.6 * math.exp(-0.3 * l)
        mod = mod_all[l]
        lw = {
            "ssd_conv_w": ssd_conv_w[l], "ssd_conv_b": ssd_conv_b[l].reshape(1, -1),
            "ssd_par": ssd_par,
            "ssd_d": jnp.repeat(ssd_d[l], SSD_P).reshape(1, -1),
            "hy_conv_w": hy_conv_w[l], "hy_conv_b": hy_conv_b[l].reshape(1, -1), "hy_bias": hy_bias[l],
        }

        h = _norm_mod_call(x, norm1_g[l], mod, 0, 1, dims)
        p = _mm_call(h, w_proj, l, dims)
        p_hy = _mm_call(h, w_hy, l, dims)

        att, new_k, new_v = _attn_call(p, att_lambda[l], att_subln_g[l], dims, lam_init, False, l, new_kv)
        new_kv = (new_k, new_v)
        att = _attn_call(p, att_lambda[l], att_subln_g[l], dims, lam_init, True, l, (att,),
                         cache_k=ck, cache_v=cv, cos=cos_t, sin=sin_t)
        ys, new_sf, new_sb = _ssd_call(p, lw, dims, False, l, new_st)
        new_st = (new_sf, new_sb)
        ys = _ssd_call(p, lw, dims, True, l, (ys,), init_f=sf0, init_b=sb0)
        zz = _hy_conv_call(p_hy, hf_ctx, lw, mats_ctx, dims, l, False, ())
        zz = _hy_conv_call(p_hy, hf_lat, lw, mats_lat, dims, l, True, (zz,))

        y_ssd = _norm_call(ys, ssd_norm_g[l], BF16, name="ssd_norm")
        y_hy = _norm_call(zz, hy_norm_g[l], BF16, name="hyena_norm")
        x = _mm_res_call(out_terms(l, att, y_ssd, y_hy), x, mod, 2, dims,
                         (1024, 768, 512, 384, 256, 128))

        h2 = _norm_mod_call(x, norm2_g[l], mod, 3, 4, dims)
        act = _ffn_up_call(h2, w_up, l, ffn_conv_w[l], ffn_conv_b[l], dims)
        x = _mm_res_call([(act, w_down, l, 0)], x, mod, 5, dims, (DOWN_TILE, 256, 128),
                         tm=min(DOWN_TILE, dims.tm))

    y_prompt = _norm_call(x, final_norm_g, F32, 0, n_ctx, name="final_norm").reshape(x_prompt.shape)
    y_sample = _norm_call(x, final_norm_g, F32, n_ctx, dims.n_lat, name="final_norm").reshape(x_sample.shape)
    kv_shape = (dims.b_ctx, depth, dims.l_ctx, dims.att_heads, V_DIM)
    st_shape = (dims.b_ctx, depth, dims.ssd_heads, SSD_P, SSD_N)
    return (y_prompt, y_sample, new_kv[0].reshape(kv_shape), new_kv[1].reshape(kv_shape),
            new_st[0].reshape(st_shape), new_st[1].reshape(st_shape))
```

```python
import functools
import math

import numpy as np
import jax
import jax.numpy as jnp
from jax import lax
from jax.experimental import pallas as pl
from jax.experimental.pallas import tpu as pltpu

F32 = jnp.float32
BF16 = jnp.bfloat16

LANE = 128
V7X_VMEM_BYTES = 64 * 1024 * 1024
VMEM_CAP = V7X_VMEM_BYTES - 6 * 1024 * 1024

NORM_EPS = 1e-6
GRID_W = 64
ROPE_BASE = 10000.0
QK_DIM = 64
V_DIM = 128
SSD_P = 64
SSD_N = 128
SSD_CHUNK = 128
SSD_HPG = 6
SSD_GW = SSD_HPG * SSD_P
HY_BANDS = 16
HY_FREQ_BASE = 10000.0
HY_MIN_DECAY = -math.log(1e-2) / 1.5
HY_MAX_DECAY = -math.log(1e-2) / 0.3
HY_CB = 256
FFN_TN = 256
FFN_BLOCKS_PER_STEP = 3
PROJ_COL_ALIGN = 1024
ATT_TQ = 512
DOWN_TILE = 512
SSD_PAIR_MAX_ROWS = 512
HY_ROWS_PER_STEP = 2048

_ANY_SPEC = pl.BlockSpec(memory_space=pl.ANY)


def _cparams(sem, vmem_bytes):
    limit = int(min(VMEM_CAP, max(vmem_bytes, 16 * 1024 * 1024)))
    return pltpu.CompilerParams(dimension_semantics=sem, vmem_limit_bytes=limit)


def _pick(n, candidates):
    for c in candidates:
        if n % c == 0:
            return c
    raise ValueError(f"no block size in {candidates} divides {n}")


def _dot(a, b):
    return jnp.dot(a, b, preferred_element_type=F32)


def _dot_nt(a, b):
    return lax.dot_general(a, b, (((1,), (1,)), ((), ())), preferred_element_type=F32)


def _split_bf16(x, n):
    parts = []
    r = x
    for _ in range(n):
        h = r.astype(BF16)
        parts.append(h)
        r = r - h.astype(F32)
    return parts


def _dot_f32_lhs(a, b_bf16, terms=3):
    return sum(_dot(p, b_bf16) for p in _split_bf16(a, terms))


def _dot_f32_rhs(a_bf16, b):
    return sum(_dot(a_bf16, p) for p in _split_bf16(b, 3))


def _dot_hp(a, b):
    ah, al = _split_bf16(a, 2)
    bh, bl = _split_bf16(b, 2)
    return _dot(ah, bh) + _dot(ah, bl) + _dot(al, bh)


def _silu(x):
    return x * jax.nn.sigmoid(x)


def _conv3_rows(u, w_ref, b_ref, first, last):
    n = u.shape[0]
    prev = jnp.where(first, 0.0, pltpu.roll(u, 1, 0))
    nxt = jnp.where(last, 0.0, pltpu.roll(u, n - 1, 0))
    return prev * w_ref[0:1, :] + b_ref[...] + u * w_ref[1:2, :] + nxt * w_ref[2:3, :]


def _ada_kernel(c_ref, w_ref, b_ref, o_ref):
    s = _silu(c_ref[...]).astype(BF16)
    o_ref[0] = _dot(s, w_ref[0].astype(BF16)) + b_ref[0]


def _ada_call(cond, w_ada, b_ada):
    depth, d, n = w_ada.shape
    rows = cond.shape[0]
    tn = _pick(n, (512, 256, 128))
    vmem = 2 * (d * tn * 4) + d * tn * 2 + 4 * rows * (d + 2 * tn) * 4 + (4 << 20)
    return pl.pallas_call(
        _ada_kernel,
        grid=(depth, n // tn),
        in_specs=[
            pl.BlockSpec((rows, d), lambda l, j: (0, 0)),
            pl.BlockSpec((1, d, tn), lambda l, j: (l, 0, j)),
            pl.BlockSpec((1, 1, tn), lambda l, j: (l, 0, j)),
        ],
        out_specs=pl.BlockSpec((1, rows, tn), lambda l, j: (l, 0, j)),
        out_shape=jax.ShapeDtypeStruct((depth, rows, n), F32),
        compiler_params=_cparams(("arbitrary", "arbitrary"), vmem),
        name="ada_mod",
    )(cond, w_ada, b_ada.reshape(depth, 1, n))


def _norm_mod_kernel(x_ref, g_ref, sc_ref, sh_ref, o_ref):
    x = x_ref[...]
    r = lax.rsqrt(jnp.mean(x * x, axis=-1, keepdims=True) + NORM_EPS)
    y = x * r * g_ref[...]
    o_ref[...] = (y * (1.0 + sc_ref[0, 0]) + sh_ref[0, 0]).astype(o_ref.dtype)


def _norm_kernel(x_ref, g_ref, o_ref):
    x = x_ref[...]
    r = lax.rsqrt(jnp.mean(x * x, axis=-1, keepdims=True) + NORM_EPS)
    o_ref[...] = (x * r * g_ref[...]).astype(o_ref.dtype)


def _mod_row(i, tm, dims):
    start = i * tm
    lat = jnp.maximum(start - dims.n_ctx, 0) // dims.l_lat
    return jnp.where(start >= dims.n_ctx, lat + 1, 0)


def _norm_mod_call(x, g, mod, sh_idx, sc_idx, dims):
    m, d = x.shape
    tr = 256
    row = functools.partial(_mod_row, tm=tr, dims=dims)
    vmem = 2 * tr * d * (4 + 2) + 3 * tr * d * 4 + (4 << 20)
    return pl.pallas_call(
        _norm_mod_kernel,
        grid=(m // tr,),
        in_specs=[
            pl.BlockSpec((tr, d), lambda i: (i, 0)),
            pl.BlockSpec((1, d), lambda i: (0, 0)),
            pl.BlockSpec((1, 1, 1, d), lambda i: (row(i), sc_idx, 0, 0)),
            pl.BlockSpec((1, 1, 1, d), lambda i: (row(i), sh_idx, 0, 0)),
        ],
        out_specs=pl.BlockSpec((tr, d), lambda i: (i, 0)),
        out_shape=jax.ShapeDtypeStruct((m, d), BF16),
        compiler_params=_cparams(("arbitrary",), vmem),
        name="norm_mod",
    )(x, g.reshape(1, d), mod, mod)


def _norm_call(x, g, out_dtype, row_start=0, rows=None, name="norm"):
    m, d = x.shape
    rows = m if rows is None else rows
    tr = 256
    off = row_start // tr
    vmem = 2 * tr * d * 8 + 3 * tr * d * 4 + (4 << 20)
    return pl.pallas_call(
        _norm_kernel,
        grid=(rows // tr,),
        in_specs=[
            pl.BlockSpec((tr, d), lambda i: (i + off, 0)),
            pl.BlockSpec((1, d), lambda i: (0, 0)),
        ],
        out_specs=pl.BlockSpec((tr, d), lambda i: (i, 0)),
        out_shape=jax.ShapeDtypeStruct((rows, d), out_dtype),
        compiler_params=_cparams(("arbitrary",), vmem),
        name=name,
    )(x, g.reshape(1, d))


def _mm_kernel(a_ref, b_ref, o_ref):
    o_ref[...] = _dot(a_ref[...], b_ref[...]).astype(o_ref.dtype)


def _mm_res_kernel(*refs, n_terms):
    x_ref, g_ref, o_ref = refs[2 * n_terms:]
    acc = _dot(refs[0][...], refs[1][...])
    for i in range(1, n_terms):
        acc = acc + _dot(refs[2 * i][...], refs[2 * i + 1][...])
    o_ref[...] = x_ref[...] + g_ref[0, 0] * acc


def _mm_call(a, w_stack, layer, dims, out_dtype=F32):
    m, k = a.shape
    n = w_stack.shape[2]
    b = w_stack
    tm = dims.tm
    tn = _pick(n, (1024, 768, 512, 384, 256, 128))
    vmem = 2 * (tm * k * 2 + k * tn * 2 + tm * tn * 4) + tm * tn * 4 + (4 << 20)
    return pl.pallas_call(
        _mm_kernel,
        grid=(m // tm, n // tn),
        in_specs=[
            pl.BlockSpec((tm, k), lambda i, j: (i, 0)),
            pl.BlockSpec((None, k, tn), lambda i, j: (layer, 0, j)),
        ],
        out_specs=pl.BlockSpec((tm, tn), lambda i, j: (i, j)),
        out_shape=jax.ShapeDtypeStruct((m, n), out_dtype),
        compiler_params=_cparams(("arbitrary", "arbitrary"), vmem),
        name="proj",
    )(a, b)


def _mm_res_call(terms, x, mod, g_idx, dims, tn_candidates, tm=None):
    m, n = x.shape
    k = sum(t[0].shape[1] for t in terms)
    tm = dims.tm if tm is None else tm
    tn = _pick(n, tn_candidates)
    row = functools.partial(_mod_row, tm=tm, dims=dims)
    vmem = 2 * (tm * k * 2 + k * tn * 2 + 2 * tm * tn * 4) + tm * tn * 4 + (4 << 20)
    in_specs, args = [], []
    for a, w_stack, layer, row_block in terms:
        ka = a.shape[1]
        in_specs.append(pl.BlockSpec((tm, ka), lambda i, j: (i, 0)))
        in_specs.append(pl.BlockSpec((None, ka, tn), lambda i, j, layer=layer, rb=row_block: (layer, rb, j)))
        args += [a, w_stack]
    in_specs += [pl.BlockSpec((tm, tn), lambda i, j: (i, j)),
                 pl.BlockSpec((1, 1, 1, tn), lambda i, j: (row(i), g_idx, 0, j))]
    return pl.pallas_call(
        functools.partial(_mm_res_kernel, n_terms=len(terms)),
        grid=(m // tm, n // tn),
        in_specs=in_specs,
        out_specs=pl.BlockSpec((tm, tn), lambda i, j: (i, j)),
        out_shape=jax.ShapeDtypeStruct((m, n), F32),
        compiler_params=_cparams(("arbitrary", "arbitrary"), vmem),
        name="proj_residual",
    )(*args, x, mod)


def _ffn_up_kernel(*refs, dims, nblk, n_alias):
    a_ref = refs[0]
    b_refs = refs[1:1 + 2 * nblk]
    cw_refs = refs[1 + 2 * nblk:1 + 4 * nblk]
    cb_refs = refs[1 + 4 * nblk:1 + 6 * nblk]
    o_ref = refs[1 + 6 * nblk + n_alias]
    a = a_ref[...]
    tm = a.shape[0]
    tn = b_refs[0].shape[1]
    seq = jnp.where(pl.program_id(0) * tm < dims.n_ctx, dims.l_ctx, dims.l_lat)
    t = lax.broadcasted_iota(jnp.int32, (tm, tn), 0) & (seq - 1)
    first = t == 0
    last = t == seq - 1
    for q in range(nblk):
        g = _conv3_rows(_dot(a, b_refs[2 * q][...]), cw_refs[2 * q], cb_refs[2 * q], first, last)
        v = _conv3_rows(_dot(a, b_refs[2 * q + 1][...]), cw_refs[2 * q + 1], cb_refs[2 * q + 1],
                        first, last)
        o_ref[:, q * tn:(q + 1) * tn] = (_silu(g) * v).astype(o_ref.dtype)


def _ffn_up_call(h, w_up, layer, conv_w, conv_b, dims):
    m, k = h.shape
    dff = w_up.shape[2] // 2
    tm = dims.tm
    tn = FFN_TN
    nb = dff // tn
    conv_b = conv_b.reshape(1, -1)

    def run(first_block, steps, nblk, carried):
        cols = []
        for q in range(nblk):
            cols += [lambda i, j, q=q: (0, first_block + nblk * j + q),
                     lambda i, j, q=q: (0, nb + first_block + nblk * j + q)]
        in_specs = [pl.BlockSpec((tm, k), lambda i, j: (i, 0), pipeline_mode=pl.Buffered(1))]
        in_specs += [pl.BlockSpec((None, k, tn), lambda i, j, c=c: (layer,) + c(i, j)) for c in cols]
        in_specs += [pl.BlockSpec((3, tn), c) for c in cols]
        in_specs += [pl.BlockSpec((1, tn), c) for c in cols]
        n_in = len(in_specs)
        vmem = (tm * k * 2 + 2 * nblk * (2 * k * tn * 2 + tm * tn * 2) + 16 * tm * tn * 4 + (4 << 20))
        return pl.pallas_call(
            functools.partial(_ffn_up_kernel, dims=dims, nblk=nblk, n_alias=len(carried)),
            grid=(m // tm, steps),
            in_specs=in_specs + [_ANY_SPEC] * len(carried),
            out_specs=pl.BlockSpec((tm, nblk * tn), lambda i, j: (i, first_block // nblk + j)),
            out_shape=jax.ShapeDtypeStruct((m, dff), BF16),
            input_output_aliases={n_in + i: i for i in range(len(carried))},
            compiler_params=_cparams(("arbitrary", "arbitrary"), vmem),
            name="ffn_up_conv_gate",
        )(h, *([w_up] * (2 * nblk)), *([conv_w] * (2 * nblk)), *([conv_b] * (2 * nblk)), *carried)

    nblk = min(FFN_BLOCKS_PER_STEP, nb)
    main_steps = nb // nblk
    act = run(0, main_steps, nblk, ())
    if main_steps * nblk < nb:
        act = run(main_steps * nblk, nb - main_steps * nblk, 1, (act,))
    return act


def _rope(x, cos, sin_signed):
    lane = lax.broadcasted_iota(jnp.int32, x.shape, 1)
    n = x.shape[1]
    swapped = jnp.where((lane & 1) == 0, pltpu.roll(x, n - 1, 1), pltpu.roll(x, 1, 1))
    return x * cos + swapped * sin_signed


def _softmax_parts(s, sc):
    m = jnp.max(s, axis=-1, keepdims=True)
    if sc is not None:
        m = jnp.maximum(m, jnp.max(sc, axis=-1, keepdims=True))
    e = jnp.exp(s - m)
    tot = jnp.sum(e, axis=-1, keepdims=True)
    ec = None
    if sc is not None:
        ec = jnp.exp(sc - m)
        tot = tot + jnp.sum(ec, axis=-1, keepdims=True)
    return e, ec, 1.0 / tot


def _attn_kernel(*refs, latent, heads, tq, lq, lam_init, n_alias):
    if latent:
        q_ref, k_ref, v_ref, kc_ref, vc_ref, cos_ref, sin_ref, lv_ref, g_ref = refs[:9]
        o_ref, kb_s, vb_s, kcb_s, vcb_s = refs[9 + n_alias:]
    else:
        q_ref, k_ref, v_ref, lv_ref, g_ref = refs[:5]
        o_ref, ko_ref, vo_ref = refs[5 + n_alias:]
        ko_ref[0, 0] = k_ref[...]
        vo_ref[0, 0] = v_ref[...]
    lv = lv_ref[...]
    lam = (jnp.exp(jnp.sum(lv[0:1] * lv[1:2], axis=-1, keepdims=True))
           - jnp.exp(jnp.sum(lv[2:3] * lv[3:4], axis=-1, keepdims=True)) + lam_init)
    gain = g_ref[...] * (1.0 - lam_init)
    scale = QK_DIM ** -0.5
    lo = lax.broadcasted_iota(jnp.int32, (tq, V_DIM), 1) < QK_DIM

    if not latent:
        hs = [slice(h * V_DIM, (h + 1) * V_DIM) for h in range(heads)]
        kb = [k_ref[:, c].astype(BF16) for c in hs]
        vb = [v_ref[:, c].astype(BF16) for c in hs]
        qs = [q_ref[:, c] * scale for c in hs]
        q1 = [jnp.where(lo, q, 0.0).astype(BF16) for q in qs]
        q2 = [jnp.where(lo, 0.0, q).astype(BF16) for q in qs]
        p1 = [_softmax_parts(_dot_nt(a, b), None) for a, b in zip(q1, kb)]
        p2 = [_softmax_parts(_dot_nt(a, b), None) for a, b in zip(q2, kb)]
        ws = [(a[0] * a[2] - b[0] * (b[2] * lam)).astype(BF16) for a, b in zip(p1, p2)]
        outs = [_dot(w, v) for w, v in zip(ws, vb)]
        for c, o in zip(hs, outs):
            rr = lax.rsqrt(jnp.mean(o * o, axis=-1, keepdims=True) + NORM_EPS)
            o_ref[:, c] = (o * rr * gain).astype(o_ref.dtype)
        return

    for h in range(heads):
        cols = slice(h * V_DIM, (h + 1) * V_DIM)
        k = k_ref[:, cols]
        if latent:
            k = _rope(k, cos_ref[...], sin_ref[...])
            kcb_s[...] = kc_ref[0, 0, :, cols].astype(BF16)
            vcb_s[...] = vc_ref[0, 0, :, cols].astype(BF16)
        kb_s[...] = k.astype(BF16)
        vb_s[...] = v_ref[:, cols].astype(BF16)

        blocks = [slice(i * tq, (i + 1) * tq) for i in range(lq // tq)]
        qs = [_rope(q_ref[r, cols], cos_ref[r, :], sin_ref[r, :]) * scale for r in blocks]
        q1 = [jnp.where(lo, q, 0.0).astype(BF16) for q in qs]
        q2 = [jnp.where(lo, 0.0, q).astype(BF16) for q in qs]
        kb = kb_s[...]
        kcb = kcb_s[...]
        p1 = [_softmax_parts(_dot_nt(a, kb), _dot_nt(a, kcb)) for a in q1]
        p2 = [_softmax_parts(_dot_nt(a, kb), _dot_nt(a, kcb)) for a in q2]
        outs = []
        for a, b in zip(p1, p2):
            r1, r2 = a[2], b[2] * lam
            o = _dot((a[0] * r1 - b[0] * r2).astype(BF16), vb_s[...])
            outs.append(o + _dot((a[1] * r1 - b[1] * r2).astype(BF16), vcb_s[...]))
        for r, o in zip(blocks, outs):
            rr = lax.rsqrt(jnp.mean(o * o, axis=-1, keepdims=True) + NORM_EPS)
            o_ref[r, cols] = (o * rr * gain).astype(o_ref.dtype)


def _attn_call(p, lv, g, dims, lam_init, latent, layer, carried, cache_k=None, cache_v=None,
               cos=None, sin=None):
    n_heads = dims.att_heads
    if latent:
        nb, lq, heads, row_off = dims.b_lat, dims.l_lat, 1, dims.n_ctx // dims.l_lat
    else:
        nb, lq, heads, row_off = dims.b_ctx, dims.l_ctx, _pick(n_heads, (6, 4, 2, 1)), 0
        assert lq <= ATT_TQ
    tq = min(lq, ATT_TQ)
    hb = n_heads // heads
    w = heads * V_DIM
    q_spec = pl.BlockSpec((lq, w), lambda b, h: (b + row_off, h))
    k_spec = pl.BlockSpec((lq, w), lambda b, h: (b + row_off, hb + h))
    v_spec = pl.BlockSpec((lq, w), lambda b, h: (b + row_off, 2 * hb + h))
    small = [pl.BlockSpec((4, QK_DIM), lambda b, h: (0, 0)),
             pl.BlockSpec((1, V_DIM), lambda b, h: (0, 0))]
    scratch = []
    if latent:
        scratch += [pltpu.VMEM((lq, V_DIM), BF16), pltpu.VMEM((lq, V_DIM), BF16)]
        past = cache_k.shape[2]
        c_spec = pl.BlockSpec((1, 1, past, w), lambda b, h: (b, layer, 0, h))
        t_spec = pl.BlockSpec((lq, V_DIM), lambda b, h: (0, 0))
        in_specs = [q_spec, k_spec, v_spec, c_spec, c_spec, t_spec, t_spec] + small
        args = (p, p, p, cache_k, cache_v, cos, sin, lv, g.reshape(1, V_DIM))
        scratch += [pltpu.VMEM((past, V_DIM), BF16), pltpu.VMEM((past, V_DIM), BF16)]
        lk = lq + past
    else:
        in_specs = [q_spec, k_spec, v_spec] + small
        args = (p, p, p, lv, g.reshape(1, V_DIM))
        lk = lq
    n_in = len(args)
    att_shape = jax.ShapeDtypeStruct((dims.n_ctx + dims.n_lat, n_heads * V_DIM), BF16)
    att_spec = pl.BlockSpec((lq, w), lambda b, h: (b + row_off, h))
    if latent:
        out_shape, out_specs = att_shape, att_spec
        aliases = {n_in: 0}
    else:
        kv_shape = jax.ShapeDtypeStruct((nb, dims.depth, lq, n_heads * V_DIM), F32)
        kv_spec = pl.BlockSpec((1, 1, lq, w), lambda b, h: (b, layer, 0, h))
        out_shape, out_specs = (att_shape, kv_shape, kv_shape), (att_spec, kv_spec, kv_spec)
        aliases = {n_in + i: 1 + i for i in range(len(carried))}
    vmem = 2 * 6 * lq * w * 4 + 12 * heads * tq * lk * 4 + (8 << 20)
    return pl.pallas_call(
        functools.partial(_attn_kernel, latent=latent, heads=heads, tq=tq, lq=lq,
                          lam_init=lam_init, n_alias=len(carried)),
        grid=(nb, hb),
        in_specs=in_specs + [_ANY_SPEC] * len(carried),
        out_specs=out_specs,
        out_shape=out_shape,
        input_output_aliases=aliases,
        scratch_shapes=scratch,
        compiler_params=_cparams(("arbitrary", "arbitrary"), vmem),
        name="diff_attention_latent" if latent else "diff_attention_context",
    )(*args, *carried)


def _softplus(x):
    return jnp.maximum(x, 0.0) + jnp.log1p(jnp.exp(-jnp.abs(x)))


def _ssd_kernel(*refs, seq_len, latent, n_alias, n_sub):
    it = iter(refs)
    x_ref, b_ref, c_ref, z_ref, dt_ref = (next(it) for _ in range(5))
    cwx_ref, cbx_ref, cwb_ref, cbb_ref, cwc_ref, cbc_ref = (next(it) for _ in range(6))
    par_ref, d_ref = next(it), next(it)
    if latent:
        sf0_ref, sb0_ref = next(it), next(it)
    for _ in range(n_alias):
        next(it)
    if latent:
        y_ref = next(it)
    else:
        y_ref, sf_ref, sb_ref = next(it), next(it), next(it)
    (xs_s, b_s, c_s, bt_s, cs_s, rcs_s, dtxf_s, dtxb_s, csxf_s, rcsxb_s,
     yf_s, yb_s, stf_s, stb_s) = it

    ch = SSD_CHUNK
    nc = seq_len // ch
    gw = SSD_GW

    def sub(ref, s, w):
        return ref.at[:, s * w:(s + 1) * w]

    def conv_silu(u_ref, w_ref, bias_ref):
        u = u_ref[...]
        t = lax.broadcasted_iota(jnp.int32, u.shape, 0)
        return _silu(_conv3_rows(u, w_ref, bias_ref, t == 0, t == seq_len - 1))

    ri = lax.broadcasted_iota(jnp.int32, (ch, ch), 0)
    ci = lax.broadcasted_iota(jnp.int32, (ch, ch), 1)
    lower = ri >= ci
    upper = ri <= ci
    t_low = jnp.where(lower, 1.0, 0.0).astype(BF16)
    t_up = jnp.where(upper, 1.0, 0.0).astype(BF16)
    er = lax.broadcasted_iota(jnp.int32, (LANE, gw), 0)
    eh = lax.broadcasted_iota(jnp.int32, (LANE, gw), 1) >> int(math.log2(SSD_P))
    exp_f = jnp.where(er == eh, 1.0, 0.0).astype(BF16)
    exp_b = jnp.where(er == eh + SSD_HPG, 1.0, 0.0).astype(BF16)
    lo = lax.broadcasted_iota(jnp.int32, (ch, LANE), 1) < SSD_P

    for s in range(n_sub):
        sub(xs_s, s, gw)[...] = conv_silu(sub(x_ref, s, gw), sub(cwx_ref, s, gw), sub(cbx_ref, s, gw))
        bm = conv_silu(sub(b_ref, s, LANE), sub(cwb_ref, s, LANE), sub(cbb_ref, s, LANE))
        sub(b_s, s, LANE)[...] = bm.astype(BF16)
        for c in range(nc):
            bt_s[s * nc + c] = bm[c * ch:(c + 1) * ch, :].T.astype(BF16)
        sub(c_s, s, LANE)[...] = conv_silu(sub(c_ref, s, LANE), sub(cwc_ref, s, LANE),
                                           sub(cbc_ref, s, LANE)).astype(BF16)
        par = sub(par_ref, s, LANE)
        dt = _softplus(sub(dt_ref, s, LANE)[...] + par[0:1, :])
        dta = dt * (-jnp.exp(par[1:2, :]))
        cs_v, rcs_v = sub(cs_s, s, LANE), sub(rcs_s, s, LANE)
        for c in range(nc):
            blk = dta[c * ch:(c + 1) * ch, :]
            cs_v[c * ch:(c + 1) * ch, :] = _dot_f32_rhs(t_low, blk)
            rcs_v[c * ch:(c + 1) * ch, :] = _dot_f32_rhs(t_up, blk)
        sub(dtxf_s, s, gw)[...] = _dot_f32_lhs(dt, exp_f, terms=2)
        sub(dtxb_s, s, gw)[...] = _dot_f32_lhs(dt, exp_b, terms=2)
        sub(csxf_s, s, gw)[...] = _dot_f32_lhs(cs_v[...], exp_f)
        sub(rcsxb_s, s, gw)[...] = _dot_f32_lhs(rcs_v[...], exp_b)

    def chunk_step(s, c, forward):
        rows = pl.ds(pl.multiple_of(c * ch, ch), ch)
        if forward:
            dtx_r, ex_r, cs_r, base, tri, y_buf, st_buf = dtxf_s, csxf_s, cs_s, 0, lower, yf_s, stf_s
        else:
            dtx_r, ex_r, cs_r, base, tri, y_buf, st_buf = dtxb_s, rcsxb_s, rcs_s, SSD_HPG, upper, yb_s, stb_s
        dtx, ex, cs = sub(dtx_r, s, gw)[rows, :], sub(ex_r, s, gw)[rows, :], sub(cs_r, s, LANE)[rows, :]
        y_buf, st_buf = sub(y_buf, s, gw), sub(st_buf, s, gw)
        xdt = sub(xs_s, s, gw)[rows, :] * dtx
        cc = sub(c_s, s, LANE)[rows, :]
        cb = _dot_nt(cc, sub(b_s, s, LANE)[rows, :])
        st = st_buf[...]
        y_off =_dot(cc, st.astype(BF16)) * jnp.exp(ex)
        cs_t = cs.T
        for j in range(SSD_GW // LANE):
            lanes = slice(j * LANE, (j + 1) * LANE)
            xp = xdt[:, lanes]
            acc = y_off[:, lanes]
            for k in range(LANE // SSD_P):
                hh = base + j * (LANE // SSD_P) + k
                diff = cs[:, hh:hh + 1] - cs_t[hh:hh + 1, :]
                decay = jnp.exp(jnp.where(tri, diff, -jnp.inf))
                g = (cb * decay).astype(BF16)
                xm = jnp.where(lo if k == 0 else jnp.logical_not(lo), xp, 0.0).astype(BF16)
                acc = acc + _dot(g, xm)
            y_buf[rows, lanes] = acc
        edge = ex[ch - 1:ch, :] if forward else ex[0:1, :]
        xd = (xdt * jnp.exp(edge - ex)).astype(BF16)
        st_buf[...] = jnp.exp(edge) * st + _dot(bt_s[s * nc + c], xd)

    def all_chains(i, carry):
        for s in range(n_sub):
            chunk_step(s, i, True)
            chunk_step(s, nc - 1 - i, False)
        return carry

    for s in range(n_sub):
        st_rows = slice(s * gw, (s + 1) * gw)
        sub(stf_s, s, gw)[...] = sf0_ref[0, 0, st_rows, :].T if latent else jnp.zeros((SSD_N, gw), F32)
        sub(stb_s, s, gw)[...] = sb0_ref[0, 0, st_rows, :].T if latent else jnp.zeros((SSD_N, gw), F32)
    lax.fori_loop(0, nc, all_chains, 0)
    if not latent:
        for s in range(n_sub):
            sf_ref[0, 0, s * gw:(s + 1) * gw, :] = sub(stf_s, s, gw)[...].T
            sb_ref[0, 0, s * gw:(s + 1) * gw, :] = sub(stb_s, s, gw)[...].T

    y = yf_s[...] + yb_s[...] + d_ref[...] * xs_s[...]
    y_ref[...] = y * _silu(z_ref[...])


def _ssd_call(p, lw, dims, latent, layer, carried, init_f=None, init_b=None):
    groups = dims.ssd_groups
    gw = SSD_GW
    if latent:
        nb, sl, row_off = dims.b_lat, dims.l_lat, dims.n_ctx // dims.l_lat
    else:
        nb, sl, row_off = dims.b_ctx, dims.l_ctx, 0
    nc = sl // SSD_CHUNK
    o = dims.offs
    starts_w = [(o.z, gw), (o.x, gw), (o.b, LANE), (o.c, LANE), (o.dt, LANE),
                (dims.ssd_width, LANE), (dims.ssd_width + groups * LANE, LANE)]
    n_sub = 2 if (groups % 2 == 0 and sl <= SSD_PAIR_MAX_ROWS
                  and all(st % (2 * w) == 0 for st, w in starts_w)) else 1
    wx, wl = n_sub * gw, n_sub * LANE
    zb, xb, bb, cb, db = o.z // wx, o.x // wx, o.b // wl, o.c // wl, o.dt // wl
    cwb, cwc = dims.ssd_width // wl, (dims.ssd_width + groups * LANE) // wl
    in_specs = [
        pl.BlockSpec((sl, wx), lambda b, g: (b + row_off, xb + g)),
        pl.BlockSpec((sl, wl), lambda b, g: (b + row_off, bb + g)),
        pl.BlockSpec((sl, wl), lambda b, g: (b + row_off, cb + g)),
        pl.BlockSpec((sl, wx), lambda b, g: (b + row_off, zb + g)),
        pl.BlockSpec((sl, wl), lambda b, g: (b + row_off, db + g)),
        pl.BlockSpec((3, wx), lambda b, g: (0, g)),
        pl.BlockSpec((1, wx), lambda b, g: (0, g)),
        pl.BlockSpec((3, wl), lambda b, g: (0, cwb + g)),
        pl.BlockSpec((1, wl), lambda b, g: (0, cwb + g)),
        pl.BlockSpec((3, wl), lambda b, g: (0, cwc + g)),
        pl.BlockSpec((1, wl), lambda b, g: (0, cwc + g)),
        pl.BlockSpec((None, 8, wl), lambda b, g: (layer, 0, g)),
        pl.BlockSpec((1, wx), lambda b, g: (0, g)),
    ]
    args = [p, p, p, p, p, lw["ssd_conv_w"], lw["ssd_conv_b"], lw["ssd_conv_w"], lw["ssd_conv_b"],
            lw["ssd_conv_w"], lw["ssd_conv_b"], lw["ssd_par"], lw["ssd_d"]]
    y_shape = jax.ShapeDtypeStruct((dims.n_ctx + dims.n_lat, dims.ssd_width), F32)
    y_spec = pl.BlockSpec((sl, wx), lambda b, g: (b + row_off, g))
    s_spec = pl.BlockSpec((1, 1, wx, SSD_N), lambda b, g: (b, layer, g, 0))
    if latent:
        in_specs += [s_spec, s_spec]
        args += [init_f, init_b]
        out_shape, out_specs = y_shape, y_spec
        aliases = {len(args): 0}
    else:
        s_shape = jax.ShapeDtypeStruct((nb, dims.depth, dims.ssd_width, SSD_N), F32)
        out_shape, out_specs = (y_shape, s_shape, s_shape), (y_spec, s_spec, s_spec)
        aliases = {len(args) + i: 1 + i for i in range(len(carried))}
    scratch = [
        pltpu.VMEM((sl, wx), F32), pltpu.VMEM((sl, wl), BF16), pltpu.VMEM((sl, wl), BF16),
        pltpu.VMEM((n_sub * nc, SSD_CHUNK, SSD_CHUNK), BF16),
        pltpu.VMEM((sl, wl), F32), pltpu.VMEM((sl, wl), F32),
        pltpu.VMEM((sl, wx), F32), pltpu.VMEM((sl, wx), F32),
        pltpu.VMEM((sl, wx), F32), pltpu.VMEM((sl, wx), F32),
        pltpu.VMEM((sl, wx), F32), pltpu.VMEM((sl, wx), F32),
        pltpu.VMEM((SSD_N, wx), F32), pltpu.VMEM((SSD_N, wx), F32),
    ]
    vmem = (2 * sl * (3 * wx + 3 * wl) * 4 + sl * (8 * wx + 4 * wl) * 4 + 8 * sl * wx * 4
            + (8 << 20))
    return pl.pallas_call(
        functools.partial(_ssd_kernel, seq_len=sl, latent=latent, n_alias=len(carried),
                          n_sub=n_sub),
        grid=(nb, groups // n_sub),
        in_specs=in_specs + [_ANY_SPEC] * len(carried),
        out_specs=out_specs,
        out_shape=out_shape,
        input_output_aliases=aliases,
        scratch_shapes=scratch,
        compiler_params=_cparams(("arbitrary", "arbitrary"), vmem),
        name="ssd_latent" if latent else "ssd_context",
    )(*args, *carried)


def _dft_mats(l):
    n = 3 * l // 2
    nh = n // 2
    k = np.arange(nh, dtype=np.int64)[:, None]
    t = np.arange(l, dtype=np.int64)[None, :]
    ang = 2.0 * np.pi * ((k * t) % n).astype(np.float64) / n
    fwd = np.concatenate([np.cos(ang), -np.sin(ang)], axis=0)
    fwd[nh, :] = np.where(np.arange(l) % 2 == 0, 1.0, -1.0)
    pos = (np.arange(l, dtype=np.int64) + l // 2)[:, None]
    kk = np.arange(nh, dtype=np.int64)[None, :]
    ang2 = 2.0 * np.pi * ((pos * kk) % n).astype(np.float64) / n
    inv = np.concatenate([2.0 * np.cos(ang2), -2.0 * np.sin(ang2)], axis=1) / n
    inv[:, 0] = 1.0 / n
    inv[:, nh] = np.where(pos[:, 0] % 2 == 0, 1.0, -1.0) / n

    def split(m):
        hi = m.astype(np.float32).astype(BF16)
        lo = (m - hi.astype(np.float64)).astype(np.float32).astype(BF16)
        return hi, lo

    return split(fwd) + split(inv)


def _hy_feats(l):
    off = jnp.arange(l, dtype=F32) - (l // 2)
    band = HY_FREQ_BASE ** (-jnp.arange(HY_BANDS, dtype=F32) / HY_BANDS)
    ang = off[:, None] * band
    feats = jnp.concatenate([off[:, None] / l, jnp.sin(ang), jnp.cos(ang)], axis=-1)
    return jnp.pad(feats, ((0, 0), (0, LANE - feats.shape[1])))


def _hy_filter_kernel(feats_ref, w1_ref, b1_ref, w2_ref, b2_ref, fr_ref, w3_ref, dl_ref,
                      fh_ref, fl_ref, o_ref, *, seq_len):
    fr = fr_ref[0]
    h1 = jnp.sin(fr[0:1] * (_dot_hp(feats_ref[...], w1_ref[0]) + b1_ref[0]))
    h2 = jnp.sin(fr[1:2] * (_dot_hp(h1, w2_ref[0]) + b2_ref[0]))
    filt = _dot_hp(h2, w3_ref[0])
    off = lax.broadcasted_iota(jnp.int32, filt.shape, 0).astype(F32) - (seq_len // 2)
    dist = jnp.abs(off) * (2.0 / seq_len)
    h = filt * jnp.exp(-dist * dl_ref[...])
    hh, hl = _split_bf16(h, 2)
    fh = fh_ref[...]
    o_ref[0, 0] = _dot(fh, hh) + _dot(fh, hl) + _dot(fl_ref[...], hh)


def _hy_filter_call(seq_len, hw, mats, dims):
    depth, hy = dims.depth, dims.hy_width
    cb = HY_CB
    ncb = hy // cb
    fh, fl = mats[0], mats[1]
    n = fh.shape[0]
    feats = _hy_feats(seq_len)
    deltas = jnp.linspace(HY_MIN_DECAY, HY_MAX_DECAY, hy, dtype=F32).reshape(1, hy)
    vmem = 2 * 2 * (n * seq_len * 2) + 6 * n * cb * 4 + (8 << 20)
    return pl.pallas_call(
        functools.partial(_hy_filter_kernel, seq_len=seq_len),
        grid=(depth, 2, ncb),
        in_specs=[
            pl.BlockSpec((seq_len, LANE), lambda l, o, c: (0, 0)),
            pl.BlockSpec((1, LANE, LANE), lambda l, o, c: (l, 0, 0)),
            pl.BlockSpec((1, 1, LANE), lambda l, o, c: (l, 0, 0)),
            pl.BlockSpec((1, LANE, LANE), lambda l, o, c: (l, 0, 0)),
            pl.BlockSpec((1, 1, LANE), lambda l, o, c: (l, 0, 0)),
            pl.BlockSpec((1, 2, LANE), lambda l, o, c: (l, 0, 0)),
            pl.BlockSpec((1, LANE, cb), lambda l, o, c: (l, 0, o * ncb + c)),
            pl.BlockSpec((1, cb), lambda l, o, c: (0, c)),
            pl.BlockSpec((n, seq_len), lambda l, o, c: (0, 0)),
            pl.BlockSpec((n, seq_len), lambda l, o, c: (0, 0)),
        ],
        out_specs=pl.BlockSpec((1, 1, n, cb), lambda l, o, c: (l, o, 0, c)),
        out_shape=jax.ShapeDtypeStruct((depth, 2, n, hy), F32),
        compiler_params=_cparams(("arbitrary",) * 3, vmem),
        name="hyena_filter_spectrum",
    )(feats, hw["w1"], hw["b1"], hw["w2"], hw["b2"], hw["freq"], hw["w3"], deltas, fh, fl)


def _hy_conv_kernel(*refs, seq_len, n_seq, n_alias):
    (x1_ref, x2_ref, v_ref, cw1_ref, cb1_ref, cw2_ref, cb2_ref, cwv_ref, cbv_ref,
     hf_ref, bias_ref, fh_ref, gh_ref) = refs[:13]
    o_ref = refs[13 + n_alias]
    nh = fh_ref.shape[0] // 2
    cb = o_ref.shape[1]
    t = lax.broadcasted_iota(jnp.int32, (seq_len, cb), 0)
    first = t == 0
    last = t == seq_len - 1
    bin0 = lax.broadcasted_iota(jnp.int32, (nh, cb), 0) == 0
    fh = fh_ref[...]
    gh = gh_ref[...]
    rows = [slice(s * seq_len, (s + 1) * seq_len) for s in range(n_seq)]
    gates = [(_conv3_rows(x1_ref[r, :], cw1_ref, cb1_ref, first, last),
              _conv3_rows(x2_ref[r, :], cw2_ref, cb2_ref, first, last)) for r in rows]
    zz = [_conv3_rows(v_ref[r, :], cwv_ref, cbv_ref, first, last) for r in rows]
    for o in range(2):
        hr, hi = hf_ref[0, o, :nh, :], hf_ref[0, o, nh:, :]
        uf = []
        for z in zz:
            zh, zl = _split_bf16(z, 2)
            uf.append(_dot(fh, zh) + _dot(fh, zl))
        ys = []
        for u in uf:
            ur, ui = u[:nh], u[nh:]
            ii = ui * hi
            yr = ur * hr - jnp.where(bin0, 0.0, ii)
            yi = jnp.where(bin0, ii, ur * hi + ui * hr)
            yh, yl = _split_bf16(jnp.concatenate([yr, yi], axis=0), 2)
            ys.append(_dot(gh, yh) + _dot(gh, yl))
        zz = [g[o] * (y + z * bias_ref[o:o + 1, :]) for g, y, z in zip(gates, ys, zz)]
    for r, z in zip(rows, zz):
        o_ref[r, :] = z


def _hy_conv_call(p, hf, lw, mats, dims, layer, latent, carried):
    hy = dims.hy_width
    cb = HY_CB
    ncb = hy // cb
    if latent:
        nb, sl, row_start = dims.b_lat, dims.l_lat, dims.n_ctx
    else:
        nb, sl, row_start = dims.b_ctx, dims.l_ctx, 0
    n_seq = max(n for n in (8, 4, 2, 1) if nb % n == 0 and (n == 1 or n * sl <= HY_ROWS_PER_STEP))
    rows = n_seq * sl
    row_off = row_start // rows
    n = mats[0].shape[0]
    const = lambda c, b: (0, 0)
    in_specs = [pl.BlockSpec((rows, cb), lambda c, b, k=k: (b + row_off, k * ncb + c)) for k in range(3)]
    for k in range(3):
        in_specs += [pl.BlockSpec((3, cb), lambda c, b, k=k: (0, k * ncb + c)),
                     pl.BlockSpec((1, cb), lambda c, b, k=k: (0, k * ncb + c))]
    once = pl.Buffered(1)
    in_specs += [
        pl.BlockSpec((1, 2, n, cb), lambda c, b: (layer, 0, 0, c), pipeline_mode=once),
        pl.BlockSpec((2, cb), lambda c, b: (0, c)),
        pl.BlockSpec((n, sl), const, pipeline_mode=once),
        pl.BlockSpec((sl, n), const, pipeline_mode=once),
    ]
    cw, cbias = lw["hy_conv_w"], lw["hy_conv_b"]
    vmem = (2 * (n * sl * 2) + 2 * n * cb * 4 + 2 * 4 * rows * cb * 4
            + n_seq * (6 * n + 8 * sl) * cb * 4 + (8 << 20))
    return pl.pallas_call(
        functools.partial(_hy_conv_kernel, seq_len=sl, n_seq=n_seq, n_alias=len(carried)),
        grid=(ncb, nb // n_seq),
        in_specs=in_specs + [_ANY_SPEC] * len(carried),
        out_specs=pl.BlockSpec((rows, cb), lambda c, b: (b + row_off, c)),
        out_shape=jax.ShapeDtypeStruct((dims.n_ctx + dims.n_lat, hy), F32),
        input_output_aliases={13 + i: i for i in range(len(carried))},
        compiler_params=_cparams(("arbitrary", "arbitrary"), vmem),
        name="hyena_conv_latent" if latent else "hyena_conv_context",
    )(p, p, p, cw, cbias, cw, cbias, cw, cbias, hf, lw["hy_bias"], mats[0], mats[2], *carried)


class _Offsets:
    def __init__(self, att_w, ssd_w, groups, hy_w):
        self.q, self.k, self.v = 0, att_w, 2 * att_w
        self.z = 3 * att_w
        self.x = self.z + ssd_w
        self.b = self.x + ssd_w
        self.c = self.b + groups * SSD_N
        self.dt = self.c + groups * SSD_N
        used = self.dt + groups * LANE
        self.total = PROJ_COL_ALIGN * ((used + PROJ_COL_ALIGN - 1) // PROJ_COL_ALIGN)


class _Dims:
    pass


def _make_dims(x_prompt, x_sample, cache_k, state_ssd_fwd, w_in, hy_norm_g, ffn_w_down):
    d = _Dims()
    d.b_ctx, d.l_ctx, d.d_model = x_prompt.shape
    d.b_lat, d.l_lat, _ = x_sample.shape
    d.n_ctx = d.b_ctx * d.l_ctx
    d.n_lat = d.b_lat * d.l_lat
    d.depth = w_in.shape[0]
    d.att_heads = cache_k.shape[3]
    d.att_width = d.att_heads * V_DIM
    d.ssd_heads = state_ssd_fwd.shape[2]
    d.ssd_width = d.ssd_heads * SSD_P
    d.ssd_groups = d.ssd_heads // SSD_HPG
    d.hy_width = hy_norm_g.shape[-1]
    d.d_ff = ffn_w_down.shape[1]
    d.offs = _Offsets(d.att_width, d.ssd_width, d.ssd_groups, d.hy_width)
    d.tm = d.l_lat
    assert d.l_lat % d.l_ctx == 0 and d.n_ctx % d.tm == 0 and d.l_ctx % 256 == 0
    assert d.l_lat & (d.l_lat - 1) == 0 and d.l_ctx & (d.l_ctx - 1) == 0
    assert d.hy_width % HY_CB == 0 and d.d_ff % FFN_TN == 0
    assert d.ssd_heads % SSD_HPG == 0
    return d


def _rope_tables(seq_len):
    rows = seq_len // GRID_W
    r, col = jnp.meshgrid(jnp.arange(rows), jnp.arange(GRID_W), indexing="ij")
    npair = QK_DIM // 4
    inv = ROPE_BASE ** (-jnp.arange(npair, dtype=F32) / npair)
    ang = jnp.concatenate([r.reshape(-1, 1).astype(F32) * inv, col.reshape(-1, 1).astype(F32) * inv], axis=-1)
    cos, sin = jnp.cos(ang), jnp.sin(ang)
    cos2 = jnp.repeat(cos, 2, axis=-1)
    sin2 = jnp.stack([-sin, sin], axis=-1).reshape(seq_len, QK_DIM)
    return jnp.tile(cos2, (1, 2)), jnp.tile(sin2, (1, 2))


def kernel(x_prompt, x_sample, cache_k, cache_v, state_ssd_fwd, state_ssd_bwd, c, c_ctx, norm1_g, norm2_g, w_ada, b_ada, w_in, att_lambda, att_subln_g, ssd_conv_w, ssd_conv_b, ssd_dt_bias, ssd_a_log, ssd_d, ssd_norm_g, hy_conv_w, hy_conv_b, hy_w1, hy_b1, hy_w2, hy_b2, hy_freq, hy_w3, hy_bias, hy_norm_g, w_out, ffn_w_up, ffn_conv_w, ffn_conv_b, ffn_w_down, final_norm_g):
    dims = _make_dims(x_prompt, x_sample, cache_k, state_ssd_fwd, w_in, hy_norm_g, ffn_w_down)
    d, depth, offs = dims.d_model, dims.depth, dims.offs
    n_ctx = dims.n_ctx

    mod_rows = 16 * ((1 + dims.b_lat + 15) // 16)
    cond = jnp.zeros((mod_rows, d), F32).at[0].set(c_ctx).at[1:1 + dims.b_lat].set(c)
    mod_all = _ada_call(cond, w_ada, b_ada).reshape(depth, mod_rows, 6, 1, d)

    pad_h = LANE - hy_w1.shape[-1]
    hw = {
        "w1": jnp.pad(hy_w1, ((0, 0), (0, LANE - hy_w1.shape[1]), (0, pad_h))),
        "b1": jnp.pad(hy_b1, ((0, 0), (0, pad_h)))[:, None, :],
        "w2": jnp.pad(hy_w2, ((0, 0), (0, pad_h), (0, pad_h))),
        "b2": jnp.pad(hy_b2, ((0, 0), (0, pad_h)))[:, None, :],
        "freq": jnp.pad(hy_freq, ((0, 0), (0, 0), (0, pad_h))),
        "w3": jnp.pad(hy_w3, ((0, 0), (0, pad_h), (0, 0))),
    }
    mats_ctx = _dft_mats(dims.l_ctx)
    mats_lat = _dft_mats(dims.l_lat)
    hf_ctx = _hy_filter_call(dims.l_ctx, hw, mats_ctx, dims)
    hf_lat = _hy_filter_call(dims.l_lat, hw, mats_lat, dims)

    cos_t, sin_t = _rope_tables(dims.l_lat)
    ck = cache_k.reshape(dims.b_lat, depth, cache_k.shape[2], dims.att_width)
    cv = cache_v.reshape(dims.b_lat, depth, cache_v.shape[2], dims.att_width)
    sf0 = state_ssd_fwd.reshape(dims.b_lat, depth, dims.ssd_width, SSD_N)
    sb0 = state_ssd_bwd.reshape(dims.b_lat, depth, dims.ssd_width, SSD_N)

    x = jnp.concatenate([x_prompt.reshape(n_ctx, d), x_sample.reshape(dims.n_lat, d)], axis=0)
    new_kv, new_st = (), ()

    dt0, dt_cols = offs.dt, 2 * dims.ssd_heads
    pick = np.zeros((dt_cols, dims.ssd_groups * LANE), np.float32)
    for g in range(dims.ssd_groups):
        for direction in range(2):
            for hh in range(SSD_HPG):
                pick[direction * dims.ssd_heads + g * SSD_HPG + hh,
                     g * LANE + direction * SSD_HPG + hh] = 1.0
    regroup = lambda t: jnp.einsum("...k,kn->...n", t, pick, precision=lax.Precision.HIGHEST)
    w_dt = regroup(w_in[:, :, dt0:dt0 + dt_cols])
    pad_cols = offs.total - dt0 - w_dt.shape[2]
    w_proj = jnp.concatenate([w_in[:, :, :dt0].astype(BF16), w_dt.astype(BF16),
                              jnp.zeros((depth, d, pad_cols), BF16)], axis=2)
    w_hy = w_in[:, :, dt0 + dt_cols:].astype(BF16)
    ssd_par = jnp.zeros((depth, 8, dims.ssd_groups * LANE), F32)
    ssd_par = ssd_par.at[:, 0].set(regroup(ssd_dt_bias.reshape(depth, dt_cols)))
    ssd_par = ssd_par.at[:, 1].set(regroup(ssd_a_log.reshape(depth, dt_cols)))
    w_o = w_out.astype(BF16)
    w_up = ffn_w_up.astype(BF16)
    w_down = ffn_w_down.astype(BF16)
    a_w, s_w = dims.att_width, dims.ssd_width
    if a_w % s_w == 0 and (a_w + s_w) % dims.hy_width == 0:
        out_terms = lambda l, att, y_ssd, y_hy: [
            (att, w_o, l, 0), (y_ssd, w_o, l, a_w // s_w), (y_hy, w_o, l, (a_w + s_w) // dims.hy_width)]
    else:
        out_terms = lambda l, att, y_ssd, y_hy: [
            (att, w_o[l:l + 1, :a_w], 0, 0), (y_ssd, w_o[l:l + 1, a_w:a_w + s_w], 0, 0),
            (y_hy, w_o[l:l + 1, a_w + s_w:], 0, 0)]

    for l in range(depth):
        lam_init = 0.8 - 0.6 * math.exp(-0.3 * l)
        mod = mod_all[l]
        lw = {
            "ssd_conv_w": ssd_conv_w[l], "ssd_conv_b": ssd_conv_b[l].reshape(1, -1),
            "ssd_par": ssd_par,
            "ssd_d": jnp.repeat(ssd_d[l], SSD_P).reshape(1, -1),
            "hy_conv_w": hy_conv_w[l], "hy_conv_b": hy_conv_b[l].reshape(1, -1), "hy_bias": hy_bias[l],
        }

        h = _norm_mod_call(x, norm1_g[l], mod, 0, 1, dims)
        p = _mm_call(h, w_proj, l, dims)
        p_hy = _mm_call(h, w_hy, l, dims)

        att, new_k, new_v = _attn_call(p, att_lambda[l], att_subln_g[l], dims, lam_init, False, l, new_kv)
        new_kv = (new_k, new_v)
        att = _attn_call(p, att_lambda[l], att_subln_g[l], dims, lam_init, True, l, (att,),
                         cache_k=ck, cache_v=cv, cos=cos_t, sin=sin_t)
        ys, new_sf, new_sb = _ssd_call(p, lw, dims, False, l, new_st)
        new_st = (new_sf, new_sb)
        ys = _ssd_call(p, lw, dims, True, l, (ys,), init_f=sf0, init_b=sb0)
        zz = _hy_conv_call(p_hy, hf_ctx, lw, mats_ctx, dims, l, False, ())
        zz = _hy_conv_call(p_hy, hf_lat, lw, mats_lat, dims, l, True, (zz,))

        y_ssd = _norm_call(ys, ssd_norm_g[l], BF16, name="ssd_norm")
        y_hy = _norm_call(zz, hy_norm_g[l], BF16, name="hyena_norm")
        x = _mm_res_call(out_terms(l, att, y_ssd, y_hy), x, mod, 2, dims,
                         (1024, 768, 512, 384, 256, 128))

        h2 = _norm_mod_call(x, norm2_g[l], mod, 3, 4, dims)
        act = _ffn_up_call(h2, w_up, l, ffn_conv_w[l], ffn_conv_b[l], dims)
        x = _mm_res_call([(act, w_down, l, 0)], x, mod, 5, dims, (DOWN_TILE, 256, 128),
                         tm=min(DOWN_TILE, dims.tm))

    y_prompt = _norm_call(x, final_norm_g, F32, 0, n_ctx, name="final_norm").reshape(x_prompt.shape)
    y_sample = _norm_call(x, final_norm_g, F32, n_ctx, dims.n_lat, name="final_norm").reshape(x_sample.shape)
    kv_shape = (dims.b_ctx, depth, dims.l_ctx, dims.att_heads, V_DIM)
    st_shape = (dims.b_ctx, depth, dims.ssd_heads, SSD_P, SSD_N)
    return (y_prompt, y_sample, new_kv[0].reshape(kv_shape), new_kv[1].reshape(kv_shape),
            new_st[0].reshape(st_shape), new_st[1].reshape(st_shape))
```
